```python
import jax
import jax.numpy as jnp
from jax import lax
import numpy as np

D_MODEL = 2048
BATCH = 8
SEQ = 4096
DEPTH = 4

CTX_LEN = 256
GRID_W = 64
CONV_CH = 512
CONV_WIDTH = 31
MLA_HEADS = 8
MLA_Q_RANK = 512
MLA_KV_RANK = 256
MLA_NOPE = 128
MLA_ROPE = 64
MLA_V = 128
GQA_HEADS = 8
GQA_KV_HEADS = 2
GQA_HEAD_DIM = 64
WINDOW = 128
BLOCK = 128
N_BRANCH = 3
D_FF = -(-8 * D_MODEL // (3 * 256)) * 256
ROPE_DIM = 64
ROPE_BASE = 10000.0
EPS = 1e-6
NEG_INF = -1e30

COL_MLA_KV = 0
COL_MLA_KR = COL_MLA_KV + MLA_KV_RANK
COL_GQA_K = COL_MLA_KR + MLA_ROPE
COL_GQA_V = COL_GQA_K + GQA_KV_HEADS * GQA_HEAD_DIM
KV_COLS = COL_GQA_V + GQA_KV_HEADS * GQA_HEAD_DIM
COL_MLA_Q = KV_COLS
COL_GQA_Q = COL_MLA_Q + MLA_Q_RANK
COL_CONV = COL_GQA_Q + GQA_HEADS * GQA_HEAD_DIM
COL_GATE = COL_CONV + 2 * CONV_CH
IN_COLS = COL_GATE + N_BRANCH * D_MODEL

kernel_name = 'hybrid_conv_mla_swa_dit_block'


def _rmsnorm(x, g):
    xf = x.astype(jnp.float32)
    y = xf * lax.rsqrt(jnp.mean(xf * xf, axis=-1, keepdims=True) + EPS)
    return (y * g.astype(jnp.float32)).astype(x.dtype)


def _layernorm(x, g, b):
    xf = x.astype(jnp.float32)
    xc = xf - jnp.mean(xf, axis=-1, keepdims=True)
    y = xc * lax.rsqrt(jnp.mean(xc * xc, axis=-1, keepdims=True) + EPS)
    return (y * g.astype(jnp.float32) + b.astype(jnp.float32)).astype(x.dtype)


def _modulate(h, shift, scale):
    return h * (1 + scale) + shift


def _axial_rope_tables(length):
    rows = length // GRID_W
    row = jnp.repeat(jnp.arange(rows), GRID_W).astype(jnp.float32)
    col = jnp.tile(jnp.arange(GRID_W), rows).astype(jnp.float32)
    n_freq = ROPE_DIM // 4
    inv_freq = ROPE_BASE ** (-jnp.arange(n_freq, dtype=jnp.float32) / n_freq)
    a_row = row[:, None] * inv_freq[None, :]
    a_col = col[:, None] * inv_freq[None, :]
    ang = jnp.concatenate([a_row, a_row, a_col, a_col], axis=-1)
    return jnp.cos(ang), jnp.sin(ang)


def _apply_axial_rope(x, rope):
    cos, sin = rope
    xs = x.reshape(x.shape[:-1] + (2, 2, ROPE_DIM // 4))
    rot = jnp.concatenate([-xs[..., 1:, :], xs[..., :1, :]], axis=-2).reshape(x.shape)
    return x * cos[None, :, None, :].astype(x.dtype) + rot * sin[None, :, None, :].astype(x.dtype)


def _attend(qb, keys, values, masks, sink):
    scale = qb.shape[-1] ** -0.5
    scores = []
    for k, m in zip(keys, masks):
        s = jnp.einsum('bqgrd,bkgd->bgrqk', qb, k).astype(jnp.float32) * scale
        if m is not None:
            s = jnp.where(m, s, NEG_INF)
        scores.append(s)
    if sink is not None:
        scores.append(jnp.broadcast_to(sink.astype(jnp.float32)[None, :, :, None, None], scores[0].shape[:-1] + (1,)))
    p = jax.nn.softmax(jnp.concatenate(scores, axis=-1), axis=-1)
    out = None
    off = 0
    for v in values:
        n = v.shape[1]
        o = jnp.einsum('bgrqk,bkgd->bqgrd', p[..., off:off + n].astype(v.dtype), v)
        out = o if out is None else out + o
        off += n
    return out


def _latent_attention(q, k, v, k_ctx, v_ctx, sink, window):
    B, L, H, dk = q.shape
    G = k.shape[2]
    R = H // G
    dv = v.shape[-1]
    n_blk = L // BLOCK
    q_blocks = jnp.moveaxis(q.reshape(B, n_blk, BLOCK, G, R, dk), 1, 0)
    sink_gr = None if sink is None else sink.reshape(G, R)
    if window is not None:
        pad = ((0, 0), (window, window), (0, 0), (0, 0))
        k_pad = jnp.pad(k, pad)
        v_pad = jnp.pad(v, pad)
        span = BLOCK + 2 * window

    def one_block(args):
        qb, blk = args
        start = blk * BLOCK
        if window is None:
            kb, vb, mask = k, v, None
        else:
            kb = lax.dynamic_slice_in_dim(k_pad, start, span, axis=1)
            vb = lax.dynamic_slice_in_dim(v_pad, start, span, axis=1)
            q_pos = start + jnp.arange(BLOCK)
            k_pos = start - window + jnp.arange(span)
            mask = ((jnp.abs(q_pos[:, None] - k_pos[None, :]) <= window)
                    & (k_pos >= 0)[None, :] & (k_pos < L)[None, :])
        return _attend(qb, [kb, k_ctx], [vb, v_ctx], [mask, None], sink_gr)

    out = lax.map(one_block, (q_blocks, jnp.arange(n_blk)))
    return jnp.moveaxis(out, 0, 1).reshape(B, L, H, dv)


def _context_attention(q, k, v, sink):
    B, C, H, dk = q.shape
    G = k.shape[2]
    sink_gr = None if sink is None else sink.reshape(G, H // G)
    o = _attend(q.reshape(B, C, G, H // G, dk), [k], [v], [None], sink_gr)
    return o.reshape(B, C, H, v.shape[-1])


def _mla_q(p, g_q_a, w_q_up, rope):
    B, L = p.shape[:2]
    q = (_rmsnorm(p[..., COL_MLA_Q:COL_MLA_Q + MLA_Q_RANK], g_q_a) @ w_q_up).reshape(B, L, MLA_HEADS, MLA_NOPE + MLA_ROPE)
    q_nope, q_rope = q[..., :MLA_NOPE], q[..., MLA_NOPE:]
    if rope is not None:
        q_rope = _apply_axial_rope(q_rope, rope)
    return jnp.concatenate([q_nope, q_rope], axis=-1)


def _mla_kv(p, g_kv_a, w_kv_up, rope):
    B, L = p.shape[:2]
    c_kv = p[..., COL_MLA_KV:COL_MLA_KV + MLA_KV_RANK]
    k_rope = p[..., COL_MLA_KR:COL_MLA_KR + MLA_ROPE][:, :, None, :]
    if rope is not None:
        k_rope = _apply_axial_rope(k_rope, rope)
    kv = (_rmsnorm(c_kv, g_kv_a) @ w_kv_up).reshape(B, L, MLA_HEADS, MLA_NOPE + MLA_V)
    k = jnp.concatenate([kv[..., :MLA_NOPE], jnp.broadcast_to(k_rope, (B, L, MLA_HEADS, MLA_ROPE))], axis=-1)
    return k, kv[..., MLA_NOPE:]


def _gqa_q(p, rope):
    B, L = p.shape[:2]
    q = p[..., COL_GQA_Q:COL_GQA_Q + GQA_HEADS * GQA_HEAD_DIM].reshape(B, L, GQA_HEADS, GQA_HEAD_DIM)
    return q if rope is None else _apply_axial_rope(q, rope)


def _gqa_kv(p, rope):
    B, L = p.shape[:2]
    k = p[..., COL_GQA_K:COL_GQA_V].reshape(B, L, GQA_KV_HEADS, GQA_HEAD_DIM)
    v = p[..., COL_GQA_V:KV_COLS].reshape(B, L, GQA_KV_HEADS, GQA_HEAD_DIM)
    if rope is not None:
        k = _apply_axial_rope(k, rope)
    return k, v


def _conv_branch(p, conv_w, conv_b, ln_g, ln_b, w_conv_out):
    a = p[..., COL_CONV:COL_CONV + CONV_CH]
    b = p[..., COL_CONV + CONV_CH:COL_GATE]
    u = a * jax.nn.sigmoid(b)
    u = lax.conv_general_dilated(u, conv_w[:, None, :], window_strides=(1,),
                                 padding=[(CONV_WIDTH // 2, CONV_WIDTH // 2)],
                                 dimension_numbers=('NWC', 'WIO', 'NWC'),
                                 feature_group_count=CONV_CH) + conv_b
    u = jax.nn.silu(_layernorm(u, ln_g, ln_b))
    return u @ w_conv_out


def _merge(p, y_conv, y_mla, y_gqa, w_out):
    g = jax.nn.sigmoid(p[..., COL_GATE:IN_COLS]).reshape(p.shape[:-1] + (N_BRANCH, D_MODEL))
    y = g[..., 0, :] * y_conv + g[..., 1, :] * y_mla + g[..., 2, :] * y_gqa
    return y @ w_out


def _swiglu(h, w_in, w_out):
    u = h @ w_in
    return (jax.nn.silu(u[..., :D_FF]) * u[..., D_FF:]) @ w_out


def _mixing(p, ctx_kv, rope, lp):
    mk_c, mv_c, gk_c, gv_c = ctx_kv
    B, L = p.shape[:2]
    mq = _mla_q(p, lp['g_q_a'], lp['w_q_up'], rope)
    gq = _gqa_q(p, rope)
    if rope is None:
        o_mla = _context_attention(mq, mk_c, mv_c, None)
        o_gqa = _context_attention(gq, gk_c, gv_c, lp['gqa_sink'])
    else:
        mk, mv = _mla_kv(p, lp['g_kv_a'], lp['w_kv_up'], rope)
        gk, gv = _gqa_kv(p, rope)
        o_mla = _latent_attention(mq, mk, mv, mk_c, mv_c, None, None)
        o_gqa = _latent_attention(gq, gk, gv, gk_c, gv_c, lp['gqa_sink'], WINDOW)
    y_mla = o_mla.reshape(B, L, MLA_HEADS * MLA_V) @ lp['w_mla_out']
    y_gqa = o_gqa.reshape(B, L, GQA_HEADS * GQA_HEAD_DIM) @ lp['w_gqa_out']
    y_conv = _conv_branch(p, lp['conv_w'], lp['conv_b'], lp['conv_ln_g'], lp['conv_ln_b'], lp['w_conv_out'])
    return _merge(p, y_conv, y_mla, y_gqa, lp['w_out'])


def setup_inputs(seed: int = 0) -> dict:
    key = jax.random.key(seed)
    ks = jax.random.split(key, 25)

    def nrm(k, shape, scale):
        return jax.random.normal(k, shape, jnp.float32) * scale

    def gain(k, shape):
        return 1.0 + nrm(k, shape, 0.05)

    return {
        'x': nrm(ks[0], (BATCH, SEQ, D_MODEL), 1.0),
        'c': nrm(ks[1], (BATCH, D_MODEL), 1.0),
        'ctx': nrm(ks[2], (BATCH, CTX_LEN, D_MODEL), 1.0),
        'c_ctx': nrm(ks[3], (D_MODEL,), 1.0),
        'w_ada': nrm(ks[4], (DEPTH, D_MODEL, 6 * D_MODEL), 0.5 * D_MODEL ** -0.5),
        'b_ada': nrm(ks[5], (DEPTH, 6 * D_MODEL), 0.01),
        'g_mix': gain(ks[6], (DEPTH, D_MODEL)),
        'w_in': nrm(ks[7], (DEPTH, D_MODEL, IN_COLS), D_MODEL ** -0.5),
        'conv_w': nrm(ks[8], (DEPTH, CONV_WIDTH, CONV_CH), CONV_WIDTH ** -0.5),
        'conv_b': nrm(ks[9], (DEPTH, CONV_CH), 0.01),
        'conv_ln_g': gain(ks[10], (DEPTH, CONV_CH)),
        'conv_ln_b': nrm(ks[11], (DEPTH, CONV_CH), 0.01),
        'w_conv_out': nrm(ks[12], (DEPTH, CONV_CH, D_MODEL), CONV_CH ** -0.5),
        'g_q_a': gain(ks[13], (DEPTH, MLA_Q_RANK)),
        'w_q_up': nrm(ks[14], (DEPTH, MLA_Q_RANK, MLA_HEADS * (MLA_NOPE + MLA_ROPE)), MLA_Q_RANK ** -0.5),
        'g_kv_a': gain(ks[15], (DEPTH, MLA_KV_RANK)),
        'w_kv_up': nrm(ks[16], (DEPTH, MLA_KV_RANK, MLA_HEADS * (MLA_NOPE + MLA_V)), MLA_KV_RANK ** -0.5),
        'w_mla_out': nrm(ks[17], (DEPTH, MLA_HEADS * MLA_V, D_MODEL), (MLA_HEADS * MLA_V) ** -0.5),
        'gqa_sink': nrm(ks[18], (DEPTH, GQA_HEADS), 0.5),
        'w_gqa_out': nrm(ks[19], (DEPTH, GQA_HEADS * GQA_HEAD_DIM, D_MODEL), (GQA_HEADS * GQA_HEAD_DIM) ** -0.5),
        'w_out': nrm(ks[20], (DEPTH, D_MODEL, D_MODEL), D_MODEL ** -0.5),
        'g_ffn': gain(ks[21], (DEPTH, D_MODEL)),
        'w_ffn_in': nrm(ks[22], (DEPTH, D_MODEL, 2 * D_FF), D_MODEL ** -0.5),
        'w_ffn_out': nrm(ks[23], (DEPTH, D_FF, D_MODEL), D_FF ** -0.5),
        'g_final': gain(ks[24], (D_MODEL,)),
    }


def reference(x, c, ctx, c_ctx, w_ada, b_ada, g_mix, w_in, conv_w, conv_b, conv_ln_g, conv_ln_b,
              w_conv_out, g_q_a, w_q_up, g_kv_a, w_kv_up, w_mla_out, gqa_sink, w_gqa_out, w_out,
              g_ffn, w_ffn_in, w_ffn_out, g_final):
    L = x.shape[1]
    rope = _axial_rope_tables(L)
    xc = ctx
    for l in range(DEPTH):
        last = l == DEPTH - 1
        lp = {'g_q_a': g_q_a[l], 'w_q_up': w_q_up[l], 'g_kv_a': g_kv_a[l], 'w_kv_up': w_kv_up[l],
              'w_mla_out': w_mla_out[l], 'gqa_sink': gqa_sink[l], 'w_gqa_out': w_gqa_out[l],
              'conv_w': conv_w[l], 'conv_b': conv_b[l], 'conv_ln_g': conv_ln_g[l], 'conv_ln_b': conv_ln_b[l],
              'w_conv_out': w_conv_out[l], 'w_out': w_out[l]}
        mod = (jax.nn.silu(c) @ w_ada[l] + b_ada[l])[:, None, :]
        mod_c = (jax.nn.silu(c_ctx) @ w_ada[l] + b_ada[l])[None, None, :]
        sh1, sc1, gt1, sh2, sc2, gt2 = jnp.split(mod, 6, axis=-1)
        csh1, csc1, cgt1, csh2, csc2, cgt2 = jnp.split(mod_c, 6, axis=-1)
        w_in_l = w_in[l]
        p = _modulate(_rmsnorm(x, g_mix[l]), sh1, sc1) @ w_in_l
        hc = _modulate(_rmsnorm(xc, g_mix[l]), csh1, csc1)
        pc = hc @ (w_in_l[:, :KV_COLS] if last else w_in_l)
        ctx_kv = _mla_kv(pc, g_kv_a[l], w_kv_up[l], None) + _gqa_kv(pc, None)
        x = x + gt1 * _mixing(p, ctx_kv, rope, lp)
        if not last:
            xc = xc + cgt1 * _mixing(pc, ctx_kv, None, lp)
        x = x + gt2 * _swiglu(_modulate(_rmsnorm(x, g_ffn[l]), sh2, sc2), w_ffn_in[l], w_ffn_out[l])
        if not last:
            xc = xc + cgt2 * _swiglu(_modulate(_rmsnorm(xc, g_ffn[l]), csh2, csc2), w_ffn_in[l], w_ffn_out[l])
    return _rmsnorm(x, g_final)
```

```python
import functools

import jax
import jax.numpy as jnp
from jax import lax
from jax.experimental import pallas as pl
from jax.experimental.pallas import tpu as pltpu

F32 = jnp.float32
BF16 = jnp.bfloat16

GRID_W = 64
CONV_CH = 512
CONV_WIDTH = 31
MLA_HEADS = 8
MLA_Q_RANK = 512
MLA_KV_RANK = 256
MLA_NOPE = 128
MLA_ROPE = 64
MLA_V = 128
GQA_HEADS = 8
GQA_KV_HEADS = 2
GQA_HEAD_DIM = 64
WINDOW = 128
N_BRANCH = 3
ROPE_DIM = 64
ROPE_BASE = 10000.0
EPS = 1e-6
NEG_INF = -1e30

COL_MLA_KV = 0
COL_MLA_KR = COL_MLA_KV + MLA_KV_RANK
COL_GQA_K = COL_MLA_KR + MLA_ROPE
COL_GQA_V = COL_GQA_K + GQA_KV_HEADS * GQA_HEAD_DIM
KV_COLS = COL_GQA_V + GQA_KV_HEADS * GQA_HEAD_DIM
COL_MLA_Q = KV_COLS
COL_GQA_Q = COL_MLA_Q + MLA_Q_RANK
COL_CONV = COL_GQA_Q + GQA_HEADS * GQA_HEAD_DIM
COL_GATE = COL_CONV + 2 * CONV_CH

LANES = 128
MLA_PAD = 256
VMEM_LIMIT = 56 * 1024 * 1024

P_MLA_Q = 0
P_GQA_Q = 512
P_CONV_A = 1024
P_CONV_B = 1536
P_CKV = 2048
P_KR = 2304
P_GQA_K = 2432
P_GQA_V = 2560
P_GATE = 3072
P_COLS = P_GATE + 0


def _cparams(*sem):
    return pltpu.CompilerParams(dimension_semantics=sem, vmem_limit_bytes=VMEM_LIMIT)


def _for_chunks(n_rows, rc, fn):
    n = n_rows // rc
    if n == 1:
        fn(0)
        return

    def body(i, carry):
        fn(pl.multiple_of(i * rc, rc))
        return carry

    lax.fori_loop(0, n, body, 0)


def _sigmoid(x):
    return 1.0 / (1.0 + jnp.exp(-x))


def _rope(x, cos, sin_up, sin_dn):
    return x * cos + pltpu.roll(x, LANES - 16, 1) * sin_up + pltpu.roll(x, 16, 1) * sin_dn


def _dot(a, b):
    return jnp.dot(a, b, preferred_element_type=F32)


def _dot_nt(a, b):
    return lax.dot_general(a, b, (((1,), (1,)), ((), ())), preferred_element_type=F32)


def _ada_kernel(c_ref, w_ref, b_ref, o_ref):
    c = c_ref[...]
    s = c * _sigmoid(c)
    o_ref[...] = jnp.dot(s, w_ref[...], preferred_element_type=F32,
                         precision=lax.Precision.HIGHEST) + b_ref[...]


def _ada(cc, w_ada, b_ada):
    depth, d, n = w_ada.shape
    rows = cc.shape[0]
    tn = 1024
    return pl.pallas_call(
        _ada_kernel,
        grid=(depth, n // tn),
        in_specs=[
            pl.BlockSpec((rows, d), lambda l, j: (0, 0)),
            pl.BlockSpec((None, d, tn), lambda l, j: (l, 0, j)),
            pl.BlockSpec((None, 1, tn), lambda l, j: (l, 0, j)),
        ],
        out_specs=pl.BlockSpec((None, rows, tn), lambda l, j: (l, 0, j)),
        out_shape=jax.ShapeDtypeStruct((depth, rows, n), F32),
        compiler_params=_cparams("parallel", "parallel"),
        name="ada_mod",
    )(cc, w_ada, b_ada.reshape(depth, 1, n))


def _norm_mod_store(x_ref, g_ref, sh_ref, sc_ref, h_ref, rc):
    gs = g_ref[...] * (1.0 + sc_ref[...])
    sh = sh_ref[...]

    def chunk(r0):
        x = x_ref[pl.ds(r0, rc), :]
        ms = jnp.mean(x * x, axis=-1, keepdims=True)
        h_ref[pl.ds(r0, rc), :] = (x * lax.rsqrt(ms + EPS) * gs + sh).astype(BF16)

    _for_chunks(x_ref.shape[0], rc, chunk)


def _in_proj_kernel(x_ref, g_ref, sh_ref, sc_ref, w_ref, o_ref, h_ref, *, n_plain, rc):
    j = pl.program_id(1)

    @pl.when(j == 0)
    def _():
        _norm_mod_store(x_ref, g_ref, sh_ref, sc_ref, h_ref, rc)

    acc = _dot(h_ref[...], w_ref[...])

    @pl.when(j < n_plain)
    def _():
        o_ref[...] = acc.astype(BF16)

    @pl.when(j >= n_plain)
    def _():
        o_ref[...] = _sigmoid(acc).astype(BF16)


def _ffn_in_kernel(x_ref, g_ref, sh_ref, sc_ref, w1_ref, w2_ref, o_ref, h_ref, *, rc):
    @pl.when(pl.program_id(1) == 0)
    def _():
        _norm_mod_store(x_ref, g_ref, sh_ref, sc_ref, h_ref, rc)

    h = h_ref[...]
    u1 = _dot(h, w1_ref[...])
    u2 = _dot(h, w2_ref[...])
    o_ref[...] = (u1 * _sigmoid(u1) * u2).astype(BF16)


def _mm_res_kernel(a_ref, w_ref, x_ref, gt_ref, o_ref):
    o_ref[...] = x_ref[...] + gt_ref[...] * _dot(a_ref[...], w_ref[...])


def _merge_kernel(oc_ref, om_ref, og_ref, wc_ref, wm_ref, wg_ref, g0_ref, g1_ref, g2_ref, y_ref):
    y = g0_ref[...].astype(F32) * _dot(oc_ref[...], wc_ref[...])
    y += g1_ref[...].astype(F32) * _dot(om_ref[...], wm_ref[...])
    y += g2_ref[...].astype(F32) * _dot(og_ref[...], wg_ref[...])
    y_ref[...] = y.astype(BF16)


def _rms_bf16(a, g):
    ms = jnp.mean(a * a, axis=-1, keepdims=True)
    return (a * lax.rsqrt(ms + EPS) * g).astype(BF16)


def _qup_kernel(pq_ref, g_ref, wq_ref, cos_ref, su_ref, sd_ref, q_ref, qg_ref, *, rc, s_mla, s_gqa):
    g = g_ref[...]

    def chunk(r0):
        rows = pl.ds(r0, rc)
        n = _rms_bf16(pq_ref[rows, 0:MLA_Q_RANK].astype(F32), g)
        q = _dot(n, wq_ref[...])
        cos = cos_ref[rows, :]
        su = su_ref[rows, :]
        sd = sd_ref[rows, :]
        for h in range(MLA_HEADS):
            c0 = h * MLA_PAD
            q_ref[rows, c0:c0 + LANES] = (q[:, c0:c0 + LANES] * s_mla).astype(BF16)
            r = _rope(q[:, c0 + LANES:c0 + 2 * LANES], cos, su, sd)
            q_ref[rows, c0 + LANES:c0 + 2 * LANES] = (r * s_mla).astype(BF16)
        for t in range(GQA_HEADS * GQA_HEAD_DIM // LANES):
            gq = pq_ref[rows, MLA_Q_RANK + t * LANES:MLA_Q_RANK + (t + 1) * LANES].astype(F32)
            qg_ref[rows, t * LANES:(t + 1) * LANES] = (_rope(gq, cos, su, sd) * s_gqa).astype(BF16)

    _for_chunks(pq_ref.shape[0], rc, chunk)


def _kvup_kernel(ckv_ref, kr_ref, g_ref, wk_ref, wv_ref, cos_ref, su_ref, sd_ref,
                 k_ref, v_ref, kg_ref, *, rc):
    g = g_ref[...]

    def chunk(r0):
        rows = pl.ds(r0, rc)
        n = _rms_bf16(ckv_ref[rows, :].astype(F32), g)
        kn = _dot(n, wk_ref[...])
        v_ref[rows, :] = _dot(n, wv_ref[...]).astype(BF16)
        cos = cos_ref[rows, :]
        su = su_ref[rows, :]
        sd = sd_ref[rows, :]
        kr = _rope(kr_ref[rows, 0:LANES].astype(F32), cos, su, sd).astype(BF16)
        for h in range(MLA_HEADS):
            c0 = h * MLA_PAD
            k_ref[rows, c0:c0 + LANES] = kn[:, h * LANES:(h + 1) * LANES].astype(BF16)
            k_ref[rows, c0 + LANES:c0 + 2 * LANES] = kr
        kg_ref[rows, :] = _rope(kr_ref[rows, LANES:2 * LANES].astype(F32), cos, su, sd).astype(BF16)

    _for_chunks(ckv_ref.shape[0], rc, chunk)


def _mla_attn_kernel(*refs, tk, n_chunks):
    if n_chunks:
        q_ref, kc_ref, vc_ref, kl_ref, vl_ref, o_ref = refs
    else:
        q_ref, kc_ref, vc_ref, _, o_ref = refs
    q = q_ref[...]
    s = _dot_nt(q, kc_ref[...])
    m = jnp.max(s, axis=-1, keepdims=True)
    p = jnp.exp(s - m)
    l = jnp.sum(p, axis=-1, keepdims=True)
    acc = _dot(p.astype(BF16), vc_ref[...])

    if n_chunks:
        def body(c, carry):
            m, l, acc = carry
            rows = pl.ds(pl.multiple_of(c * tk, tk), tk)
            s = _dot_nt(q, kl_ref[rows, :])
            m_new = jnp.maximum(m, jnp.max(s, axis=-1, keepdims=True))
            alpha = jnp.exp(m - m_new)
            p = jnp.exp(s - m_new)
            l = alpha * l + jnp.sum(p, axis=-1, keepdims=True)
            acc = alpha * acc + _dot(p.astype(BF16), vl_ref[rows, :])
            return m_new, l, acc

        m, l, acc = lax.fori_loop(0, n_chunks, body, (m, l, acc))
    o_ref[...] = (acc / l).astype(BF16)


def _gqa_attn_kernel(*refs, seq, n_sub, latent):
    if latent:
        q_ref, kc_ref, vc_ref, sink_ref, kl_ref, vl_ref, o_ref = refs
    else:
        q_ref, kc_ref, vc_ref, sink_ref, _, o_ref = refs
    span = 3 * WINDOW
    dh = GQA_HEAD_DIM
    rep = GQA_HEADS // GQA_KV_HEADS
    tile_q0 = pl.program_id(1) * (n_sub * WINDOW) if latent else 0

    def sub(r0):
        rows = pl.ds(r0, WINDOW)
        if latent:
            q0 = tile_q0 + r0
            start = pl.multiple_of(jnp.clip(q0 - WINDOW, 0, seq - span), WINDOW)
            kl = kl_ref[pl.ds(start, span), :]
            vl = vl_ref[pl.ds(start, span), :]
            qpos = q0 + lax.broadcasted_iota(jnp.int32, (WINDOW, span), 0)
            kpos = start + lax.broadcasted_iota(jnp.int32, (WINDOW, span), 1)
            valid = jnp.abs(qpos - kpos) <= WINDOW
        kc = kc_ref[...]
        vc = vc_ref[...]
        for h in range(GQA_HEADS):
            g = h // rep
            q = q_ref[rows, h * dh:(h + 1) * dh]
            sink = sink_ref[0:1, h:h + 1]
            sc = _dot_nt(q, kc[:, g * dh:(g + 1) * dh])
            m = jnp.maximum(jnp.max(sc, axis=-1, keepdims=True), sink)
            if latent:
                sl = jnp.where(valid, _dot_nt(q, kl[:, g * dh:(g + 1) * dh]), NEG_INF)
                m = jnp.maximum(m, jnp.max(sl, axis=-1, keepdims=True))
            pc = jnp.exp(sc - m)
            den = jnp.sum(pc, axis=-1, keepdims=True) + jnp.exp(sink - m)
            out = _dot(pc.astype(BF16), vc[:, g * dh:(g + 1) * dh])
            if latent:
                pw = jnp.exp(sl - m)
                den = den + jnp.sum(pw, axis=-1, keepdims=True)
                out = out + _dot(pw.astype(BF16), vl[:, g * dh:(g + 1) * dh])
            o_ref[rows, h * dh:(h + 1) * dh] = (out / den).astype(BF16)

    _for_chunks(n_sub * WINDOW, WINDOW, sub)


def _conv_kernel(a_ref, b_ref, w_ref, cb_ref, lg_ref, lb_ref, o_ref, u_ref, win_ref, *, seq, rc):
    halo = 16
    u_ref[0:halo, :] = jnp.zeros((halo, CONV_CH), F32)
    u_ref[halo + seq:2 * halo + seq, :] = jnp.zeros((halo, CONV_CH), F32)

    def fill(r0):
        a = a_ref[pl.ds(r0, 256), :].astype(F32)
        b = b_ref[pl.ds(r0, 256), :].astype(F32)
        u_ref[pl.ds(halo + r0, 256), :] = a * _sigmoid(b)

    _for_chunks(seq, 256, fill)

    cb = cb_ref[...]
    lg = lg_ref[...]
    lb = lb_ref[...]

    half = CONV_CH // 2
    first_tap = halo - CONV_WIDTH // 2

    def conv(r0):
        win_ref[...] = u_ref[pl.ds(r0, rc + 2 * halo), :]
        parts = []
        for c0 in (0, half):
            acc = jnp.zeros((rc, half), F32) + cb[:, c0:c0 + half]
            for s in range(8):
                shifted = win_ref[s:s + rc + 24, c0:c0 + half]
                for a in range(4):
                    k = 8 * a + s - first_tap
                    if 0 <= k < CONV_WIDTH:
                        acc = acc + w_ref[k:k + 1, c0:c0 + half] * shifted[8 * a:8 * a + rc]
            parts.append(acc)
        acc = jnp.concatenate(parts, axis=-1)
        mu = jnp.mean(acc, axis=-1, keepdims=True)
        xc = acc - mu
        var = jnp.mean(xc * xc, axis=-1, keepdims=True)
        y = xc * lax.rsqrt(var + EPS) * lg + lb
        o_ref[pl.ds(r0, rc), :] = (y * _sigmoid(y)).astype(BF16)

    _for_chunks(seq, rc, conv)


def _final_norm_kernel(x_ref, g_ref, o_ref, *, rc):
    g = g_ref[...]

    def chunk(r0):
        x = x_ref[pl.ds(r0, rc), :]
        ms = jnp.mean(x * x, axis=-1, keepdims=True)
        o_ref[pl.ds(r0, rc), :] = x * lax.rsqrt(ms + EPS) * g

    _for_chunks(x_ref.shape[0], rc, chunk)


def _rope_tables(length, extra):
    rows = length // GRID_W
    row = jnp.repeat(jnp.arange(rows), GRID_W).astype(F32)
    col = jnp.tile(jnp.arange(GRID_W), rows).astype(F32)
    n_freq = ROPE_DIM // 4
    inv_freq = ROPE_BASE ** (-jnp.arange(n_freq, dtype=F32) / n_freq)
    a_row = row[:, None] * inv_freq[None, :]
    a_col = col[:, None] * inv_freq[None, :]
    ang = jnp.concatenate([a_row, a_row, a_col, a_col], axis=-1)
    cos = jnp.concatenate([jnp.cos(ang), jnp.ones((extra, ROPE_DIM), F32)], axis=0)
    sin = jnp.concatenate([jnp.sin(ang), jnp.zeros((extra, ROPE_DIM), F32)], axis=0)
    cos = jnp.tile(cos, (1, LANES // ROPE_DIM))
    sin = jnp.tile(sin, (1, LANES // ROPE_DIM))
    first = (jnp.arange(LANES) // 16) % 2 == 0
    sin_up = jnp.where(first[None, :], -sin, 0.0)
    sin_dn = jnp.where(first[None, :], 0.0, sin)
    return cos, sin_up, sin_dn


def _pick_tile(*sizes):
    for t in (1024, 512, 256):
        if all(s % t == 0 for s in sizes):
            return t
    raise ValueError(f"unsupported row counts {sizes}")


def kernel(x, c, ctx, c_ctx, w_ada, b_ada, g_mix, w_in, conv_w, conv_b, conv_ln_g, conv_ln_b, w_conv_out, g_q_a, w_q_up, g_kv_a, w_kv_up, w_mla_out, gqa_sink, w_gqa_out, w_out, g_ffn, w_ffn_in, w_ffn_out, g_final):
    bsz, seq, d = x.shape
    n_ctx = ctx.shape[1]
    depth = w_ada.shape[0]
    d_ff = w_ffn_out.shape[1]
    m_lat = bsz * seq
    m_ctx = bsz * n_ctx
    m_all = m_lat + m_ctx
    assert seq % GRID_W == 0 and n_ctx == 2 * WINDOW and seq % (4 * WINDOW) == 0
    assert bsz + 1 <= 16 and w_in.shape[2] == COL_GATE + N_BRANCH * d

    tm = _pick_tile(seq, m_ctx)
    ts = min(tm, 512)
    tn = 512
    rc = 256
    n_lat_tiles = m_lat // tm
    tiles_per_batch = seq // tm
    p_cols = P_GATE + N_BRANCH * d
    ctx_blk0 = m_lat // n_ctx

    def mod_row(i, tile):
        return jnp.minimum(i // (seq // tile), bsz)

    def cols(a, n):
        return w_in[:, :, a:a + n].astype(BF16)

    def zeros(n):
        return jnp.zeros((depth, d, n), BF16)

    w_in_r = jnp.concatenate([
        cols(COL_MLA_Q, MLA_Q_RANK), cols(COL_GQA_Q, GQA_HEADS * GQA_HEAD_DIM),
        cols(COL_CONV, CONV_CH), cols(COL_CONV + CONV_CH, CONV_CH),
        cols(COL_MLA_KV, MLA_KV_RANK), cols(COL_MLA_KR, MLA_ROPE), zeros(LANES - MLA_ROPE),
        cols(COL_GQA_K, GQA_KV_HEADS * GQA_HEAD_DIM), cols(COL_GQA_V, GQA_KV_HEADS * GQA_HEAD_DIM),
        zeros(P_GATE - P_GQA_V - GQA_KV_HEADS * GQA_HEAD_DIM),
        cols(COL_GATE, N_BRANCH * d)], axis=-1)
    wq = w_q_up.reshape(depth, MLA_Q_RANK, MLA_HEADS, MLA_NOPE + MLA_ROPE).astype(BF16)
    wq = jnp.pad(wq, ((0, 0), (0, 0), (0, 0), (0, MLA_PAD - MLA_NOPE - MLA_ROPE)))
    wq = wq.reshape(depth, MLA_Q_RANK, MLA_HEADS * MLA_PAD)
    wkv = w_kv_up.reshape(depth, MLA_KV_RANK, MLA_HEADS, MLA_NOPE + MLA_V).astype(BF16)
    wk = wkv[..., :MLA_NOPE].reshape(depth, MLA_KV_RANK, MLA_HEADS * MLA_NOPE)
    wv = wkv[..., MLA_NOPE:].reshape(depth, MLA_KV_RANK, MLA_HEADS * MLA_V)
    w_conv_out_b = w_conv_out.astype(BF16)
    w_mla_out_b = w_mla_out.astype(BF16)
    w_gqa_out_b = w_gqa_out.astype(BF16)
    w_out_b = w_out.astype(BF16)
    w_ffn_in_b = w_ffn_in.astype(BF16)
    w_ffn_out_b = w_ffn_out.astype(BF16)
    cos, sin_up, sin_dn = _rope_tables(seq, ts)

    cc = jnp.zeros((16, d), F32).at[:bsz].set(c).at[bsz].set(c_ctx)
    mod = _ada(cc, w_ada, b_ada).reshape(depth, 16, 1, 6 * d)

    xa = jnp.concatenate([x.reshape(m_lat, d), ctx.reshape(m_ctx, d)], axis=0)

    def vec(a):
        return a.reshape(depth, 1, a.shape[-1])

    g_mix3, g_ffn3, g_q3, g_kv3 = vec(g_mix), vec(g_ffn), vec(g_q_a), vec(g_kv_a)
    conv_b3, ln_g3, ln_b3, sink3 = vec(conv_b), vec(conv_ln_g), vec(conv_ln_b), vec(gqa_sink)

    def mod_spec(l, part, tile, width, with_j):
        nb = d // width
        if with_j:
            return pl.BlockSpec((None, None, 1, width),
                                lambda i, j: (l, mod_row(i, tile), 0, part * nb + j))
        return pl.BlockSpec((None, None, 1, width), lambda i, j: (l, mod_row(i, tile), 0, part))

    def table_specs(tile):
        per_batch = seq // tile
        n_lat = m_lat // tile
        return [pl.BlockSpec((tile, LANES), lambda i: (jnp.where(i < n_lat, i % per_batch, per_batch), 0))] * 3

    for l in range(depth):
        p = pl.pallas_call(
            functools.partial(_in_proj_kernel, n_plain=P_GATE // tn, rc=rc),
            grid=(m_all // tm, p_cols // tn),
            in_specs=[
                pl.BlockSpec((tm, d), lambda i, j: (i, 0)),
                pl.BlockSpec((None, 1, d), lambda i, j: (l, 0, 0)),
                mod_spec(l, 0, tm, d, False),
                mod_spec(l, 1, tm, d, False),
                pl.BlockSpec((None, d, tn), lambda i, j: (l, 0, j)),
            ],
            out_specs=pl.BlockSpec((tm, tn), lambda i, j: (i, j)),
            out_shape=jax.ShapeDtypeStruct((m_all, p_cols), BF16),
            scratch_shapes=[pltpu.VMEM((tm, d), BF16)],
            compiler_params=_cparams("parallel", "arbitrary"),
            name="in_proj",
        )(xa, g_mix3, mod, mod, w_in_r)

        q_mla, q_gqa = pl.pallas_call(
            functools.partial(_qup_kernel, rc=rc, s_mla=float((MLA_NOPE + MLA_ROPE) ** -0.5),
                              s_gqa=float(GQA_HEAD_DIM ** -0.5)),
            grid=(m_all // ts,),
            in_specs=[
                pl.BlockSpec((ts, 1024), lambda i: (i, P_MLA_Q // 1024)),
                pl.BlockSpec((None, 1, MLA_Q_RANK), lambda i: (l, 0, 0)),
                pl.BlockSpec((None, MLA_Q_RANK, MLA_HEADS * MLA_PAD), lambda i: (l, 0, 0)),
            ] + table_specs(ts),
            out_specs=[pl.BlockSpec((ts, MLA_HEADS * MLA_PAD), lambda i: (i, 0)),
                       pl.BlockSpec((ts, GQA_HEADS * GQA_HEAD_DIM), lambda i: (i, 0))],
            out_shape=[jax.ShapeDtypeStruct((m_all, MLA_HEADS * MLA_PAD), BF16),
                       jax.ShapeDtypeStruct((m_all, GQA_HEADS * GQA_HEAD_DIM), BF16)],
            compiler_params=_cparams("parallel"),
            name="q_up",
        )(p, g_q3, wq, cos, sin_up, sin_dn)

        k_mla, v_mla, k_gqa = pl.pallas_call(
            functools.partial(_kvup_kernel, rc=rc),
            grid=(m_all // ts,),
            in_specs=[
                pl.BlockSpec((ts, MLA_KV_RANK), lambda i: (i, P_CKV // MLA_KV_RANK)),
                pl.BlockSpec((ts, 2 * LANES), lambda i: (i, P_KR // (2 * LANES))),
                pl.BlockSpec((None, 1, MLA_KV_RANK), lambda i: (l, 0, 0)),
                pl.BlockSpec((None, MLA_KV_RANK, MLA_HEADS * MLA_NOPE), lambda i: (l, 0, 0)),
                pl.BlockSpec((None, MLA_KV_RANK, MLA_HEADS * MLA_V), lambda i: (l, 0, 0)),
            ] + table_specs(ts),
            out_specs=[pl.BlockSpec((ts, MLA_HEADS * MLA_PAD), lambda i: (i, 0)),
                       pl.BlockSpec((ts, MLA_HEADS * MLA_V), lambda i: (i, 0)),
                       pl.BlockSpec((ts, LANES), lambda i: (i, 0))],
            out_shape=[jax.ShapeDtypeStruct((m_all, MLA_HEADS * MLA_PAD), BF16),
                       jax.ShapeDtypeStruct((m_all, MLA_HEADS * MLA_V), BF16),
                       jax.ShapeDtypeStruct((m_all, LANES), BF16)],
            compiler_params=_cparams("parallel"),
            name="kv_up",
        )(p, p, g_kv3, wk, wv, cos, sin_up, sin_dn)

        tq = 512
        tk = 512
        nq = seq // tq
        o_mla = pl.pallas_call(
            functools.partial(_mla_attn_kernel, tk=tk, n_chunks=seq // tk),
            grid=(bsz, MLA_HEADS, nq),
            in_specs=[
                pl.BlockSpec((tq, MLA_PAD), lambda b, h, i: (b * nq + i, h)),
                pl.BlockSpec((n_ctx, MLA_PAD), lambda b, h, i: (ctx_blk0 + b, h)),
                pl.BlockSpec((n_ctx, MLA_V), lambda b, h, i: (ctx_blk0 + b, h)),
                pl.BlockSpec((seq, MLA_PAD), lambda b, h, i: (b, h)),
                pl.BlockSpec((seq, MLA_V), lambda b, h, i: (b, h)),
            ],
            out_specs=pl.BlockSpec((tq, MLA_V), lambda b, h, i: (b * nq + i, h)),
            out_shape=jax.ShapeDtypeStruct((m_all, MLA_HEADS * MLA_V), BF16),
            compiler_params=_cparams("parallel", "parallel", "arbitrary"),
            name="mla_attn",
        )(q_mla, k_mla, v_mla, k_mla, v_mla)
        o_mla = pl.pallas_call(
            functools.partial(_mla_attn_kernel, tk=tk, n_chunks=0),
            grid=(bsz, MLA_HEADS),
            in_specs=[
                pl.BlockSpec((n_ctx, MLA_PAD), lambda b, h: (ctx_blk0 + b, h)),
                pl.BlockSpec((n_ctx, MLA_PAD), lambda b, h: (ctx_blk0 + b, h)),
                pl.BlockSpec((n_ctx, MLA_V), lambda b, h: (ctx_blk0 + b, h)),
                pl.BlockSpec(memory_space=pl.ANY),
            ],
            out_specs=pl.BlockSpec((n_ctx, MLA_V), lambda b, h: (ctx_blk0 + b, h)),
            out_shape=jax.ShapeDtypeStruct((m_all, MLA_HEADS * MLA_V), BF16),
            input_output_aliases={3: 0},
            compiler_params=_cparams("parallel", "parallel"),
            name="mla_attn_ctx",
        )(q_mla, k_mla, v_mla, o_mla)

        tqg = 512
        nqg = seq // tqg
        gv_blk = P_GQA_V // LANES
        o_gqa = pl.pallas_call(
            functools.partial(_gqa_attn_kernel, seq=seq, n_sub=tqg // WINDOW, latent=True),
            grid=(bsz, nqg),
            in_specs=[
                pl.BlockSpec((tqg, GQA_HEADS * GQA_HEAD_DIM), lambda b, i: (b * nqg + i, 0)),
                pl.BlockSpec((n_ctx, LANES), lambda b, i: (ctx_blk0 + b, 0)),
                pl.BlockSpec((n_ctx, LANES), lambda b, i: (ctx_blk0 + b, gv_blk)),
                pl.BlockSpec((None, 1, GQA_HEADS), lambda b, i: (l, 0, 0)),
                pl.BlockSpec((seq, LANES), lambda b, i: (b, 0)),
                pl.BlockSpec((seq, LANES), lambda b, i: (b, gv_blk)),
            ],
            out_specs=pl.BlockSpec((tqg, GQA_HEADS * GQA_HEAD_DIM), lambda b, i: (b * nqg + i, 0)),
            out_shape=jax.ShapeDtypeStruct((m_all, GQA_HEADS * GQA_HEAD_DIM), BF16),
            compiler_params=_cparams("parallel", "arbitrary"),
            name="gqa_attn",
        )(q_gqa, k_gqa, p, sink3, k_gqa, p)
        o_gqa = pl.pallas_call(
            functools.partial(_gqa_attn_kernel, seq=n_ctx, n_sub=n_ctx // WINDOW, latent=False),
            grid=(bsz,),
            in_specs=[
                pl.BlockSpec((n_ctx, GQA_HEADS * GQA_HEAD_DIM), lambda b: (ctx_blk0 + b, 0)),
                pl.BlockSpec((n_ctx, LANES), lambda b: (ctx_blk0 + b, 0)),
                pl.BlockSpec((n_ctx, LANES), lambda b: (ctx_blk0 + b, gv_blk)),
                pl.BlockSpec((None, 1, GQA_HEADS), lambda b: (l, 0, 0)),
                pl.BlockSpec(memory_space=pl.ANY),
            ],
            out_specs=pl.BlockSpec((n_ctx, GQA_HEADS * GQA_HEAD_DIM), lambda b: (ctx_blk0 + b, 0)),
            out_shape=jax.ShapeDtypeStruct((m_all, GQA_HEADS * GQA_HEAD_DIM), BF16),
            input_output_aliases={4: 0},
            compiler_params=_cparams("parallel"),
            name="gqa_attn_ctx",
        )(q_gqa, k_gqa, p, sink3, o_gqa)

        def conv_call(length, blk0, prev):
            crc = 64
            in_specs = [
                pl.BlockSpec((length, CONV_CH), lambda b: (blk0 + b, P_CONV_A // CONV_CH)),
                pl.BlockSpec((length, CONV_CH), lambda b: (blk0 + b, P_CONV_B // CONV_CH)),
                pl.BlockSpec((None, CONV_WIDTH, CONV_CH), lambda b: (l, 0, 0)),
                pl.BlockSpec((None, 1, CONV_CH), lambda b: (l, 0, 0)),
                pl.BlockSpec((None, 1, CONV_CH), lambda b: (l, 0, 0)),
                pl.BlockSpec((None, 1, CONV_CH), lambda b: (l, 0, 0)),
            ]
            args = [p, p, conv_w, conv_b3, ln_g3, ln_b3]
            aliases = {}
            if prev is not None:
                in_specs.append(pl.BlockSpec(memory_space=pl.ANY))
                args.append(prev)
                aliases = {6: 0}

            def body(a_ref, b_ref, w_ref, cb_ref, lg_ref, lb_ref, *rest):
                o_ref, u_ref, win_ref = rest[-3:]
                _conv_kernel(a_ref, b_ref, w_ref, cb_ref, lg_ref, lb_ref, o_ref, u_ref, win_ref,
                             seq=length, rc=crc)

            return pl.pallas_call(
                body,
                grid=(bsz,),
                in_specs=in_specs,
                out_specs=pl.BlockSpec((length, CONV_CH), lambda b: (blk0 + b, 0)),
                out_shape=jax.ShapeDtypeStruct((m_all, CONV_CH), BF16),
                scratch_shapes=[pltpu.VMEM((length + 32, CONV_CH), F32),
                                pltpu.VMEM((crc + 32, CONV_CH), F32)],
                input_output_aliases=aliases,
                compiler_params=_cparams("parallel"),
                name="conv_branch" if prev is None else "conv_branch_ctx",
            )(*args)

        o_conv = conv_call(seq, 0, None)
        o_conv = conv_call(n_ctx, ctx_blk0, o_conv)

        gate_blk = P_GATE // tn
        nb = d // tn
        y = pl.pallas_call(
            _merge_kernel,
            grid=(m_all // tm, nb),
            in_specs=[
                pl.BlockSpec((tm, CONV_CH), lambda i, j: (i, 0)),
                pl.BlockSpec((tm, MLA_HEADS * MLA_V), lambda i, j: (i, 0)),
                pl.BlockSpec((tm, GQA_HEADS * GQA_HEAD_DIM), lambda i, j: (i, 0)),
                pl.BlockSpec((None, CONV_CH, tn), lambda i, j: (l, 0, j)),
                pl.BlockSpec((None, MLA_HEADS * MLA_V, tn), lambda i, j: (l, 0, j)),
                pl.BlockSpec((None, GQA_HEADS * GQA_HEAD_DIM, tn), lambda i, j: (l, 0, j)),
                pl.BlockSpec((tm, tn), lambda i, j: (i, gate_blk + j)),
                pl.BlockSpec((tm, tn), lambda i, j: (i, gate_blk + nb + j)),
                pl.BlockSpec((tm, tn), lambda i, j: (i, gate_blk + 2 * nb + j)),
            ],
            out_specs=pl.BlockSpec((tm, tn), lambda i, j: (i, j)),
            out_shape=jax.ShapeDtypeStruct((m_all, d), BF16),
            compiler_params=_cparams("parallel", "parallel"),
            name="merge",
        )(o_conv, o_mla, o_gqa, w_conv_out_b, w_mla_out_b, w_gqa_out_b, p, p, p)

        def mm_res(a, w, xin, part, tile, name):
            kdim = a.shape[1]
            return pl.pallas_call(
                _mm_res_kernel,
                grid=(m_all // tile, d // tn),
                in_specs=[
                    pl.BlockSpec((tile, kdim), lambda i, j: (i, 0)),
                    pl.BlockSpec((None, kdim, tn), lambda i, j: (l, 0, j)),
                    pl.BlockSpec((tile, tn), lambda i, j: (i, j)),
                    mod_spec(l, part, tile, tn, True),
                ],
                out_specs=pl.BlockSpec((tile, tn), lambda i, j: (i, j)),
                out_shape=jax.ShapeDtypeStruct((m_all, d), F32),
                compiler_params=_cparams("parallel", "parallel"),
                name=name,
            )(a, w, xin, mod)

        xa = mm_res(y, w_out_b, xa, 2, tm, "out_proj")

        nf = d_ff // tn
        act = pl.pallas_call(
            functools.partial(_ffn_in_kernel, rc=rc),
            grid=(m_all // tm, nf),
            in_specs=[
                pl.BlockSpec((tm, d), lambda i, j: (i, 0)),
                pl.BlockSpec((None, 1, d), lambda i, j: (l, 0, 0)),
                mod_spec(l, 3, tm, d, False),
                mod_spec(l, 4, tm, d, False),
                pl.BlockSpec((None, d, tn), lambda i, j: (l, 0, j)),
                pl.BlockSpec((None, d, tn), lambda i, j: (l, 0, nf + j)),
            ],
            out_specs=pl.BlockSpec((tm, tn), lambda i, j: (i, j)),
            out_shape=jax.ShapeDtypeStruct((m_all, d_ff), BF16),
            scratch_shapes=[pltpu.VMEM((tm, d), BF16)],
            compiler_params=_cparams("parallel", "arbitrary"),
            name="ffn_in",
        )(xa, g_ffn3, mod, mod, w_ffn_in_b, w_ffn_in_b)
        xa = mm_res(act, w_ffn_out_b, xa, 5, ts, "ffn_out")

    out = pl.pallas_call(
        functools.partial(_final_norm_kernel, rc=rc),
        grid=(m_lat // tm,),
        in_specs=[pl.BlockSpec((tm, d), lambda i: (i, 0)),
                  pl.BlockSpec((1, d), lambda i: (0, 0))],
        out_specs=pl.BlockSpec((tm, d), lambda i: (i, 0)),
        out_shape=jax.ShapeDtypeStruct((m_lat, d), F32),
        compiler_params=_cparams("parallel"),
        name="final_norm",
    )(xa, g_final.reshape(1, d))
    return out.reshape(bsz, seq, d)
```

```python
import functools

import jax
import jax.numpy as jnp
from jax import lax
from jax.experimental import pallas as pl
from jax.experimental.pallas import tpu as pltpu

F32 = jnp.float32
BF16 = jnp.bfloat16

GRID_W = 64
CONV_CH = 512
CONV_WIDTH = 31
MLA_HEADS = 8
MLA_Q_RANK = 512
MLA_KV_RANK = 256
MLA_NOPE = 128
MLA_ROPE = 64
MLA_V = 128
GQA_HEADS = 8
GQA_KV_HEADS = 2
GQA_HEAD_DIM = 64
WINDOW = 128
N_BRANCH = 3
ROPE_DIM = 64
ROPE_BASE = 10000.0
EPS = 1e-6
NEG_INF = -1e30

COL_MLA_KV = 0
COL_MLA_KR = COL_MLA_KV + MLA_KV_RANK
COL_GQA_K = COL_MLA_KR + MLA_ROPE
COL_GQA_V = COL_GQA_K + GQA_KV_HEADS * GQA_HEAD_DIM
KV_COLS = COL_GQA_V + GQA_KV_HEADS * GQA_HEAD_DIM
COL_MLA_Q = KV_COLS
COL_GQA_Q = COL_MLA_Q + MLA_Q_RANK
COL_CONV = COL_GQA_Q + GQA_HEADS * GQA_HEAD_DIM
COL_GATE = COL_CONV + 2 * CONV_CH

LANES = 128
MLA_PAD = 256
VMEM_LIMIT = 56 * 1024 * 1024
MLA_TQ = 1024
MLA_RG = 64
MLA_TK = 512
LOG2_E = 1.4426950408889634

P_MLA_Q = 0
P_GQA_Q = 512
P_CONV_A = 1024
P_CONV_B = 1536
P_CKV = 2048
P_KR = 2304
P_GQA_K = 2432
P_GQA_V = 2560
P_GATE = 3072
P_COLS = P_GATE + 0


def _cparams(*sem):
    return pltpu.CompilerParams(dimension_semantics=sem, vmem_limit_bytes=VMEM_LIMIT)


def _for_chunks(n_rows, rc, fn):
    n = n_rows // rc
    if n == 1:
        fn(0)
        return

    def body(i, carry):
        fn(pl.multiple_of(i * rc, rc))
        return carry

    lax.fori_loop(0, n, body, 0)


def _sigmoid(x):
    return 0.5 * jnp.tanh(0.5 * x) + 0.5


def _rope(x, cos, sin_up, sin_dn):
    return x * cos + pltpu.roll(x, LANES - 16, 1) * sin_up + pltpu.roll(x, 16, 1) * sin_dn


def _dot(a, b):
    return jnp.dot(a, b, preferred_element_type=F32)


def _dot_nt(a, b):
    return lax.dot_general(a, b, (((1,), (1,)), ((), ())), preferred_element_type=F32)


def _ada_kernel(c_ref, w_ref, b_ref, o_ref):
    c = c_ref[...]
    s = c * _sigmoid(c)
    o_ref[...] = jnp.dot(s, w_ref[...], preferred_element_type=F32,
                         precision=lax.Precision.HIGHEST) + b_ref[...]


def _ada(cc, w_ada, b_ada):
    depth, d, n = w_ada.shape
    rows = cc.shape[0]
    tn = 1024
    return pl.pallas_call(
        _ada_kernel,
        grid=(depth, n // tn),
        in_specs=[
            pl.BlockSpec((rows, d), lambda l, j: (0, 0)),
            pl.BlockSpec((None, d, tn), lambda l, j: (l, 0, j)),
            pl.BlockSpec((None, 1, tn), lambda l, j: (l, 0, j)),
        ],
        out_specs=pl.BlockSpec((None, rows, tn), lambda l, j: (l, 0, j)),
        out_shape=jax.ShapeDtypeStruct((depth, rows, n), F32),
        compiler_params=_cparams("parallel", "parallel"),
        name="ada_mod",
    )(cc, w_ada, b_ada.reshape(depth, 1, n))


def _norm_mod_store(x_ref, g_ref, sh_ref, sc_ref, h_ref, rc):
    gs = g_ref[...] * (1.0 + sc_ref[...])
    sh = sh_ref[...]

    def chunk(r0):
        x = x_ref[pl.ds(r0, rc), :]
        ms = jnp.mean(x * x, axis=-1, keepdims=True)
        h_ref[pl.ds(r0, rc), :] = (x * lax.rsqrt(ms + EPS) * gs + sh).astype(BF16)

    _for_chunks(x_ref.shape[0], rc, chunk)


def _in_proj_kernel(x_ref, g_ref, sh_ref, sc_ref, w_ref, o_ref, h_ref, *, n_plain, rc):
    j = pl.program_id(1)

    @pl.when(j == 0)
    def _():
        _norm_mod_store(x_ref, g_ref, sh_ref, sc_ref, h_ref, rc)

    acc = _dot(h_ref[...], w_ref[...])
    o_ref[...] = jnp.where(j >= n_plain, _sigmoid(acc), acc).astype(BF16)


def _ffn_in_kernel(x_ref, g_ref, sh_ref, sc_ref, w1_ref, w2_ref, o_ref, h_ref, *, rc):
    @pl.when(pl.program_id(1) == 0)
    def _():
        _norm_mod_store(x_ref, g_ref, sh_ref, sc_ref, h_ref, rc)

    h = h_ref[...]
    u1 = _dot(h, w1_ref[...])
    u2 = _dot(h, w2_ref[...])
    o_ref[...] = (u1 * _sigmoid(u1) * u2).astype(BF16)


def _mm_res_kernel(a_ref, w_ref, x_ref, gt_ref, o_ref):
    o_ref[...] = x_ref[...] + gt_ref[...] * _dot(a_ref[...], w_ref[...])


def _merge_kernel(oc_ref, om_ref, og_ref, wc_ref, wm_ref, wg_ref, g0_ref, g1_ref, g2_ref, y_ref):
    y = g0_ref[...].astype(F32) * _dot(oc_ref[...], wc_ref[...])
    y += g1_ref[...].astype(F32) * _dot(om_ref[...], wm_ref[...])
    y += g2_ref[...].astype(F32) * _dot(og_ref[...], wg_ref[...])
    y_ref[...] = y.astype(BF16)


def _rms_bf16(a, g):
    ms = jnp.mean(a * a, axis=-1, keepdims=True)
    return (a * lax.rsqrt(ms + EPS) * g).astype(BF16)


def _qup_kernel(pq_ref, g_ref, wq_ref, cos_ref, su_ref, sd_ref, q_ref, qg_ref, *, rc, s_mla, s_gqa):
    g = g_ref[...]

    def chunk(r0):
        rows = pl.ds(r0, rc)
        n = _rms_bf16(pq_ref[rows, 0:MLA_Q_RANK].astype(F32), g)
        q = _dot(n, wq_ref[...])
        cos = cos_ref[rows, :]
        su = su_ref[rows, :]
        sd = sd_ref[rows, :]
        for h in range(MLA_HEADS):
            c0 = h * MLA_PAD
            q_ref[rows, c0:c0 + LANES] = (q[:, c0:c0 + LANES] * s_mla).astype(BF16)
            r = _rope(q[:, c0 + LANES:c0 + 2 * LANES], cos, su, sd)
            q_ref[rows, c0 + LANES:c0 + 2 * LANES] = (r * s_mla).astype(BF16)
        for t in range(GQA_HEADS * GQA_HEAD_DIM // LANES):
            gq = pq_ref[rows, MLA_Q_RANK + t * LANES:MLA_Q_RANK + (t + 1) * LANES].astype(F32)
            qg_ref[rows, t * LANES:(t + 1) * LANES] = (_rope(gq, cos, su, sd) * s_gqa).astype(BF16)

    _for_chunks(pq_ref.shape[0], rc, chunk)


def _kvup_kernel(ckv_ref, kr_ref, g_ref, wk_ref, wv_ref, cos_ref, su_ref, sd_ref,
                 k_ref, v_ref, kg_ref, vg_ref, *, rc):
    g = g_ref[...]

    def chunk(r0):
        rows = pl.ds(r0, rc)
        n = _rms_bf16(ckv_ref[rows, :].astype(F32), g)
        kn = _dot(n, wk_ref[...])
        vv = _dot(n, wv_ref[...])
        cos = cos_ref[rows, :]
        su = su_ref[rows, :]
        sd = sd_ref[rows, :]
        kr = _rope(kr_ref[rows, 0:LANES].astype(F32), cos, su, sd).astype(BF16)
        ones_col = jnp.where(lax.broadcasted_iota(jnp.int32, (rc, LANES), 1) == 0, 1.0, 0.0).astype(BF16)
        for h in range(MLA_HEADS):
            c0 = h * MLA_PAD
            k_ref[rows, c0:c0 + LANES] = kn[:, h * LANES:(h + 1) * LANES].astype(BF16)
            k_ref[rows, c0 + LANES:c0 + 2 * LANES] = kr
            v_ref[rows, c0:c0 + LANES] = vv[:, h * LANES:(h + 1) * LANES].astype(BF16)
            v_ref[rows, c0 + LANES:c0 + 2 * LANES] = ones_col
        low_half = lax.broadcasted_iota(jnp.int32, (rc, LANES), 1) < GQA_HEAD_DIM
        gk = _rope(kr_ref[rows, LANES:2 * LANES].astype(F32), cos, su, sd)
        gk_sw = pltpu.roll(gk, GQA_HEAD_DIM, 1)
        kg_ref[rows, 0:LANES] = jnp.where(low_half, gk, gk_sw).astype(BF16)
        kg_ref[rows, LANES:2 * LANES] = jnp.where(low_half, gk_sw, gk).astype(BF16)
        gv = kr_ref[rows, 2 * LANES:3 * LANES].astype(F32)
        gv_sw = pltpu.roll(gv, GQA_HEAD_DIM, 1)
        vg_ref[rows, 0:LANES] = jnp.where(low_half, gv, gv_sw).astype(BF16)
        vg_ref[rows, LANES:2 * LANES] = ones_col
        vg_ref[rows, 2 * LANES:3 * LANES] = jnp.where(low_half, gv_sw, gv).astype(BF16)
        vg_ref[rows, 3 * LANES:4 * LANES] = ones_col

    _for_chunks(ckv_ref.shape[0], rc, chunk)


def _mla_attn_kernel(q_ref, k_ref, v_ref, *rest, tk, rg):
    o_ref, s0, s1, p0, p1, a0, a1, m_ref, acc_ref = rest[-9:]
    s_buf, p_buf, a_buf = (s0, s1), (p0, p1), (a0, a1)
    tq = q_ref.shape[0]
    head = k_ref.shape[0] % tk
    has_head = 1 if head else 0
    n = k_ref.shape[0] // tk + has_head

    def start_of(c):
        return head + (c - has_head) * tk

    def size_of(c):
        return head if (has_head and c == 0) else tk

    def qk(slot, start, size):
        s_buf[slot][:, 0:size] = _dot_nt(q_ref[...], k_ref[pl.ds(start, size), :])

    def sm(slot, size):
        for r in range(0, tq, rg):
            s = s_buf[slot][r:r + rg, 0:size]
            m_old = m_ref[r:r + rg, :]
            m_new = jnp.maximum(m_old, jnp.max(s, axis=-1, keepdims=True))
            p_buf[slot][r:r + rg, 0:size] = jnp.exp2(s - m_new).astype(BF16)
            a_buf[slot][r:r + rg, :] = jnp.exp2(m_old - m_new)
            m_ref[r:r + rg, :] = m_new

    def pv(slot, start, size):
        acc_ref[...] = a_buf[slot][...] * acc_ref[...] + _dot(p_buf[slot][:, 0:size],
                                                               v_ref[pl.ds(start, size), :])

    def static_start(c):
        return 0 if (has_head and c == 0) else start_of(c)

    def static_tick(t):
        if 0 <= t - 2 < n:
            pv((t - 2) % 2, static_start(t - 2), size_of(t - 2))
        if 0 <= t < n:
            qk(t % 2, static_start(t), size_of(t))
        if 0 <= t - 1 < n:
            sm((t - 1) % 2, size_of(t - 1))

    m_ref[...] = jnp.full(m_ref.shape, NEG_INF, F32)
    acc_ref[...] = jnp.zeros(acc_ref.shape, F32)

    first_steady = 2 + has_head
    pairs = max(n - first_steady, 0) // 2
    for t in range(first_steady):
        static_tick(t)
    if pairs:
        def pair(j, carry):
            for u in range(2):
                t = first_steady + 2 * j + u
                par = (first_steady + u) % 2
                pv(par, pl.multiple_of(start_of(t - 2), 2 * LANES), tk)
                qk(par, pl.multiple_of(start_of(t), 2 * LANES), tk)
                sm(1 - par, tk)
            return carry

        lax.fori_loop(0, pairs, pair, 0)
    for t in range(first_steady + 2 * pairs, n + 2):
        static_tick(t)
    acc = acc_ref[...]
    o_ref[...] = (acc[:, :MLA_V] / acc[:, MLA_V:MLA_V + 1]).astype(BF16)


def _gqa_attn_kernel(*refs, seq, n_sub, latent):
    if latent:
        q_ref, kc_ref, vc_ref, sink_ref, kl_ref, vl_ref, o_ref = refs
    else:
        q_ref, kc_ref, vc_ref, sink_ref, _, o_ref = refs
    span = 3 * WINDOW
    rep = GQA_HEADS // GQA_KV_HEADS
    tile_q0 = pl.program_id(1) * (n_sub * WINDOW) if latent else 0
    lane = lax.broadcasted_iota(jnp.int32, (1, LANES), 1)
    low_half = lane < GQA_HEAD_DIM
    half_masks = (jnp.where(low_half, 1.0, 0.0).astype(BF16), jnp.where(low_half, 0.0, 1.0).astype(BF16))
    sinks = sink_ref[...] * LOG2_E
    sinks = [sinks[:, h:h + 1] for h in range(GQA_HEADS)]

    def sub(r0):
        rows = pl.ds(r0, WINDOW)
        if latent:
            q0 = tile_q0 + r0
            start = pl.multiple_of(jnp.clip(q0 - WINDOW, 0, seq - span), WINDOW)
            qpos = q0 + lax.broadcasted_iota(jnp.int32, (WINDOW, span), 0)
            kpos = start + lax.broadcasted_iota(jnp.int32, (WINDOW, span), 1)
            bias = jnp.where(jnp.abs(qpos - kpos) <= WINDOW, 0.0, NEG_INF)
        sc, sw = [], []
        for h in range(GQA_HEADS):
            g = h // rep
            q = q_ref[rows, (h // 2) * LANES:(h // 2 + 1) * LANES] * half_masks[h % 2]
            sc.append(_dot_nt(q, kc_ref[:, g * LANES:(g + 1) * LANES]))
            if latent:
                sw.append(_dot_nt(q, kl_ref[pl.ds(start, span), g * LANES:(g + 1) * LANES]) + bias)
        ms, pc, pw = [], [], []
        for h in range(GQA_HEADS):
            m = jnp.maximum(jnp.max(sc[h], axis=-1, keepdims=True), sinks[h])
            if latent:
                m = jnp.maximum(m, jnp.max(sw[h], axis=-1, keepdims=True))
                pw.append(jnp.exp2(sw[h] - m).astype(BF16))
            pc.append(jnp.exp2(sc[h] - m).astype(BF16))
            ms.append(m)
        outs = []
        for h in range(GQA_HEADS):
            g = h // rep
            o = _dot(pc[h], vc_ref[:, g * 2 * LANES:(g + 1) * 2 * LANES])
            if latent:
                o = o + _dot(pw[h], vl_ref[pl.ds(start, span), g * 2 * LANES:(g + 1) * 2 * LANES])
            den = o[:, LANES:LANES + 1] + jnp.exp2(sinks[h] - ms[h])
            outs.append(o[:, :LANES] / den)
        for t in range(GQA_HEADS // 2):
            o_ref[rows, t * LANES:(t + 1) * LANES] = jnp.where(low_half, outs[2 * t], outs[2 * t + 1]).astype(BF16)

    _for_chunks(n_sub * WINDOW, WINDOW, sub)


def _conv_kernel(a_ref, b_ref, w_ref, cb_ref, lg_ref, lb_ref, o_ref, u_ref, win_ref, *, seq, rc):
    halo = 16
    u_ref[0:halo, :] = jnp.zeros((halo, CONV_CH), F32)
    u_ref[halo + seq:2 * halo + seq, :] = jnp.zeros((halo, CONV_CH), F32)

    def fill(r0):
        a = a_ref[pl.ds(r0, 256), :].astype(F32)
        b = b_ref[pl.ds(r0, 256), :].astype(F32)
        u_ref[pl.ds(halo + r0, 256), :] = a * _sigmoid(b)

    _for_chunks(seq, 256, fill)

    cb = cb_ref[...]
    lg = lg_ref[...]
    lb = lb_ref[...]

    half = CONV_CH // 2
    first_tap = halo - CONV_WIDTH // 2

    def conv(r0):
        win_ref[...] = u_ref[pl.ds(r0, rc + 2 * halo), :]
        parts = []
        for c0 in (0, half):
            acc = jnp.zeros((rc, half), F32) + cb[:, c0:c0 + half]
            for s in range(8):
                shifted = win_ref[s:s + rc + 24, c0:c0 + half]
                for a in range(4):
                    k = 8 * a + s - first_tap
                    if 0 <= k < CONV_WIDTH:
                        acc = acc + w_ref[k:k + 1, c0:c0 + half] * shifted[8 * a:8 * a + rc]
            parts.append(acc)
        acc = jnp.concatenate(parts, axis=-1)
        mu = jnp.mean(acc, axis=-1, keepdims=True)
        xc = acc - mu
        var = jnp.mean(xc * xc, axis=-1, keepdims=True)
        y = xc * lax.rsqrt(var + EPS) * lg + lb
        o_ref[pl.ds(r0, rc), :] = (y * _sigmoid(y)).astype(BF16)

    _for_chunks(seq, rc, conv)


def _final_norm_kernel(x_ref, g_ref, o_ref, *, rc):
    g = g_ref[...]

    def chunk(r0):
        x = x_ref[pl.ds(r0, rc), :]
        ms = jnp.mean(x * x, axis=-1, keepdims=True)
        o_ref[pl.ds(r0, rc), :] = x * lax.rsqrt(ms + EPS) * g

    _for_chunks(x_ref.shape[0], rc, chunk)


def _rope_tables(length, extra):
    rows = length // GRID_W
    row = jnp.repeat(jnp.arange(rows), GRID_W).astype(F32)
    col = jnp.tile(jnp.arange(GRID_W), rows).astype(F32)
    n_freq = ROPE_DIM // 4
    inv_freq = ROPE_BASE ** (-jnp.arange(n_freq, dtype=F32) / n_freq)
    a_row = row[:, None] * inv_freq[None, :]
    a_col = col[:, None] * inv_freq[None, :]
    ang = jnp.concatenate([a_row, a_row, a_col, a_col], axis=-1)
    cos = jnp.concatenate([jnp.cos(ang), jnp.ones((extra, ROPE_DIM), F32)], axis=0)
    sin = jnp.concatenate([jnp.sin(ang), jnp.zeros((extra, ROPE_DIM), F32)], axis=0)
    cos = jnp.tile(cos, (1, LANES // ROPE_DIM))
    sin = jnp.tile(sin, (1, LANES // ROPE_DIM))
    first = (jnp.arange(LANES) // 16) % 2 == 0
    sin_up = jnp.where(first[None, :], -sin, 0.0)
    sin_dn = jnp.where(first[None, :], 0.0, sin)
    return cos, sin_up, sin_dn


def _pick_tile(*sizes):
    for t in (1024, 512, 256):
        if all(s % t == 0 for s in sizes):
            return t
    raise ValueError(f"unsupported row counts {sizes}")


def kernel(x, c, ctx, c_ctx, w_ada, b_ada, g_mix, w_in, conv_w, conv_b, conv_ln_g, conv_ln_b, w_conv_out, g_q_a, w_q_up, g_kv_a, w_kv_up, w_mla_out, gqa_sink, w_gqa_out, w_out, g_ffn, w_ffn_in, w_ffn_out, g_final):
    bsz, seq, d = x.shape
    n_ctx = ctx.shape[1]
    depth = w_ada.shape[0]
    d_ff = w_ffn_out.shape[1]
    m_lat = bsz * seq
    m_ctx = bsz * n_ctx
    m_all = m_lat + m_ctx
    assert seq % GRID_W == 0 and n_ctx == 2 * WINDOW and seq % (4 * WINDOW) == 0
    assert bsz + 1 <= 16 and w_in.shape[2] == COL_GATE + N_BRANCH * d

    tm = _pick_tile(seq, m_ctx)
    ts = min(tm, 512)
    tn = 512
    rc = 256
    n_lat_tiles = m_lat // tm
    tiles_per_batch = seq // tm
    p_cols = P_GATE + N_BRANCH * d
    ctx_blk0 = m_lat // n_ctx

    def mod_row(i, tile):
        return jnp.minimum(i // (seq // tile), bsz)

    def cols(a, n):
        return w_in[:, :, a:a + n].astype(BF16)

    def zeros(n):
        return jnp.zeros((depth, d, n), BF16)

    w_in_r = jnp.concatenate([
        cols(COL_MLA_Q, MLA_Q_RANK), cols(COL_GQA_Q, GQA_HEADS * GQA_HEAD_DIM),
        cols(COL_CONV, CONV_CH), cols(COL_CONV + CONV_CH, CONV_CH),
        cols(COL_MLA_KV, MLA_KV_RANK), cols(COL_MLA_KR, MLA_ROPE), zeros(LANES - MLA_ROPE),
        cols(COL_GQA_K, GQA_KV_HEADS * GQA_HEAD_DIM), cols(COL_GQA_V, GQA_KV_HEADS * GQA_HEAD_DIM),
        zeros(P_GATE - P_GQA_V - GQA_KV_HEADS * GQA_HEAD_DIM),
        cols(COL_GATE, N_BRANCH * d)], axis=-1)
    wq = w_q_up.reshape(depth, MLA_Q_RANK, MLA_HEADS, MLA_NOPE + MLA_ROPE).astype(BF16)
    wq = jnp.pad(wq, ((0, 0), (0, 0), (0, 0), (0, MLA_PAD - MLA_NOPE - MLA_ROPE)))
    wq = wq.reshape(depth, MLA_Q_RANK, MLA_HEADS * MLA_PAD)
    wkv = w_kv_up.reshape(depth, MLA_KV_RANK, MLA_HEADS, MLA_NOPE + MLA_V).astype(BF16)
    wk = wkv[..., :MLA_NOPE].reshape(depth, MLA_KV_RANK, MLA_HEADS * MLA_NOPE)
    wv = wkv[..., MLA_NOPE:].reshape(depth, MLA_KV_RANK, MLA_HEADS * MLA_V)
    w_conv_out_b = w_conv_out.astype(BF16)
    w_mla_out_b = w_mla_out.astype(BF16)
    w_gqa_out_b = w_gqa_out.astype(BF16)
    w_out_b = w_out.astype(BF16)
    w_ffn_in_b = w_ffn_in.astype(BF16)
    w_ffn_out_b = w_ffn_out.astype(BF16)
    cos, sin_up, sin_dn = _rope_tables(seq, ts)

    cc = jnp.zeros((16, d), F32).at[:bsz].set(c).at[bsz].set(c_ctx)
    mod = _ada(cc, w_ada, b_ada).reshape(depth, 16, 1, 6 * d)

    xa = jnp.concatenate([x.reshape(m_lat, d), ctx.reshape(m_ctx, d)], axis=0)

    def vec(a):
        return a.reshape(depth, 1, a.shape[-1])

    g_mix3, g_ffn3, g_q3, g_kv3 = vec(g_mix), vec(g_ffn), vec(g_q_a), vec(g_kv_a)
    conv_b3, ln_g3, ln_b3, sink3 = vec(conv_b), vec(conv_ln_g), vec(conv_ln_b), vec(gqa_sink)

    def mod_spec(l, part, tile, width, with_j):
        nb = d // width
        if with_j:
            return pl.BlockSpec((None, None, 1, width),
                                lambda i, j: (l, mod_row(i, tile), 0, part * nb + j))
        return pl.BlockSpec((None, None, 1, width), lambda i, j: (l, mod_row(i, tile), 0, part))

    def table_specs(tile):
        per_batch = seq // tile
        n_lat = m_lat // tile
        return [pl.BlockSpec((tile, LANES), lambda i: (jnp.where(i < n_lat, i % per_batch, per_batch), 0))] * 3

    for l in range(depth):
        p = pl.pallas_call(
            functools.partial(_in_proj_kernel, n_plain=P_GATE // tn, rc=rc),
            grid=(m_all // tm, p_cols // tn),
            in_specs=[
                pl.BlockSpec((tm, d), lambda i, j: (i, 0)),
                pl.BlockSpec((None, 1, d), lambda i, j: (l, 0, 0)),
                mod_spec(l, 0, tm, d, False),
                mod_spec(l, 1, tm, d, False),
                pl.BlockSpec((None, d, tn), lambda i, j: (l, 0, j)),
            ],
            out_specs=pl.BlockSpec((tm, tn), lambda i, j: (i, j)),
            out_shape=jax.ShapeDtypeStruct((m_all, p_cols), BF16),
            scratch_shapes=[pltpu.VMEM((tm, d), BF16)],
            compiler_params=_cparams("parallel", "arbitrary"),
            name="in_proj",
        )(xa, g_mix3, mod, mod, w_in_r)

        q_mla, q_gqa = pl.pallas_call(
            functools.partial(_qup_kernel, rc=rc, s_mla=float(LOG2_E * (MLA_NOPE + MLA_ROPE) ** -0.5),
                              s_gqa=float(LOG2_E * GQA_HEAD_DIM ** -0.5)),
            grid=(m_all // ts,),
            in_specs=[
                pl.BlockSpec((ts, 1024), lambda i: (i, P_MLA_Q // 1024)),
                pl.BlockSpec((None, 1, MLA_Q_RANK), lambda i: (l, 0, 0)),
                pl.BlockSpec((None, MLA_Q_RANK, MLA_HEADS * MLA_PAD), lambda i: (l, 0, 0)),
            ] + table_specs(ts),
            out_specs=[pl.BlockSpec((ts, MLA_HEADS * MLA_PAD), lambda i: (i, 0)),
                       pl.BlockSpec((ts, GQA_HEADS * GQA_HEAD_DIM), lambda i: (i, 0))],
            out_shape=[jax.ShapeDtypeStruct((m_all, MLA_HEADS * MLA_PAD), BF16),
                       jax.ShapeDtypeStruct((m_all, GQA_HEADS * GQA_HEAD_DIM), BF16)],
            compiler_params=_cparams("parallel"),
            name="q_up",
        )(p, g_q3, wq, cos, sin_up, sin_dn)

        tkv = n_ctx
        kv_per_batch = seq // tkv
        kv_lat_tiles = m_lat // tkv

        def kv_block(i):
            lat = (i // kv_per_batch) * (kv_per_batch + 1) + 1 + i % kv_per_batch
            return jnp.where(i < kv_lat_tiles, lat, (i - kv_lat_tiles) * (kv_per_batch + 1))

        k_mla, v_mla, k_gqa, v_gqa = pl.pallas_call(
            functools.partial(_kvup_kernel, rc=tkv),
            grid=(m_all // tkv,),
            in_specs=[
                pl.BlockSpec((tkv, MLA_KV_RANK), lambda i: (i, P_CKV // MLA_KV_RANK)),
                pl.BlockSpec((tkv, 3 * LANES), lambda i: (i, P_KR // (3 * LANES))),
                pl.BlockSpec((None, 1, MLA_KV_RANK), lambda i: (l, 0, 0)),
                pl.BlockSpec((None, MLA_KV_RANK, MLA_HEADS * MLA_NOPE), lambda i: (l, 0, 0)),
                pl.BlockSpec((None, MLA_KV_RANK, MLA_HEADS * MLA_V), lambda i: (l, 0, 0)),
            ] + table_specs(tkv),
            out_specs=[pl.BlockSpec((tkv, MLA_HEADS * MLA_PAD), lambda i: (kv_block(i), 0)),
                       pl.BlockSpec((tkv, MLA_HEADS * MLA_PAD), lambda i: (kv_block(i), 0)),
                       pl.BlockSpec((tkv, 2 * LANES), lambda i: (i, 0)),
                       pl.BlockSpec((tkv, 4 * LANES), lambda i: (i, 0))],
            out_shape=[jax.ShapeDtypeStruct((m_all, MLA_HEADS * MLA_PAD), BF16),
                       jax.ShapeDtypeStruct((m_all, MLA_HEADS * MLA_PAD), BF16),
                       jax.ShapeDtypeStruct((m_all, 2 * LANES), BF16),
                       jax.ShapeDtypeStruct((m_all, 4 * LANES), BF16)],
            compiler_params=_cparams("parallel"),
            name="kv_up",
        )(p, p, g_kv3, wk, wv, cos, sin_up, sin_dn)

        tq = min(MLA_TQ, seq)
        nq = seq // tq
        def mla_scratch(rows, tk):
            return ([pltpu.VMEM((rows, tk), F32)] * 2 + [pltpu.VMEM((rows, tk), BF16)] * 2
                    + [pltpu.VMEM((rows, 1), F32)] * 3 + [pltpu.VMEM((rows, MLA_PAD), F32)])

        o_mla = pl.pallas_call(
            functools.partial(_mla_attn_kernel, tk=MLA_TK, rg=MLA_RG),
            grid=(bsz, MLA_HEADS, nq),
            scratch_shapes=mla_scratch(tq, MLA_TK),
            in_specs=[
                pl.BlockSpec((tq, MLA_PAD), lambda b, h, i: (b * nq + i, h)),
                pl.BlockSpec((seq + n_ctx, MLA_PAD), lambda b, h, i: (b, h)),
                pl.BlockSpec((seq + n_ctx, MLA_PAD), lambda b, h, i: (b, h)),
            ],
            out_specs=pl.BlockSpec((tq, MLA_V), lambda b, h, i: (b * nq + i, h)),
            out_shape=jax.ShapeDtypeStruct((m_all, MLA_HEADS * MLA_V), BF16),
            compiler_params=_cparams("parallel", "parallel", "arbitrary"),
            name="mla_attn",
        )(q_mla, k_mla, v_mla)
        o_mla = pl.pallas_call(
            functools.partial(_mla_attn_kernel, tk=n_ctx, rg=MLA_RG),
            grid=(bsz, MLA_HEADS),
            scratch_shapes=mla_scratch(n_ctx, n_ctx),
            in_specs=[
                pl.BlockSpec((n_ctx, MLA_PAD), lambda b, h: (ctx_blk0 + b, h)),
                pl.BlockSpec((n_ctx, MLA_PAD), lambda b, h: (b * (kv_per_batch + 1), h)),
                pl.BlockSpec((n_ctx, MLA_PAD), lambda b, h: (b * (kv_per_batch + 1), h)),
                pl.BlockSpec(memory_space=pl.ANY),
            ],
            out_specs=pl.BlockSpec((n_ctx, MLA_V), lambda b, h: (ctx_blk0 + b, h)),
            out_shape=jax.ShapeDtypeStruct((m_all, MLA_HEADS * MLA_V), BF16),
            input_output_aliases={3: 0},
            compiler_params=_cparams("parallel", "parallel"),
            name="mla_attn_ctx",
        )(q_mla, k_mla, v_mla, o_mla)

        tqg = 512
        nqg = seq // tqg
        o_gqa = pl.pallas_call(
            functools.partial(_gqa_attn_kernel, seq=seq, n_sub=tqg // WINDOW, latent=True),
            grid=(bsz, nqg),
            in_specs=[
                pl.BlockSpec((tqg, GQA_HEADS * GQA_HEAD_DIM), lambda b, i: (b * nqg + i, 0)),
                pl.BlockSpec((n_ctx, 2 * LANES), lambda b, i: (ctx_blk0 + b, 0)),
                pl.BlockSpec((n_ctx, 4 * LANES), lambda b, i: (ctx_blk0 + b, 0)),
                pl.BlockSpec((None, 1, GQA_HEADS), lambda b, i: (l, 0, 0)),
                pl.BlockSpec((seq, 2 * LANES), lambda b, i: (b, 0)),
                pl.BlockSpec((seq, 4 * LANES), lambda b, i: (b, 0)),
            ],
            out_specs=pl.BlockSpec((tqg, GQA_HEADS * GQA_HEAD_DIM), lambda b, i: (b * nqg + i, 0)),
            out_shape=jax.ShapeDtypeStruct((m_all, GQA_HEADS * GQA_HEAD_DIM), BF16),
            compiler_params=_cparams("parallel", "arbitrary"),
            name="gqa_attn",
        )(q_gqa, k_gqa, v_gqa, sink3, k_gqa, v_gqa)
        o_gqa = pl.pallas_call(
            functools.partial(_gqa_attn_kernel, seq=n_ctx, n_sub=n_ctx // WINDOW, latent=False),
            grid=(bsz,),
            in_specs=[
                pl.BlockSpec((n_ctx, GQA_HEADS * GQA_HEAD_DIM), lambda b: (ctx_blk0 + b, 0)),
                pl.BlockSpec((n_ctx, 2 * LANES), lambda b: (ctx_blk0 + b, 0)),
                pl.BlockSpec((n_ctx, 4 * LANES), lambda b: (ctx_blk0 + b, 0)),
                pl.BlockSpec((None, 1, GQA_HEADS), lambda b: (l, 0, 0)),
                pl.BlockSpec(memory_space=pl.ANY),
            ],
            out_specs=pl.BlockSpec((n_ctx, GQA_HEADS * GQA_HEAD_DIM), lambda b: (ctx_blk0 + b, 0)),
            out_shape=jax.ShapeDtypeStruct((m_all, GQA_HEADS * GQA_HEAD_DIM), BF16),
            input_output_aliases={4: 0},
            compiler_params=_cparams("parallel"),
            name="gqa_attn_ctx",
        )(q_gqa, k_gqa, v_gqa, sink3, o_gqa)

        def conv_call(length, blk0, prev):
            crc = 64
            in_specs = [
                pl.BlockSpec((length, CONV_CH), lambda b: (blk0 + b, P_CONV_A // CONV_CH)),
                pl.BlockSpec((length, CONV_CH), lambda b: (blk0 + b, P_CONV_B // CONV_CH)),
                pl.BlockSpec((None, CONV_WIDTH, CONV_CH), lambda b: (l, 0, 0)),
                pl.BlockSpec((None, 1, CONV_CH), lambda b: (l, 0, 0)),
                pl.BlockSpec((None, 1, CONV_CH), lambda b: (l, 0, 0)),
                pl.BlockSpec((None, 1, CONV_CH), lambda b: (l, 0, 0)),
            ]
            args = [p, p, conv_w, conv_b3, ln_g3, ln_b3]
            aliases = {}
            if prev is not None:
                in_specs.append(pl.BlockSpec(memory_space=pl.ANY))
                args.append(prev)
                aliases = {6: 0}

            def body(a_ref, b_ref, w_ref, cb_ref, lg_ref, lb_ref, *rest):
                o_ref, u_ref, win_ref = rest[-3:]
                _conv_kernel(a_ref, b_ref, w_ref, cb_ref, lg_ref, lb_ref, o_ref, u_ref, win_ref,
                             seq=length, rc=crc)

            return pl.pallas_call(
                body,
                grid=(bsz,),
                in_specs=in_specs,
                out_specs=pl.BlockSpec((length, CONV_CH), lambda b: (blk0 + b, 0)),
                out_shape=jax.ShapeDtypeStruct((m_all, CONV_CH), BF16),
                scratch_shapes=[pltpu.VMEM((length + 32, CONV_CH), F32),
                                pltpu.VMEM((crc + 32, CONV_CH), F32)],
                input_output_aliases=aliases,
                compiler_params=_cparams("parallel"),
                name="conv_branch" if prev is None else "conv_branch_ctx",
            )(*args)

        o_conv = conv_call(seq, 0, None)
        o_conv = conv_call(n_ctx, ctx_blk0, o_conv)

        gate_blk = P_GATE // tn
        nb = d // tn
        y = pl.pallas_call(
            _merge_kernel,
            grid=(m_all // tm, nb),
            in_specs=[
                pl.BlockSpec((tm, CONV_CH), lambda i, j: (i, 0)),
                pl.BlockSpec((tm, MLA_HEADS * MLA_V), lambda i, j: (i, 0)),
                pl.BlockSpec((tm, GQA_HEADS * GQA_HEAD_DIM), lambda i, j: (i, 0)),
                pl.BlockSpec((None, CONV_CH, tn), lambda i, j: (l, 0, j)),
                pl.BlockSpec((None, MLA_HEADS * MLA_V, tn), lambda i, j: (l, 0, j)),
                pl.BlockSpec((None, GQA_HEADS * GQA_HEAD_DIM, tn), lambda i, j: (l, 0, j)),
                pl.BlockSpec((tm, tn), lambda i, j: (i, gate_blk + j)),
                pl.BlockSpec((tm, tn), lambda i, j: (i, gate_blk + nb + j)),
                pl.BlockSpec((tm, tn), lambda i, j: (i, gate_blk + 2 * nb + j)),
            ],
            out_specs=pl.BlockSpec((tm, tn), lambda i, j: (i, j)),
            out_shape=jax.ShapeDtypeStruct((m_all, d), BF16),
            compiler_params=_cparams("parallel", "parallel"),
            name="merge",
        )(o_conv, o_mla, o_gqa, w_conv_out_b, w_mla_out_b, w_gqa_out_b, p, p, p)

        def mm_res(a, w, xin, part, tile, name):
            kdim = a.shape[1]
            return pl.pallas_call(
                _mm_res_kernel,
                grid=(m_all // tile, d // tn),
                in_specs=[
                    pl.BlockSpec((tile, kdim), lambda i, j: (i, 0)),
                    pl.BlockSpec((None, kdim, tn), lambda i, j: (l, 0, j)),
                    pl.BlockSpec((tile, tn), lambda i, j: (i, j)),
                    mod_spec(l, part, tile, tn, True),
                ],
                out_specs=pl.BlockSpec((tile, tn), lambda i, j: (i, j)),
                out_shape=jax.ShapeDtypeStruct((m_all, d), F32),
                compiler_params=_cparams("parallel", "parallel"),
                name=name,
            )(a, w, xin, mod)

        xa = mm_res(y, w_out_b, xa, 2, tm, "out_proj")

        nf = d_ff // tn
        act = pl.pallas_call(
            functools.partial(_ffn_in_kernel, rc=rc),
            grid=(m_all // tm, nf),
            in_specs=[
                pl.BlockSpec((tm, d), lambda i, j: (i, 0)),
                pl.BlockSpec((None, 1, d), lambda i, j: (l, 0, 0)),
                mod_spec(l, 3, tm, d, False),
                mod_spec(l, 4, tm, d, False),
                pl.BlockSpec((None, d, tn), lambda i, j: (l, 0, j)),
                pl.BlockSpec((None, d, tn), lambda i, j: (l, 0, nf + j)),
            ],
            out_specs=pl.BlockSpec((tm, tn), lambda i, j: (i, j)),
            out_shape=jax.ShapeDtypeStruct((m_all, d_ff), BF16),
            scratch_shapes=[pltpu.VMEM((tm, d), BF16)],
            compiler_params=_cparams("parallel", "arbitrary"),
            name="ffn_in",
        )(xa, g_ffn3, mod, mod, w_ffn_in_b, w_ffn_in_b)
        xa = mm_res(act, w_ffn_out_b, xa, 5, tm, "ffn_out")

    out = pl.pallas_call(
        functools.partial(_final_norm_kernel, rc=rc),
        grid=(m_lat // tm,),
        in_specs=[pl.BlockSpec((tm, d), lambda i: (i, 0)),
                  pl.BlockSpec((1, d), lambda i: (0, 0))],
        out_specs=pl.BlockSpec((tm, d), lambda i: (i, 0)),
        out_shape=jax.ShapeDtypeStruct((m_lat, d), F32),
        compiler_params=_cparams("parallel"),
        name="final_norm",
    )(xa, g_final.reshape(1, d))
    return out.reshape(bsz, seq, d)
```

```python
import functools

import jax
import jax.numpy as jnp
from jax import lax
from jax.experimental import pallas as pl
from jax.experimental.pallas import tpu as pltpu

F32 = jnp.float32
BF16 = jnp.bfloat16

GRID_W = 64
CONV_CH = 512
CONV_WIDTH = 31
MLA_HEADS = 8
MLA_Q_RANK = 512
MLA_KV_RANK = 256
MLA_NOPE = 128
MLA_ROPE = 64
MLA_V = 128
GQA_HEADS = 8
GQA_KV_HEADS = 2
GQA_HEAD_DIM = 64
WINDOW = 128
N_BRANCH = 3
ROPE_DIM = 64
ROPE_BASE = 10000.0
EPS = 1e-6
NEG_INF = -1e30

COL_MLA_KV = 0
COL_MLA_KR = COL_MLA_KV + MLA_KV_RANK
COL_GQA_K = COL_MLA_KR + MLA_ROPE
COL_GQA_V = COL_GQA_K + GQA_KV_HEADS * GQA_HEAD_DIM
KV_COLS = COL_GQA_V + GQA_KV_HEADS * GQA_HEAD_DIM
COL_MLA_Q = KV_COLS
COL_GQA_Q = COL_MLA_Q + MLA_Q_RANK
COL_CONV = COL_GQA_Q + GQA_HEADS * GQA_HEAD_DIM
COL_GATE = COL_CONV + 2 * CONV_CH

LANES = 128
MLA_PAD = 256
VMEM_LIMIT = 56 * 1024 * 1024
MLA_TQ = 1024
MLA_RG = 64
MLA_TK = 512
MLA_UNROLL = 3
LOG2_E = 1.4426950408889634

P_MLA_Q = 0
P_GQA_Q = 512
P_CONV_A = 1024
P_CONV_B = 1536
P_CKV = 2048
P_KR = 2304
P_GQA_K = 2432
P_GQA_V = 2560
P_SMALL = 3072


def _cparams(*sem):
    return pltpu.CompilerParams(dimension_semantics=sem, vmem_limit_bytes=VMEM_LIMIT)


def _for_chunks(n_rows, rc, fn):
    n = n_rows // rc
    if n == 1:
        fn(0)
        return

    def body(i, carry):
        fn(pl.multiple_of(i * rc, rc))
        return carry

    lax.fori_loop(0, n, body, 0)


def _sigmoid(x):
    return 0.5 * jnp.tanh(0.5 * x) + 0.5


def _rope(x, cos, sin_up, sin_dn):
    return x * cos + pltpu.roll(x, LANES - 16, 1) * sin_up + pltpu.roll(x, 16, 1) * sin_dn


def _dot(a, b):
    return jnp.dot(a, b, preferred_element_type=F32)


def _dot_nt(a, b):
    return lax.dot_general(a, b, (((1,), (1,)), ((), ())), preferred_element_type=F32)


def _ada_kernel(c_ref, w_ref, b_ref, o_ref):
    c = c_ref[...]
    s = c * _sigmoid(c)
    o_ref[...] = jnp.dot(s, w_ref[...], preferred_element_type=F32,
                         precision=lax.Precision.HIGHEST) + b_ref[...]


def _ada(cc, w_ada, b_ada):
    depth, d, n = w_ada.shape
    rows = cc.shape[0]
    tn = 1024
    return pl.pallas_call(
        _ada_kernel,
        grid=(depth, n // tn),
        in_specs=[
            pl.BlockSpec((rows, d), lambda l, j: (0, 0)),
            pl.BlockSpec((None, d, tn), lambda l, j: (l, 0, j)),
            pl.BlockSpec((None, 1, tn), lambda l, j: (l, 0, j)),
        ],
        out_specs=pl.BlockSpec((None, rows, tn), lambda l, j: (l, 0, j)),
        out_shape=jax.ShapeDtypeStruct((depth, rows, n), F32),
        compiler_params=_cparams("parallel", "parallel"),
        name="ada_mod",
    )(cc, w_ada, b_ada.reshape(depth, 1, n))


def _norm_mod_store(x_ref, g_ref, sh_ref, sc_ref, h_ref, rc):
    gs = g_ref[...] * (1.0 + sc_ref[...])
    sh = sh_ref[...]

    def chunk(r0):
        x = x_ref[pl.ds(r0, rc), :]
        ms = jnp.mean(x * x, axis=-1, keepdims=True)
        h_ref[pl.ds(r0, rc), :] = (x * lax.rsqrt(ms + EPS) * gs + sh).astype(BF16)

    _for_chunks(x_ref.shape[0], rc, chunk)


def _in_proj_kernel(x_ref, g_ref, sh_ref, sc_ref, w_ref, *rest, n_gate, col0, rc):
    o_ref, h_ref = rest[-2:]
    j = pl.program_id(1)

    @pl.when(j == 0)
    def _():
        _norm_mod_store(x_ref, g_ref, sh_ref, sc_ref, h_ref, rc)

    acc = _dot(h_ref[...], w_ref[...])
    o_ref[...] = jnp.where(col0 + j < n_gate, _sigmoid(acc), acc).astype(BF16)


def _ffn_in_kernel(x_ref, g_ref, sh_ref, sc_ref, w1_ref, w2_ref, o_ref, h_ref, *, rc):
    @pl.when(pl.program_id(1) == 0)
    def _():
        _norm_mod_store(x_ref, g_ref, sh_ref, sc_ref, h_ref, rc)

    h = h_ref[...]
    u1 = _dot(h, w1_ref[...])
    u2 = _dot(h, w2_ref[...])
    o_ref[...] = (u1 * _sigmoid(u1) * u2).astype(BF16)


def _mm_res_kernel(a_ref, w_ref, x_ref, gt_ref, o_ref):
    o_ref[...] = x_ref[...] + gt_ref[...] * _dot(a_ref[...], w_ref[...])


def _merge_kernel(oc_ref, om_ref, og_ref, wc_ref, wm_ref, wg_ref, g0_ref, g1_ref, g2_ref, y_ref):
    y = g0_ref[...].astype(F32) * _dot(oc_ref[...], wc_ref[...])
    y += g1_ref[...].astype(F32) * _dot(om_ref[...], wm_ref[...])
    y += g2_ref[...].astype(F32) * _dot(og_ref[...], wg_ref[...])
    y_ref[...] = y.astype(BF16)


def _rms_bf16(a, g):
    ms = jnp.mean(a * a, axis=-1, keepdims=True)
    return (a * lax.rsqrt(ms + EPS) * g).astype(BF16)


def _qup_kernel(pq_ref, g_ref, wq_ref, cos_ref, su_ref, sd_ref, q_ref, qg_ref, *, rc, s_mla, s_gqa):
    g = g_ref[...]

    def chunk(r0):
        rows = pl.ds(r0, rc)
        n = _rms_bf16(pq_ref[rows, 0:MLA_Q_RANK].astype(F32), g)
        q = _dot(n, wq_ref[...])
        cos = cos_ref[rows, :]
        su = su_ref[rows, :]
        sd = sd_ref[rows, :]
        for h in range(MLA_HEADS):
            c0 = h * MLA_PAD
            q_ref[rows, c0:c0 + LANES] = (q[:, c0:c0 + LANES] * s_mla).astype(BF16)
            r = _rope(q[:, c0 + LANES:c0 + 2 * LANES], cos, su, sd)
            q_ref[rows, c0 + LANES:c0 + 2 * LANES] = (r * s_mla).astype(BF16)
        for t in range(GQA_HEADS * GQA_HEAD_DIM // LANES):
            gq = pq_ref[rows, MLA_Q_RANK + t * LANES:MLA_Q_RANK + (t + 1) * LANES].astype(F32)
            qg_ref[rows, t * LANES:(t + 1) * LANES] = (_rope(gq, cos, su, sd) * s_gqa).astype(BF16)

    _for_chunks(pq_ref.shape[0], rc, chunk)


def _kvup_kernel(ckv_ref, kr_ref, g_ref, wk_ref, wv_ref, cos_ref, su_ref, sd_ref,
                 k_ref, v_ref, kg_ref, vg_ref, *, rc):
    g = g_ref[...]

    def chunk(r0):
        rows = pl.ds(r0, rc)
        n = _rms_bf16(ckv_ref[rows, :].astype(F32), g)
        kn = _dot(n, wk_ref[...])
        vv = _dot(n, wv_ref[...])
        cos = cos_ref[rows, :]
        su = su_ref[rows, :]
        sd = sd_ref[rows, :]
        kr = _rope(kr_ref[rows, 0:LANES].astype(F32), cos, su, sd).astype(BF16)
        ones_col = jnp.where(lax.broadcasted_iota(jnp.int32, (rc, LANES), 1) == 0, 1.0, 0.0).astype(BF16)
        for h in range(MLA_HEADS):
            c0 = h * MLA_PAD
            k_ref[rows, c0:c0 + LANES] = kn[:, h * LANES:(h + 1) * LANES].astype(BF16)
            k_ref[rows, c0 + LANES:c0 + 2 * LANES] = kr
            v_ref[rows, c0:c0 + LANES] = vv[:, h * LANES:(h + 1) * LANES].astype(BF16)
            v_ref[rows, c0 + LANES:c0 + 2 * LANES] = ones_col
        low_half = lax.broadcasted_iota(jnp.int32, (rc, LANES), 1) < GQA_HEAD_DIM
        gk = _rope(kr_ref[rows, LANES:2 * LANES].astype(F32), cos, su, sd)
        gk_sw = pltpu.roll(gk, GQA_HEAD_DIM, 1)
        kg_ref[rows, 0:LANES] = jnp.where(low_half, gk, gk_sw).astype(BF16)
        kg_ref[rows, LANES:2 * LANES] = jnp.where(low_half, gk_sw, gk).astype(BF16)
        gv = kr_ref[rows, 2 * LANES:3 * LANES].astype(F32)
        gv_sw = pltpu.roll(gv, GQA_HEAD_DIM, 1)
        vg_ref[rows, 0:LANES] = jnp.where(low_half, gv, gv_sw).astype(BF16)
        vg_ref[rows, LANES:2 * LANES] = ones_col
        vg_ref[rows, 2 * LANES:3 * LANES] = jnp.where(low_half, gv_sw, gv).astype(BF16)
        vg_ref[rows, 3 * LANES:4 * LANES] = ones_col

    _for_chunks(ckv_ref.shape[0], rc, chunk)


def _mla_attn_kernel(q_ref, k_ref, v_ref, *rest, tk, rg, unroll):
    n_buf = 3 * unroll
    o_ref = rest[-n_buf - 3]
    bufs = rest[-n_buf - 2:-2]
    s_buf, p_buf, a_buf = bufs[:unroll], bufs[unroll:2 * unroll], bufs[2 * unroll:]
    m_ref, acc_ref = rest[-2:]
    tq = q_ref.shape[0]
    head = k_ref.shape[0] % tk
    has_head = 1 if head else 0
    n = k_ref.shape[0] // tk + has_head

    def start_of(c):
        return head + (c - has_head) * tk

    def size_of(c):
        return head if (has_head and c == 0) else tk

    def qk(slot, start, size):
        s_buf[slot][:, 0:size] = _dot_nt(q_ref[...], k_ref[pl.ds(start, size), :])

    def sm(slot, size):
        for r in range(0, tq, rg):
            s = s_buf[slot][r:r + rg, 0:size]
            m_old = m_ref[r:r + rg, :]
            m_new = jnp.maximum(m_old, jnp.max(s, axis=-1, keepdims=True))
            p_buf[slot][r:r + rg, 0:size] = jnp.exp2(s - pltpu.repeat(m_new, size // LANES, 1)).astype(BF16)
            a_buf[slot][r:r + rg, :] = jnp.exp2(m_old - m_new)
            m_ref[r:r + rg, :] = m_new

    def pv(slot, start, size):
        alpha = pltpu.repeat(a_buf[slot][...], acc_ref.shape[1] // LANES, 1)
        acc_ref[...] = alpha * acc_ref[...] + _dot(p_buf[slot][:, 0:size], v_ref[pl.ds(start, size), :])

    def static_start(c):
        return 0 if (has_head and c == 0) else start_of(c)

    def static_tick(t):
        if 0 <= t < n:
            qk(t % unroll, static_start(t), size_of(t))
        if 0 <= t - 2 < n:
            pv((t - 2) % unroll, static_start(t - 2), size_of(t - 2))
        if 0 <= t - 1 < n:
            sm((t - 1) % unroll, size_of(t - 1))

    m_ref[...] = jnp.full(m_ref.shape, NEG_INF, F32)
    acc_ref[...] = jnp.zeros(acc_ref.shape, F32)

    first_steady = 2 + has_head
    trips = max(n - first_steady, 0) // unroll
    for t in range(first_steady):
        static_tick(t)
    if trips:
        def trip(j, carry):
            for u in range(unroll):
                t = first_steady + unroll * j + u
                qk((first_steady + u) % unroll, pl.multiple_of(start_of(t), 2 * LANES), tk)
                pv((first_steady + u - 2) % unroll, pl.multiple_of(start_of(t - 2), 2 * LANES), tk)
                sm((first_steady + u - 1) % unroll, tk)
            return carry

        lax.fori_loop(0, trips, trip, 0)
    for t in range(first_steady + unroll * trips, n + 2):
        static_tick(t)
    acc = acc_ref[...]
    o_ref[...] = (acc[:, :MLA_V] / acc[:, MLA_V:MLA_V + 1]).astype(BF16)


def _gqa_attn_kernel(*refs, seq, n_sub, latent):
    if latent:
        q_ref, kc_ref, vc_ref, sink_ref, kl_ref, vl_ref, o_ref = refs
    else:
        q_ref, kc_ref, vc_ref, sink_ref, _, o_ref = refs
    span = 3 * WINDOW
    rep = GQA_HEADS // GQA_KV_HEADS
    tile_q0 = pl.program_id(1) * (n_sub * WINDOW) if latent else 0
    lane = lax.broadcasted_iota(jnp.int32, (1, LANES), 1)
    low_half = lane < GQA_HEAD_DIM
    half_masks = (jnp.where(low_half, 1.0, 0.0).astype(BF16), jnp.where(low_half, 0.0, 1.0).astype(BF16))
    sinks = sink_ref[...] * LOG2_E
    sinks = [sinks[:, h:h + 1] for h in range(GQA_HEADS)]

    def sub(r0):
        rows = pl.ds(r0, WINDOW)
        if latent:
            q0 = tile_q0 + r0
            start = pl.multiple_of(jnp.clip(q0 - WINDOW, 0, seq - span), WINDOW)
            qpos = q0 + lax.broadcasted_iota(jnp.int32, (WINDOW, span), 0)
            kpos = start + lax.broadcasted_iota(jnp.int32, (WINDOW, span), 1)
            bias = jnp.where(jnp.abs(qpos - kpos) <= WINDOW, 0.0, NEG_INF)
        sc, sw = [], []
        for h in range(GQA_HEADS):
            g = h // rep
            q = q_ref[rows, (h // 2) * LANES:(h // 2 + 1) * LANES] * half_masks[h % 2]
            sc.append(_dot_nt(q, kc_ref[:, g * LANES:(g + 1) * LANES]))
            if latent:
                sw.append(_dot_nt(q, kl_ref[pl.ds(start, span), g * LANES:(g + 1) * LANES]) + bias)
        ms, pc, pw = [], [], []
        for h in range(GQA_HEADS):
            m = jnp.maximum(jnp.max(sc[h], axis=-1, keepdims=True), sinks[h])
            if latent:
                m = jnp.maximum(m, jnp.max(sw[h], axis=-1, keepdims=True))
                pw.append(jnp.exp2(sw[h] - m).astype(BF16))
            pc.append(jnp.exp2(sc[h] - m).astype(BF16))
            ms.append(m)
        outs = []
        for h in range(GQA_HEADS):
            g = h // rep
            o = _dot(pc[h], vc_ref[:, g * 2 * LANES:(g + 1) * 2 * LANES])
            if latent:
                o = o + _dot(pw[h], vl_ref[pl.ds(start, span), g * 2 * LANES:(g + 1) * 2 * LANES])
            den = o[:, LANES:LANES + 1] + jnp.exp2(sinks[h] - ms[h])
            outs.append(o[:, :LANES] / den)
        for t in range(GQA_HEADS // 2):
            o_ref[rows, t * LANES:(t + 1) * LANES] = jnp.where(low_half, outs[2 * t], outs[2 * t + 1]).astype(BF16)

    _for_chunks(n_sub * WINDOW, WINDOW, sub)


def _conv_kernel(a_ref, b_ref, w_ref, cb_ref, lg_ref, lb_ref, o_ref, u_ref, win_ref, *, seq, rc):
    halo = 16
    u_ref[0:halo, :] = jnp.zeros((halo, CONV_CH), F32)
    u_ref[halo + seq:2 * halo + seq, :] = jnp.zeros((halo, CONV_CH), F32)

    def fill(r0):
        a = a_ref[pl.ds(r0, 256), :].astype(F32)
        b = b_ref[pl.ds(r0, 256), :].astype(F32)
        u_ref[pl.ds(halo + r0, 256), :] = a * _sigmoid(b)

    _for_chunks(seq, 256, fill)

    cb = cb_ref[...]
    lg = lg_ref[...]
    lb = lb_ref[...]

    half = CONV_CH // 2
    first_tap = halo - CONV_WIDTH // 2

    def conv(r0):
        win_ref[...] = u_ref[pl.ds(r0, rc + 2 * halo), :]
        parts = []
        for c0 in (0, half):
            acc = jnp.zeros((rc, half), F32) + cb[:, c0:c0 + half]
            for s in range(8):
                shifted = win_ref[s:s + rc + 24, c0:c0 + half]
                for a in range(4):
                    k = 8 * a + s - first_tap
                    if 0 <= k < CONV_WIDTH:
                        acc = acc + w_ref[k:k + 1, c0:c0 + half] * shifted[8 * a:8 * a + rc]
            parts.append(acc)
        acc = jnp.concatenate(parts, axis=-1)
        mu = jnp.mean(acc, axis=-1, keepdims=True)
        xc = acc - mu
        var = jnp.mean(xc * xc, axis=-1, keepdims=True)
        y = xc * lax.rsqrt(var + EPS) * lg + lb
        o_ref[pl.ds(r0, rc), :] = (y * _sigmoid(y)).astype(BF16)

    _for_chunks(seq, rc, conv)


def _final_norm_kernel(x_ref, g_ref, o_ref, *, rc):
    g = g_ref[...]

    def chunk(r0):
        x = x_ref[pl.ds(r0, rc), :]
        ms = jnp.mean(x * x, axis=-1, keepdims=True)
        o_ref[pl.ds(r0, rc), :] = x * lax.rsqrt(ms + EPS) * g

    _for_chunks(x_ref.shape[0], rc, chunk)


def _rope_tables(length, extra):
    rows = length // GRID_W
    row = jnp.repeat(jnp.arange(rows), GRID_W).astype(F32)
    col = jnp.tile(jnp.arange(GRID_W), rows).astype(F32)
    n_freq = ROPE_DIM // 4
    inv_freq = ROPE_BASE ** (-jnp.arange(n_freq, dtype=F32) / n_freq)
    a_row = row[:, None] * inv_freq[None, :]
    a_col = col[:, None] * inv_freq[None, :]
    ang = jnp.concatenate([a_row, a_row, a_col, a_col], axis=-1)
    cos = jnp.concatenate([jnp.cos(ang), jnp.ones((extra, ROPE_DIM), F32)], axis=0)
    sin = jnp.concatenate([jnp.sin(ang), jnp.zeros((extra, ROPE_DIM), F32)], axis=0)
    cos = jnp.tile(cos, (1, LANES // ROPE_DIM))
    sin = jnp.tile(sin, (1, LANES // ROPE_DIM))
    first = (jnp.arange(LANES) // 16) % 2 == 0
    sin_up = jnp.where(first[None, :], -sin, 0.0)
    sin_dn = jnp.where(first[None, :], 0.0, sin)
    return cos, sin_up, sin_dn


def _pick_tile(*sizes):
    for t in (1024, 512, 256):
        if all(s % t == 0 for s in sizes):
            return t
    raise ValueError(f"unsupported row counts {sizes}")


def kernel(x, c, ctx, c_ctx, w_ada, b_ada, g_mix, w_in, conv_w, conv_b, conv_ln_g, conv_ln_b, w_conv_out, g_q_a, w_q_up, g_kv_a, w_kv_up, w_mla_out, gqa_sink, w_gqa_out, w_out, g_ffn, w_ffn_in, w_ffn_out, g_final):
    bsz, seq, d = x.shape
    n_ctx = ctx.shape[1]
    depth = w_ada.shape[0]
    d_ff = w_ffn_out.shape[1]
    m_lat = bsz * seq
    m_ctx = bsz * n_ctx
    m_all = m_lat + m_ctx
    assert seq % GRID_W == 0 and n_ctx == 2 * WINDOW and seq % (4 * WINDOW) == 0
    assert bsz + 1 <= 16 and w_in.shape[2] == COL_GATE + N_BRANCH * d

    tm = _pick_tile(seq, m_ctx)
    ts = min(tm, 512)
    tn = 512
    rc = 256
    n_lat_tiles = m_lat // tm
    tiles_per_batch = seq // tm
    p_base = N_BRANCH * d
    p_cols = p_base + P_SMALL
    tn_in = 1024
    assert (p_base + P_CKV) % tn_in == 0 and P_SMALL - P_CKV == tn_in
    ctx_blk0 = m_lat // n_ctx

    def mod_row(i, tile):
        return jnp.minimum(i // (seq // tile), bsz)

    def cols(a, n):
        return w_in[:, :, a:a + n].astype(BF16)

    def zeros(n):
        return jnp.zeros((depth, d, n), BF16)

    w_in_r = jnp.concatenate([
        cols(COL_GATE, N_BRANCH * d),
        cols(COL_MLA_Q, MLA_Q_RANK), cols(COL_GQA_Q, GQA_HEADS * GQA_HEAD_DIM),
        cols(COL_CONV, CONV_CH), cols(COL_CONV + CONV_CH, CONV_CH),
        cols(COL_MLA_KV, MLA_KV_RANK), cols(COL_MLA_KR, MLA_ROPE), zeros(LANES - MLA_ROPE),
        cols(COL_GQA_K, GQA_KV_HEADS * GQA_HEAD_DIM), cols(COL_GQA_V, GQA_KV_HEADS * GQA_HEAD_DIM),
        zeros(P_SMALL - P_GQA_V - GQA_KV_HEADS * GQA_HEAD_DIM)], axis=-1)
    wq = w_q_up.reshape(depth, MLA_Q_RANK, MLA_HEADS, MLA_NOPE + MLA_ROPE).astype(BF16)
    wq = jnp.pad(wq, ((0, 0), (0, 0), (0, 0), (0, MLA_PAD - MLA_NOPE - MLA_ROPE)))
    wq = wq.reshape(depth, MLA_Q_RANK, MLA_HEADS * MLA_PAD)
    wkv = w_kv_up.reshape(depth, MLA_KV_RANK, MLA_HEADS, MLA_NOPE + MLA_V).astype(BF16)
    wk = wkv[..., :MLA_NOPE].reshape(depth, MLA_KV_RANK, MLA_HEADS * MLA_NOPE)
    wv = wkv[..., MLA_NOPE:].reshape(depth, MLA_KV_RANK, MLA_HEADS * MLA_V)
    w_conv_out_b = w_conv_out.astype(BF16)
    w_mla_out_b = w_mla_out.astype(BF16)
    w_gqa_out_b = w_gqa_out.astype(BF16)
    w_out_b = w_out.astype(BF16)
    w_ffn_in_b = w_ffn_in.astype(BF16)
    w_ffn_out_b = w_ffn_out.astype(BF16)
    cos, sin_up, sin_dn = _rope_tables(seq, ts)

    cc = jnp.zeros((16, d), F32).at[:bsz].set(c).at[bsz].set(c_ctx)
    mod = _ada(cc, w_ada, b_ada).reshape(depth, 16, 1, 6 * d)

    xa = jnp.concatenate([x.reshape(m_lat, d), ctx.reshape(m_ctx, d)], axis=0)

    def vec(a):
        return a.reshape(depth, 1, a.shape[-1])

    g_mix3, g_ffn3, g_q3, g_kv3 = vec(g_mix), vec(g_ffn), vec(g_q_a), vec(g_kv_a)
    conv_b3, ln_g3, ln_b3, sink3 = vec(conv_b), vec(conv_ln_g), vec(conv_ln_b), vec(gqa_sink)

    def mod_spec(l, part, tile, width, with_j, tile0=0):
        nb = d // width
        if with_j:
            return pl.BlockSpec((None, None, 1, width),
                                lambda i, j: (l, mod_row(tile0 + i, tile), 0, part * nb + j))
        return pl.BlockSpec((None, None, 1, width), lambda i, j: (l, mod_row(tile0 + i, tile), 0, part))

    def table_specs(tile):
        per_batch = seq // tile
        n_lat = m_lat // tile
        return [pl.BlockSpec((tile, LANES), lambda i: (jnp.where(i < n_lat, i % per_batch, per_batch), 0))] * 3

    for l in range(depth):
        last = l == depth - 1
        m_rows = m_lat if last else m_all

        def in_proj_call(row_tile0, n_row_tiles, col_tile0, n_col_tiles, prev):
            in_specs = [
                pl.BlockSpec((tm, d), lambda i, j: (row_tile0 + i, 0)),
                pl.BlockSpec((None, 1, d), lambda i, j: (l, 0, 0)),
                mod_spec(l, 0, tm, d, False, row_tile0),
                mod_spec(l, 1, tm, d, False, row_tile0),
                pl.BlockSpec((None, d, tn_in), lambda i, j: (l, 0, col_tile0 + j)),
            ]
            args = [xa, g_mix3, mod, mod, w_in_r]
            aliases = {}
            if prev is not None:
                in_specs.append(pl.BlockSpec(memory_space=pl.ANY))
                args.append(prev)
                aliases = {5: 0}
            return pl.pallas_call(
                functools.partial(_in_proj_kernel, n_gate=p_base // tn_in, col0=col_tile0, rc=rc),
                grid=(n_row_tiles, n_col_tiles),
                in_specs=in_specs,
                out_specs=pl.BlockSpec((tm, tn_in), lambda i, j: (row_tile0 + i, col_tile0 + j)),
                out_shape=jax.ShapeDtypeStruct((m_all, p_cols), BF16),
                scratch_shapes=[pltpu.VMEM((tm, d), BF16)],
                input_output_aliases=aliases,
                compiler_params=_cparams("parallel", "arbitrary"),
                name="in_proj" if prev is None else "in_proj_ctx_kv",
            )(*args)

        if last:
            p = in_proj_call(0, n_lat_tiles, 0, p_cols // tn_in, None)
            p = in_proj_call(n_lat_tiles, m_ctx // tm, (p_base + P_CKV) // tn_in, 1, p)
        else:
            p = in_proj_call(0, m_all // tm, 0, p_cols // tn_in, None)

        q_mla, q_gqa = pl.pallas_call(
            functools.partial(_qup_kernel, rc=rc, s_mla=float(LOG2_E * (MLA_NOPE + MLA_ROPE) ** -0.5),
                              s_gqa=float(LOG2_E * GQA_HEAD_DIM ** -0.5)),
            grid=(m_rows // ts,),
            in_specs=[
                pl.BlockSpec((ts, 1024), lambda i: (i, (p_base + P_MLA_Q) // 1024)),
                pl.BlockSpec((None, 1, MLA_Q_RANK), lambda i: (l, 0, 0)),
                pl.BlockSpec((None, MLA_Q_RANK, MLA_HEADS * MLA_PAD), lambda i: (l, 0, 0)),
            ] + table_specs(ts),
            out_specs=[pl.BlockSpec((ts, MLA_HEADS * MLA_PAD), lambda i: (i, 0)),
                       pl.BlockSpec((ts, GQA_HEADS * GQA_HEAD_DIM), lambda i: (i, 0))],
            out_shape=[jax.ShapeDtypeStruct((m_all, MLA_HEADS * MLA_PAD), BF16),
                       jax.ShapeDtypeStruct((m_all, GQA_HEADS * GQA_HEAD_DIM), BF16)],
            compiler_params=_cparams("parallel"),
            name="q_up",
        )(p, g_q3, wq, cos, sin_up, sin_dn)

        tkv = n_ctx
        kv_per_batch = seq // tkv
        kv_lat_tiles = m_lat // tkv

        def kv_block(i):
            lat = (i // kv_per_batch) * (kv_per_batch + 1) + 1 + i % kv_per_batch
            return jnp.where(i < kv_lat_tiles, lat, (i - kv_lat_tiles) * (kv_per_batch + 1))

        k_mla, v_mla, k_gqa, v_gqa = pl.pallas_call(
            functools.partial(_kvup_kernel, rc=tkv),
            grid=(m_all // tkv,),
            in_specs=[
                pl.BlockSpec((tkv, MLA_KV_RANK), lambda i: (i, (p_base + P_CKV) // MLA_KV_RANK)),
                pl.BlockSpec((tkv, 3 * LANES), lambda i: (i, (p_base + P_KR) // (3 * LANES))),
                pl.BlockSpec((None, 1, MLA_KV_RANK), lambda i: (l, 0, 0)),
                pl.BlockSpec((None, MLA_KV_RANK, MLA_HEADS * MLA_NOPE), lambda i: (l, 0, 0)),
                pl.BlockSpec((None, MLA_KV_RANK, MLA_HEADS * MLA_V), lambda i: (l, 0, 0)),
            ] + table_specs(tkv),
            out_specs=[pl.BlockSpec((tkv, MLA_HEADS * MLA_PAD), lambda i: (kv_block(i), 0)),
                       pl.BlockSpec((tkv, MLA_HEADS * MLA_PAD), lambda i: (kv_block(i), 0)),
                       pl.BlockSpec((tkv, 2 * LANES), lambda i: (i, 0)),
                       pl.BlockSpec((tkv, 4 * LANES), lambda i: (i, 0))],
            out_shape=[jax.ShapeDtypeStruct((m_all, MLA_HEADS * MLA_PAD), BF16),
                       jax.ShapeDtypeStruct((m_all, MLA_HEADS * MLA_PAD), BF16),
                       jax.ShapeDtypeStruct((m_all, 2 * LANES), BF16),
                       jax.ShapeDtypeStruct((m_all, 4 * LANES), BF16)],
            compiler_params=_cparams("parallel"),
            name="kv_up",
        )(p, p, g_kv3, wk, wv, cos, sin_up, sin_dn)

        tq = min(MLA_TQ, seq)
        nq = seq // tq
        def mla_scratch(rows, tk):
            return ([pltpu.VMEM((rows, tk), F32)] * MLA_UNROLL + [pltpu.VMEM((rows, tk), BF16)] * MLA_UNROLL
                    + [pltpu.VMEM((rows, LANES), F32)] * (MLA_UNROLL + 1) + [pltpu.VMEM((rows, MLA_PAD), F32)])

        o_mla = pl.pallas_call(
            functools.partial(_mla_attn_kernel, tk=MLA_TK, rg=MLA_RG, unroll=MLA_UNROLL),
            grid=(bsz, MLA_HEADS, nq),
            scratch_shapes=mla_scratch(tq, MLA_TK),
            in_specs=[
                pl.BlockSpec((tq, MLA_PAD), lambda b, h, i: (b * nq + i, h)),
                pl.BlockSpec((seq + n_ctx, MLA_PAD), lambda b, h, i: (b, h)),
                pl.BlockSpec((seq + n_ctx, MLA_PAD), lambda b, h, i: (b, h)),
            ],
            out_specs=pl.BlockSpec((tq, MLA_V), lambda b, h, i: (b * nq + i, h)),
            out_shape=jax.ShapeDtypeStruct((m_all, MLA_HEADS * MLA_V), BF16),
            compiler_params=_cparams("parallel", "parallel", "arbitrary"),
            name="mla_attn",
        )(q_mla, k_mla, v_mla)
        o_mla = o_mla if last else pl.pallas_call(
            functools.partial(_mla_attn_kernel, tk=n_ctx, rg=MLA_RG, unroll=MLA_UNROLL),
            grid=(bsz, MLA_HEADS),
            scratch_shapes=mla_scratch(n_ctx, n_ctx),
            in_specs=[
                pl.BlockSpec((n_ctx, MLA_PAD), lambda b, h: (ctx_blk0 + b, h)),
                pl.BlockSpec((n_ctx, MLA_PAD), lambda b, h: (b * (kv_per_batch + 1), h)),
                pl.BlockSpec((n_ctx, MLA_PAD), lambda b, h: (b * (kv_per_batch + 1), h)),
                pl.BlockSpec(memory_space=pl.ANY),
            ],
            out_specs=pl.BlockSpec((n_ctx, MLA_V), lambda b, h: (ctx_blk0 + b, h)),
            out_shape=jax.ShapeDtypeStruct((m_all, MLA_HEADS * MLA_V), BF16),
            input_output_aliases={3: 0},
            compiler_params=_cparams("parallel", "parallel"),
            name="mla_attn_ctx",
        )(q_mla, k_mla, v_mla, o_mla)

        tqg = 512
        nqg = seq // tqg
        o_gqa = pl.pallas_call(
            functools.partial(_gqa_attn_kernel, seq=seq, n_sub=tqg // WINDOW, latent=True),
            grid=(bsz, nqg),
            in_specs=[
                pl.BlockSpec((tqg, GQA_HEADS * GQA_HEAD_DIM), lambda b, i: (b * nqg + i, 0)),
                pl.BlockSpec((n_ctx, 2 * LANES), lambda b, i: (ctx_blk0 + b, 0)),
                pl.BlockSpec((n_ctx, 4 * LANES), lambda b, i: (ctx_blk0 + b, 0)),
                pl.BlockSpec((None, 1, GQA_HEADS), lambda b, i: (l, 0, 0)),
                pl.BlockSpec((seq, 2 * LANES), lambda b, i: (b, 0)),
                pl.BlockSpec((seq, 4 * LANES), lambda b, i: (b, 0)),
            ],
            out_specs=pl.BlockSpec((tqg, GQA_HEADS * GQA_HEAD_DIM), lambda b, i: (b * nqg + i, 0)),
            out_shape=jax.ShapeDtypeStruct((m_all, GQA_HEADS * GQA_HEAD_DIM), BF16),
            compiler_params=_cparams("parallel", "arbitrary"),
            name="gqa_attn",
        )(q_gqa, k_gqa, v_gqa, sink3, k_gqa, v_gqa)
        o_gqa = o_gqa if last else pl.pallas_call(
            functools.partial(_gqa_attn_kernel, seq=n_ctx, n_sub=n_ctx // WINDOW, latent=False),
            grid=(bsz,),
            in_specs=[
                pl.BlockSpec((n_ctx, GQA_HEADS * GQA_HEAD_DIM), lambda b: (ctx_blk0 + b, 0)),
                pl.BlockSpec((n_ctx, 2 * LANES), lambda b: (ctx_blk0 + b, 0)),
                pl.BlockSpec((n_ctx, 4 * LANES), lambda b: (ctx_blk0 + b, 0)),
                pl.BlockSpec((None, 1, GQA_HEADS), lambda b: (l, 0, 0)),
                pl.BlockSpec(memory_space=pl.ANY),
            ],
            out_specs=pl.BlockSpec((n_ctx, GQA_HEADS * GQA_HEAD_DIM), lambda b: (ctx_blk0 + b, 0)),
            out_shape=jax.ShapeDtypeStruct((m_all, GQA_HEADS * GQA_HEAD_DIM), BF16),
            input_output_aliases={4: 0},
            compiler_params=_cparams("parallel"),
            name="gqa_attn_ctx",
        )(q_gqa, k_gqa, v_gqa, sink3, o_gqa)

        def conv_call(length, blk0, prev):
            crc = 64
            in_specs = [
                pl.BlockSpec((length, CONV_CH), lambda b: (blk0 + b, (p_base + P_CONV_A) // CONV_CH)),
                pl.BlockSpec((length, CONV_CH), lambda b: (blk0 + b, (p_base + P_CONV_B) // CONV_CH)),
                pl.BlockSpec((None, CONV_WIDTH, CONV_CH), lambda b: (l, 0, 0)),
                pl.BlockSpec((None, 1, CONV_CH), lambda b: (l, 0, 0)),
                pl.BlockSpec((None, 1, CONV_CH), lambda b: (l, 0, 0)),
                pl.BlockSpec((None, 1, CONV_CH), lambda b: (l, 0, 0)),
            ]
            args = [p, p, conv_w, conv_b3, ln_g3, ln_b3]
            aliases = {}
            if prev is not None:
                in_specs.append(pl.BlockSpec(memory_space=pl.ANY))
                args.append(prev)
                aliases = {6: 0}

            def body(a_ref, b_ref, w_ref, cb_ref, lg_ref, lb_ref, *rest):
                o_ref, u_ref, win_ref = rest[-3:]
                _conv_kernel(a_ref, b_ref, w_ref, cb_ref, lg_ref, lb_ref, o_ref, u_ref, win_ref,
                             seq=length, rc=crc)

            return pl.pallas_call(
                body,
                grid=(bsz,),
                in_specs=in_specs,
                out_specs=pl.BlockSpec((length, CONV_CH), lambda b: (blk0 + b, 0)),
                out_shape=jax.ShapeDtypeStruct((m_all, CONV_CH), BF16),
                scratch_shapes=[pltpu.VMEM((length + 32, CONV_CH), F32),
                                pltpu.VMEM((crc + 32, CONV_CH), F32)],
                input_output_aliases=aliases,
                compiler_params=_cparams("parallel"),
                name="conv_branch" if prev is None else "conv_branch_ctx",
            )(*args)

        o_conv = conv_call(seq, 0, None)
        if not last:
            o_conv = conv_call(n_ctx, ctx_blk0, o_conv)

        y = pl.pallas_call(
            _merge_kernel,
            grid=(m_rows // ts, 1),
            in_specs=[
                pl.BlockSpec((ts, CONV_CH), lambda i, j: (i, 0)),
                pl.BlockSpec((ts, MLA_HEADS * MLA_V), lambda i, j: (i, 0)),
                pl.BlockSpec((ts, GQA_HEADS * GQA_HEAD_DIM), lambda i, j: (i, 0)),
                pl.BlockSpec((None, CONV_CH, d), lambda i, j: (l, 0, 0)),
                pl.BlockSpec((None, MLA_HEADS * MLA_V, d), lambda i, j: (l, 0, 0)),
                pl.BlockSpec((None, GQA_HEADS * GQA_HEAD_DIM, d), lambda i, j: (l, 0, 0)),
                pl.BlockSpec((ts, d), lambda i, j: (i, 0)),
                pl.BlockSpec((ts, d), lambda i, j: (i, 1)),
                pl.BlockSpec((ts, d), lambda i, j: (i, 2)),
            ],
            out_specs=pl.BlockSpec((ts, d), lambda i, j: (i, 0)),
            out_shape=jax.ShapeDtypeStruct((m_all, d), BF16),
            compiler_params=_cparams("parallel", "parallel"),
            name="merge",
        )(o_conv, o_mla, o_gqa, w_conv_out_b, w_mla_out_b, w_gqa_out_b, p, p, p)

        def mm_res(a, w, xin, part, tile, width, name):
            kdim = a.shape[1]
            return pl.pallas_call(
                _mm_res_kernel,
                grid=(m_rows // tile, d // width),
                in_specs=[
                    pl.BlockSpec((tile, kdim), lambda i, j: (i, 0)),
                    pl.BlockSpec((None, kdim, width), lambda i, j: (l, 0, j)),
                    pl.BlockSpec((tile, width), lambda i, j: (i, j)),
                    mod_spec(l, part, tile, width, True),
                ],
                out_specs=pl.BlockSpec((tile, width), lambda i, j: (i, j)),
                out_shape=jax.ShapeDtypeStruct((m_all, d), F32),
                compiler_params=_cparams("parallel", "parallel"),
                name=name,
            )(a, w, xin, mod)

        xa = mm_res(y, w_out_b, xa, 2, ts, d, "out_proj")

        nf = d_ff // tn
        act = pl.pallas_call(
            functools.partial(_ffn_in_kernel, rc=rc),
            grid=(m_rows // tm, nf),
            in_specs=[
                pl.BlockSpec((tm, d), lambda i, j: (i, 0)),
                pl.BlockSpec((None, 1, d), lambda i, j: (l, 0, 0)),
                mod_spec(l, 3, tm, d, False),
                mod_spec(l, 4, tm, d, False),
                pl.BlockSpec((None, d, tn), lambda i, j: (l, 0, j)),
                pl.BlockSpec((None, d, tn), lambda i, j: (l, 0, nf + j)),
            ],
            out_specs=pl.BlockSpec((tm, tn), lambda i, j: (i, j)),
            out_shape=jax.ShapeDtypeStruct((m_all, d_ff), BF16),
            scratch_shapes=[pltpu.VMEM((tm, d), BF16)],
            compiler_params=_cparams("parallel", "arbitrary"),
            name="ffn_in",
        )(xa, g_ffn3, mod, mod, w_ffn_in_b, w_ffn_in_b)
        xa = mm_res(act, w_ffn_out_b, xa, 5, tm, tn, "ffn_out")

    out = pl.pallas_call(
        functools.partial(_final_norm_kernel, rc=rc),
        grid=(m_lat // tm,),
        in_specs=[pl.BlockSpec((tm, d), lambda i: (i, 0)),
                  pl.BlockSpec((1, d), lambda i: (0, 0))],
        out_specs=pl.BlockSpec((tm, d), lambda i: (i, 0)),
        out_shape=jax.ShapeDtypeStruct((m_lat, d), F32),
        compiler_params=_cparams("parallel"),
        name="final_norm",
    )(xa, g_final.reshape(1, d))
    return out.reshape(bsz, seq, d)
```

```python
import functools

import jax
import jax.numpy as jnp
from jax import lax
from jax.experimental import pallas as pl
from jax.experimental.pallas import tpu as pltpu

F32 = jnp.float32
BF16 = jnp.bfloat16

GRID_W = 64
CONV_CH = 512
CONV_WIDTH = 31
MLA_HEADS = 8
MLA_Q_RANK = 512
MLA_KV_RANK = 256
MLA_NOPE = 128
MLA_ROPE = 64
MLA_V = 128
GQA_HEADS = 8
GQA_KV_HEADS = 2
GQA_HEAD_DIM = 64
WINDOW = 128
N_BRANCH = 3
ROPE_DIM = 64
ROPE_BASE = 10000.0
EPS = 1e-6
NEG_INF = -1e30

COL_MLA_KV = 0
COL_MLA_KR = COL_MLA_KV + MLA_KV_RANK
COL_GQA_K = COL_MLA_KR + MLA_ROPE
COL_GQA_V = COL_GQA_K + GQA_KV_HEADS * GQA_HEAD_DIM
KV_COLS = COL_GQA_V + GQA_KV_HEADS * GQA_HEAD_DIM
COL_MLA_Q = KV_COLS
COL_GQA_Q = COL_MLA_Q + MLA_Q_RANK
COL_CONV = COL_GQA_Q + GQA_HEADS * GQA_HEAD_DIM
COL_GATE = COL_CONV + 2 * CONV_CH

LANES = 128
MLA_PAD = 256
VMEM_LIMIT = 56 * 1024 * 1024
MLA_TQ = 1024
MLA_RG = 64
MLA_TK = 2048
MLA_UNROLL = 2
LOG2_E = 1.4426950408889634

P_MLA_Q = 0
P_GQA_Q = 512
P_CONV_A = 1024
P_CONV_B = 1536
P_CKV = 2048
P_KR = 2304
P_GQA_K = 2432
P_GQA_V = 2560
P_SMALL = 3072


def _cparams(*sem):
    return pltpu.CompilerParams(dimension_semantics=sem, vmem_limit_bytes=VMEM_LIMIT)


def _for_chunks(n_rows, rc, fn):
    n = n_rows // rc
    if n == 1:
        fn(0)
        return

    def body(i, carry):
        fn(pl.multiple_of(i * rc, rc))
        return carry

    lax.fori_loop(0, n, body, 0)


def _sigmoid(x):
    return 0.5 * jnp.tanh(0.5 * x) + 0.5


def _rope(x, cos, sin_up, sin_dn):
    return x * cos + pltpu.roll(x, LANES - 16, 1) * sin_up + pltpu.roll(x, 16, 1) * sin_dn


def _dot(a, b):
    return jnp.dot(a, b, preferred_element_type=F32)


def _dot_nt(a, b):
    return lax.dot_general(a, b, (((1,), (1,)), ((), ())), preferred_element_type=F32)


def _ada_kernel(c_ref, w_ref, b_ref, o_ref):
    c = c_ref[...]
    s = c * _sigmoid(c)
    o_ref[...] = jnp.dot(s, w_ref[...], preferred_element_type=F32,
                         precision=lax.Precision.HIGHEST) + b_ref[...]


def _ada(cc, w_ada, b_ada):
    depth, d, n = w_ada.shape
    rows = cc.shape[0]
    tn = 1024
    return pl.pallas_call(
        _ada_kernel,
        grid=(depth, n // tn),
        in_specs=[
            pl.BlockSpec((rows, d), lambda l, j: (0, 0)),
            pl.BlockSpec((None, d, tn), lambda l, j: (l, 0, j)),
            pl.BlockSpec((None, 1, tn), lambda l, j: (l, 0, j)),
        ],
        out_specs=pl.BlockSpec((None, rows, tn), lambda l, j: (l, 0, j)),
        out_shape=jax.ShapeDtypeStruct((depth, rows, n), F32),
        compiler_params=_cparams("parallel", "parallel"),
        name="ada_mod",
    )(cc, w_ada, b_ada.reshape(depth, 1, n))


def _norm_mod_store(x_ref, g_ref, sh_ref, sc_ref, h_ref, rc):
    gs = g_ref[...] * (1.0 + sc_ref[...])
    sh = sh_ref[...]

    def chunk(r0):
        x = x_ref[pl.ds(r0, rc), :]
        ms = jnp.mean(x * x, axis=-1, keepdims=True)
        h_ref[pl.ds(r0, rc), :] = (x * lax.rsqrt(ms + EPS) * gs + sh).astype(BF16)

    _for_chunks(x_ref.shape[0], rc, chunk)


def _in_proj_kernel(x_ref, g_ref, sh_ref, sc_ref, w_ref, *rest, n_gate, col0, rc):
    o_ref, h_ref = rest[-2:]
    j = pl.program_id(1)

    @pl.when(j == 0)
    def _():
        _norm_mod_store(x_ref, g_ref, sh_ref, sc_ref, h_ref, rc)

    acc = _dot(h_ref[...], w_ref[...])
    o_ref[...] = jnp.where(col0 + j < n_gate, _sigmoid(acc), acc).astype(BF16)


def _ffn_in_kernel(x_ref, g_ref, sh_ref, sc_ref, w1_ref, w2_ref, o_ref, h_ref, *, rc):
    @pl.when(pl.program_id(1) == 0)
    def _():
        _norm_mod_store(x_ref, g_ref, sh_ref, sc_ref, h_ref, rc)

    h = h_ref[...]
    u1 = _dot(h, w1_ref[...])
    u2 = _dot(h, w2_ref[...])
    o_ref[...] = (u1 * _sigmoid(u1) * u2).astype(BF16)


def _mm_res_kernel(a_ref, w_ref, x_ref, gt_ref, o_ref):
    o_ref[...] = x_ref[...] + gt_ref[...] * _dot(a_ref[...], w_ref[...])


def _merge_kernel(oc_ref, om_ref, og_ref, wc_ref, wm_ref, wg_ref, g0_ref, g1_ref, g2_ref, y_ref):
    y = g0_ref[...].astype(F32) * _dot(oc_ref[...], wc_ref[...])
    y += g1_ref[...].astype(F32) * _dot(om_ref[...], wm_ref[...])
    y += g2_ref[...].astype(F32) * _dot(og_ref[...], wg_ref[...])
    y_ref[...] = y.astype(BF16)


def _rms_bf16(a, g):
    ms = jnp.mean(a * a, axis=-1, keepdims=True)
    return (a * lax.rsqrt(ms + EPS) * g).astype(BF16)


def _qup_kernel(pq_ref, g_ref, wq_ref, cos_ref, su_ref, sd_ref, q_ref, qg_ref, *, rc, s_mla, s_gqa):
    g = g_ref[...]

    def chunk(r0):
        rows = pl.ds(r0, rc)
        n = _rms_bf16(pq_ref[rows, 0:MLA_Q_RANK].astype(F32), g)
        q = _dot(n, wq_ref[...])
        cos = cos_ref[rows, :]
        su = su_ref[rows, :]
        sd = sd_ref[rows, :]
        for h in range(MLA_HEADS):
            c0 = h * MLA_PAD
            q_ref[rows, c0:c0 + LANES] = (q[:, c0:c0 + LANES] * s_mla).astype(BF16)
            r = _rope(q[:, c0 + LANES:c0 + 2 * LANES], cos, su, sd)
            q_ref[rows, c0 + LANES:c0 + 2 * LANES] = (r * s_mla).astype(BF16)
        for t in range(GQA_HEADS * GQA_HEAD_DIM // LANES):
            gq = pq_ref[rows, MLA_Q_RANK + t * LANES:MLA_Q_RANK + (t + 1) * LANES].astype(F32)
            qg_ref[rows, t * LANES:(t + 1) * LANES] = (_rope(gq, cos, su, sd) * s_gqa).astype(BF16)

    _for_chunks(pq_ref.shape[0], rc, chunk)


def _kvup_kernel(ckv_ref, kr_ref, g_ref, wk_ref, wv_ref, cos_ref, su_ref, sd_ref,
                 k_ref, v_ref, kg_ref, vg_ref, *, rc):
    g = g_ref[...]

    def chunk(r0):
        rows = pl.ds(r0, rc)
        n = _rms_bf16(ckv_ref[rows, :].astype(F32), g)
        kn = _dot(n, wk_ref[...])
        vv = _dot(n, wv_ref[...])
        cos = cos_ref[rows, :]
        su = su_ref[rows, :]
        sd = sd_ref[rows, :]
        kr = _rope(kr_ref[rows, 0:LANES].astype(F32), cos, su, sd).astype(BF16)
        ones_tile = jnp.ones((rc, LANES), BF16)
        for h in range(MLA_HEADS):
            c0 = h * MLA_PAD
            k_ref[rows, c0:c0 + LANES] = kn[:, h * LANES:(h + 1) * LANES].astype(BF16)
            k_ref[rows, c0 + LANES:c0 + 2 * LANES] = kr
            v_ref[rows, c0:c0 + LANES] = vv[:, h * LANES:(h + 1) * LANES].astype(BF16)
            v_ref[rows, c0 + LANES:c0 + 2 * LANES] = ones_tile
        low_half = lax.broadcasted_iota(jnp.int32, (rc, LANES), 1) < GQA_HEAD_DIM
        gk = _rope(kr_ref[rows, LANES:2 * LANES].astype(F32), cos, su, sd)
        gk_sw = pltpu.roll(gk, GQA_HEAD_DIM, 1)
        kg_ref[rows, 0:LANES] = jnp.where(low_half, gk, gk_sw).astype(BF16)
        kg_ref[rows, LANES:2 * LANES] = jnp.where(low_half, gk_sw, gk).astype(BF16)
        gv = kr_ref[rows, 2 * LANES:3 * LANES].astype(F32)
        gv_sw = pltpu.roll(gv, GQA_HEAD_DIM, 1)
        vg_ref[rows, 0:LANES] = jnp.where(low_half, gv, gv_sw).astype(BF16)
        vg_ref[rows, LANES:2 * LANES] = ones_tile
        vg_ref[rows, 2 * LANES:3 * LANES] = jnp.where(low_half, gv_sw, gv).astype(BF16)
        vg_ref[rows, 3 * LANES:4 * LANES] = ones_tile

    _for_chunks(ckv_ref.shape[0], rc, chunk)


def _mla_attn_kernel(q_ref, k_ref, v_ref, *rest, tk, rg, unroll):
    n_buf = 3 * unroll
    o_ref = rest[-n_buf - 3]
    bufs = rest[-n_buf - 2:-2]
    s_buf, p_buf, a_buf = bufs[:unroll], bufs[unroll:2 * unroll], bufs[2 * unroll:]
    m_ref, acc_ref = rest[-2:]
    tq = q_ref.shape[0]
    head = k_ref.shape[0] % tk
    has_head = 1 if head else 0
    n = k_ref.shape[0] // tk + has_head

    def start_of(c):
        return head + (c - has_head) * tk

    def size_of(c):
        return head if (has_head and c == 0) else tk

    def qk(slot, start, size):
        s_buf[slot][:, 0:size] = _dot_nt(q_ref[...], k_ref[pl.ds(start, size), :])

    def sm(slot, size):
        for r in range(0, tq, rg):
            s = s_buf[slot][r:r + rg, 0:size]
            m_old = m_ref[r:r + rg, :]
            m_new = jnp.maximum(m_old, jnp.max(s, axis=-1, keepdims=True))
            p_buf[slot][r:r + rg, 0:size] = jnp.exp2(s - pltpu.repeat(m_new, size // LANES, 1)).astype(BF16)
            a_buf[slot][r:r + rg, :] = jnp.exp2(m_old - m_new)
            m_ref[r:r + rg, :] = m_new

    def pv(slot, start, size):
        alpha = pltpu.repeat(a_buf[slot][...], acc_ref.shape[1] // LANES, 1)
        acc_ref[...] = alpha * acc_ref[...] + _dot(p_buf[slot][:, 0:size], v_ref[pl.ds(start, size), :])

    def static_start(c):
        return 0 if (has_head and c == 0) else start_of(c)

    def static_tick(t):
        if 0 <= t < n:
            qk(t % unroll, static_start(t), size_of(t))
        if 0 <= t - 2 < n:
            pv((t - 2) % unroll, static_start(t - 2), size_of(t - 2))
        if 0 <= t - 1 < n:
            sm((t - 1) % unroll, size_of(t - 1))

    m_ref[...] = jnp.full(m_ref.shape, NEG_INF, F32)
    acc_ref[...] = jnp.zeros(acc_ref.shape, F32)

    first_steady = 2 + has_head
    trips = max(n - first_steady, 0) // unroll
    for t in range(first_steady):
        static_tick(t)
    if trips:
        def trip(j, carry):
            for u in range(unroll):
                t = first_steady + unroll * j + u
                qk((first_steady + u) % unroll, pl.multiple_of(start_of(t), 2 * LANES), tk)
                pv((first_steady + u - 2) % unroll, pl.multiple_of(start_of(t - 2), 2 * LANES), tk)
                sm((first_steady + u - 1) % unroll, tk)
            return carry

        lax.fori_loop(0, trips, trip, 0)
    for t in range(first_steady + unroll * trips, n + 2):
        static_tick(t)
    acc = acc_ref[...]
    o_ref[...] = (acc[:, :MLA_V] / acc[:, MLA_V:MLA_V + 1]).astype(BF16)


def _gqa_attn_kernel(q_ref, kc_ref, vc_ref, sink_ref, *rest, seq, n_sub, latent):
    if latent:
        kl_ref, vl_ref = rest[0], rest[1]
    o_ref, s0, s1, p0, p1, e0, e1 = rest[-7:]
    s_buf, p_buf, e_buf = (s0, s1), (p0, p1), (e0, e1)
    span = 3 * WINDOW
    n_c = kc_ref.shape[0]
    rep = GQA_HEADS // GQA_KV_HEADS
    lane = lax.broadcasted_iota(jnp.int32, (1, LANES), 1)
    low_half = lane < GQA_HEAD_DIM
    half_masks = (jnp.where(low_half, 1.0, 0.0).astype(BF16), jnp.where(low_half, 0.0, 1.0).astype(BF16))
    sinks = sink_ref[...] * LOG2_E
    sinks = [sinks[:, h:h + 1] for h in range(GQA_HEADS)]

    def row0(r):
        return r * WINDOW if isinstance(r, int) else pl.multiple_of(r * WINDOW, WINDOW)

    def win_start(r):
        if isinstance(r, int):
            return min(max(r * WINDOW - WINDOW, 0), seq - span)
        return pl.multiple_of(jnp.clip(r * WINDOW - WINDOW, 0, seq - span), WINDOW)

    def qk(slot, r):
        rows = pl.ds(row0(r), WINDOW)
        for h in range(GQA_HEADS):
            g = h // rep
            q = q_ref[rows, (h // 2) * LANES:(h // 2 + 1) * LANES] * half_masks[h % 2]
            s_buf[slot][h, :, 0:n_c] = _dot_nt(q, kc_ref[:, g * LANES:(g + 1) * LANES])
            if latent:
                s_buf[slot][h, :, n_c:n_c + span] = _dot_nt(
                    q, kl_ref[pl.ds(win_start(r), span), g * LANES:(g + 1) * LANES])

    def sm(slot, r):
        if latent:
            qpos = r * WINDOW + lax.broadcasted_iota(jnp.int32, (WINDOW, span), 0)
            kpos = win_start(r) + lax.broadcasted_iota(jnp.int32, (WINDOW, span), 1)
            bias = jnp.where(jnp.abs(qpos - kpos) <= WINDOW, 0.0, NEG_INF)
        for h in range(GQA_HEADS):
            sc = s_buf[slot][h, :, 0:n_c]
            m = jnp.maximum(jnp.max(sc, axis=-1, keepdims=True), sinks[h])
            if latent:
                sw = s_buf[slot][h, :, n_c:n_c + span] + bias
                m = jnp.maximum(m, jnp.max(sw, axis=-1, keepdims=True))
                p_buf[slot][h, :, n_c:n_c + span] = jnp.exp2(sw - m).astype(BF16)
            p_buf[slot][h, :, 0:n_c] = jnp.exp2(sc - m).astype(BF16)
            e_buf[slot][h] = jnp.broadcast_to(jnp.exp2(sinks[h] - m), (WINDOW, LANES))

    def pv(slot, r):
        outs = []
        for h in range(GQA_HEADS):
            g = h // rep
            o = _dot(p_buf[slot][h, :, 0:n_c], vc_ref[:, g * 2 * LANES:(g + 1) * 2 * LANES])
            if latent:
                o = o + _dot(p_buf[slot][h, :, n_c:n_c + span],
                             vl_ref[pl.ds(win_start(r), span), g * 2 * LANES:(g + 1) * 2 * LANES])
            outs.append(o[:, :LANES] / (o[:, LANES:] + e_buf[slot][h]))
        rows = pl.ds(row0(r), WINDOW)
        for t in range(GQA_HEADS // 2):
            o_ref[rows, t * LANES:(t + 1) * LANES] = jnp.where(low_half, outs[2 * t], outs[2 * t + 1]).astype(BF16)

    def tick(t, par):
        qk(par, t)
        pv(par, t - 2)
        sm(1 - par, t - 1)

    def static_tick(t):
        if 0 <= t < n_sub:
            qk(t % 2, t)
        if 0 <= t - 2 < n_sub:
            pv(t % 2, t - 2)
        if 0 <= t - 1 < n_sub:
            sm((t - 1) % 2, t - 1)

    trips = max(n_sub - 2, 0) // 2
    for t in range(2):
        static_tick(t)
    if trips:
        def trip(j, carry):
            for u in range(2):
                tick(2 + 2 * j + u, u)
            return carry

        lax.fori_loop(0, trips, trip, 0)
    for t in range(2 + 2 * trips, n_sub + 2):
        static_tick(t)


def _conv_kernel(a_ref, b_ref, w_ref, cb_ref, lg_ref, lb_ref, o_ref, u_ref, *, seq, rc):
    halo = 16
    u_ref[0:halo, :] = jnp.zeros((halo, CONV_CH), F32)
    u_ref[halo + seq:2 * halo + seq, :] = jnp.zeros((halo, CONV_CH), F32)

    def fill(r0):
        a = a_ref[pl.ds(r0, 256), :].astype(F32)
        b = b_ref[pl.ds(r0, 256), :].astype(F32)
        u_ref[pl.ds(halo + r0, 256), :] = a * _sigmoid(b)

    _for_chunks(seq, 256, fill)

    cb = cb_ref[...]
    lg = lg_ref[...]
    lb = lb_ref[...]

    half = CONV_CH // 2
    first_tap = halo - CONV_WIDTH // 2

    def conv(r0):
        parts = []
        for c0 in (0, half):
            acc = jnp.zeros((rc, half), F32) + cb[:, c0:c0 + half]
            for s in range(8):
                part = None
                for a in range(4):
                    k = 8 * a + s - first_tap
                    if 0 <= k < CONV_WIDTH:
                        rows = u_ref[pl.ds(pl.multiple_of(r0 + 8 * a, 8), rc + 8), c0:c0 + half]
                        term = w_ref[k:k + 1, c0:c0 + half] * rows
                        part = term if part is None else part + term
                acc = acc + part[s:s + rc]
            parts.append(acc)
        acc = jnp.concatenate(parts, axis=-1)
        mu = jnp.mean(acc, axis=-1, keepdims=True)
        xc = acc - mu
        var = jnp.mean(xc * xc, axis=-1, keepdims=True)
        y = xc * lax.rsqrt(var + EPS) * lg + lb
        o_ref[pl.ds(r0, rc), :] = (y * _sigmoid(y)).astype(BF16)

    _for_chunks(seq, rc, conv)


def _final_norm_kernel(x_ref, g_ref, o_ref, *, rc):
    g = g_ref[...]

    def chunk(r0):
        x = x_ref[pl.ds(r0, rc), :]
        ms = jnp.mean(x * x, axis=-1, keepdims=True)
        o_ref[pl.ds(r0, rc), :] = x * lax.rsqrt(ms + EPS) * g

    _for_chunks(x_ref.shape[0], rc, chunk)


def _rope_tables(length, extra):
    rows = length // GRID_W
    row = jnp.repeat(jnp.arange(rows), GRID_W).astype(F32)
    col = jnp.tile(jnp.arange(GRID_W), rows).astype(F32)
    n_freq = ROPE_DIM // 4
    inv_freq = ROPE_BASE ** (-jnp.arange(n_freq, dtype=F32) / n_freq)
    a_row = row[:, None] * inv_freq[None, :]
    a_col = col[:, None] * inv_freq[None, :]
    ang = jnp.concatenate([a_row, a_row, a_col, a_col], axis=-1)
    cos = jnp.concatenate([jnp.cos(ang), jnp.ones((extra, ROPE_DIM), F32)], axis=0)
    sin = jnp.concatenate([jnp.sin(ang), jnp.zeros((extra, ROPE_DIM), F32)], axis=0)
    cos = jnp.tile(cos, (1, LANES // ROPE_DIM))
    sin = jnp.tile(sin, (1, LANES // ROPE_DIM))
    first = (jnp.arange(LANES) // 16) % 2 == 0
    sin_up = jnp.where(first[None, :], -sin, 0.0)
    sin_dn = jnp.where(first[None, :], 0.0, sin)
    return cos, sin_up, sin_dn


def _pick_tile(*sizes):
    for t in (1024, 512, 256):
        if all(s % t == 0 for s in sizes):
            return t
    raise ValueError(f"unsupported row counts {sizes}")


def kernel(x, c, ctx, c_ctx, w_ada, b_ada, g_mix, w_in, conv_w, conv_b, conv_ln_g, conv_ln_b, w_conv_out, g_q_a, w_q_up, g_kv_a, w_kv_up, w_mla_out, gqa_sink, w_gqa_out, w_out, g_ffn, w_ffn_in, w_ffn_out, g_final):
    bsz, seq, d = x.shape
    n_ctx = ctx.shape[1]
    depth = w_ada.shape[0]
    d_ff = w_ffn_out.shape[1]
    m_lat = bsz * seq
    m_ctx = bsz * n_ctx
    m_all = m_lat + m_ctx
    assert seq % GRID_W == 0 and n_ctx == 2 * WINDOW and seq % (4 * WINDOW) == 0
    assert bsz + 1 <= 16 and w_in.shape[2] == COL_GATE + N_BRANCH * d

    tm = _pick_tile(seq, m_ctx)
    ts = min(tm, 512)
    tn = 512
    rc = 256
    n_lat_tiles = m_lat // tm
    tiles_per_batch = seq // tm
    p_base = N_BRANCH * d
    p_cols = p_base + P_SMALL
    tn_in = 1024
    assert (p_base + P_CKV) % tn_in == 0 and P_SMALL - P_CKV == tn_in
    ctx_blk0 = m_lat // n_ctx

    def mod_row(i, tile):
        return jnp.minimum(i // (seq // tile), bsz)

    def cols(a, n):
        return w_in[:, :, a:a + n].astype(BF16)

    def zeros(n):
        return jnp.zeros((depth, d, n), BF16)

    w_in_r = jnp.concatenate([
        cols(COL_GATE, N_BRANCH * d),
        cols(COL_MLA_Q, MLA_Q_RANK), cols(COL_GQA_Q, GQA_HEADS * GQA_HEAD_DIM),
        cols(COL_CONV, CONV_CH), cols(COL_CONV + CONV_CH, CONV_CH),
        cols(COL_MLA_KV, MLA_KV_RANK), cols(COL_MLA_KR, MLA_ROPE), zeros(LANES - MLA_ROPE),
        cols(COL_GQA_K, GQA_KV_HEADS * GQA_HEAD_DIM), cols(COL_GQA_V, GQA_KV_HEADS * GQA_HEAD_DIM),
        zeros(P_SMALL - P_GQA_V - GQA_KV_HEADS * GQA_HEAD_DIM)], axis=-1)
    wq = w_q_up.reshape(depth, MLA_Q_RANK, MLA_HEADS, MLA_NOPE + MLA_ROPE).astype(BF16)
    wq = jnp.pad(wq, ((0, 0), (0, 0), (0, 0), (0, MLA_PAD - MLA_NOPE - MLA_ROPE)))
    wq = wq.reshape(depth, MLA_Q_RANK, MLA_HEADS * MLA_PAD)
    wkv = w_kv_up.reshape(depth, MLA_KV_RANK, MLA_HEADS, MLA_NOPE + MLA_V).astype(BF16)
    wk = wkv[..., :MLA_NOPE].reshape(depth, MLA_KV_RANK, MLA_HEADS * MLA_NOPE)
    wv = wkv[..., MLA_NOPE:].reshape(depth, MLA_KV_RANK, MLA_HEADS * MLA_V)
    w_conv_out_b = w_conv_out.astype(BF16)
    w_mla_out_b = w_mla_out.astype(BF16)
    w_gqa_out_b = w_gqa_out.astype(BF16)
    w_out_b = w_out.astype(BF16)
    w_ffn_in_b = w_ffn_in.astype(BF16)
    w_ffn_out_b = w_ffn_out.astype(BF16)
    cos, sin_up, sin_dn = _rope_tables(seq, ts)

    cc = jnp.zeros((16, d), F32).at[:bsz].set(c).at[bsz].set(c_ctx)
    mod = _ada(cc, w_ada, b_ada).reshape(depth, 16, 1, 6 * d)

    xa = jnp.concatenate([x.reshape(m_lat, d), ctx.reshape(m_ctx, d)], axis=0)

    def vec(a):
        return a.reshape(depth, 1, a.shape[-1])

    g_mix3, g_ffn3, g_q3, g_kv3 = vec(g_mix), vec(g_ffn), vec(g_q_a), vec(g_kv_a)
    conv_b3, ln_g3, ln_b3, sink3 = vec(conv_b), vec(conv_ln_g), vec(conv_ln_b), vec(gqa_sink)

    def mod_spec(l, part, tile, width, with_j, tile0=0):
        nb = d // width
        if with_j:
            return pl.BlockSpec((None, None, 1, width),
                                lambda i, j: (l, mod_row(tile0 + i, tile), 0, part * nb + j))
        return pl.BlockSpec((None, None, 1, width), lambda i, j: (l, mod_row(tile0 + i, tile), 0, part))

    def table_specs(tile):
        per_batch = seq // tile
        n_lat = m_lat // tile
        return [pl.BlockSpec((tile, LANES), lambda i: (jnp.where(i < n_lat, i % per_batch, per_batch), 0))] * 3

    for l in range(depth):
        last = l == depth - 1
        m_rows = m_lat if last else m_all

        def in_proj_call(row_tile0, n_row_tiles, col_tile0, n_col_tiles, prev):
            in_specs = [
                pl.BlockSpec((tm, d), lambda i, j: (row_tile0 + i, 0)),
                pl.BlockSpec((None, 1, d), lambda i, j: (l, 0, 0)),
                mod_spec(l, 0, tm, d, False, row_tile0),
                mod_spec(l, 1, tm, d, False, row_tile0),
                pl.BlockSpec((None, d, tn_in), lambda i, j: (l, 0, col_tile0 + j)),
            ]
            args = [xa, g_mix3, mod, mod, w_in_r]
            aliases = {}
            if prev is not None:
                in_specs.append(pl.BlockSpec(memory_space=pl.ANY))
                args.append(prev)
                aliases = {5: 0}
            return pl.pallas_call(
                functools.partial(_in_proj_kernel, n_gate=p_base // tn_in, col0=col_tile0, rc=rc),
                grid=(n_row_tiles, n_col_tiles),
                in_specs=in_specs,
                out_specs=pl.BlockSpec((tm, tn_in), lambda i, j: (row_tile0 + i, col_tile0 + j)),
                out_shape=jax.ShapeDtypeStruct((m_all, p_cols), BF16),
                scratch_shapes=[pltpu.VMEM((tm, d), BF16)],
                input_output_aliases=aliases,
                compiler_params=_cparams("parallel", "arbitrary"),
                name="in_proj" if prev is None else "in_proj_ctx_kv",
            )(*args)

        if last:
            p = in_proj_call(0, n_lat_tiles, 0, p_cols // tn_in, None)
            p = in_proj_call(n_lat_tiles, m_ctx // tm, (p_base + P_CKV) // tn_in, 1, p)
        else:
            p = in_proj_call(0, m_all // tm, 0, p_cols // tn_in, None)

        q_mla, q_gqa = pl.pallas_call(
            functools.partial(_qup_kernel, rc=rc, s_mla=float(LOG2_E * (MLA_NOPE + MLA_ROPE) ** -0.5),
                              s_gqa=float(LOG2_E * GQA_HEAD_DIM ** -0.5)),
            grid=(m_rows // ts,),
            in_specs=[
                pl.BlockSpec((ts, 1024), lambda i: (i, (p_base + P_MLA_Q) // 1024)),
                pl.BlockSpec((None, 1, MLA_Q_RANK), lambda i: (l, 0, 0)),
                pl.BlockSpec((None, MLA_Q_RANK, MLA_HEADS * MLA_PAD), lambda i: (l, 0, 0)),
            ] + table_specs(ts),
            out_specs=[pl.BlockSpec((ts, MLA_HEADS * MLA_PAD), lambda i: (i, 0)),
                       pl.BlockSpec((ts, GQA_HEADS * GQA_HEAD_DIM), lambda i: (i, 0))],
            out_shape=[jax.ShapeDtypeStruct((m_all, MLA_HEADS * MLA_PAD), BF16),
                       jax.ShapeDtypeStruct((m_all, GQA_HEADS * GQA_HEAD_DIM), BF16)],
            compiler_params=_cparams("parallel"),
            name="q_up",
        )(p, g_q3, wq, cos, sin_up, sin_dn)

        tkv = n_ctx
        kv_per_batch = seq // tkv
        kv_lat_tiles = m_lat // tkv

        def kv_block(i):
            lat = (i // kv_per_batch) * (kv_per_batch + 1) + 1 + i % kv_per_batch
            return jnp.where(i < kv_lat_tiles, lat, (i - kv_lat_tiles) * (kv_per_batch + 1))

        k_mla, v_mla, k_gqa, v_gqa = pl.pallas_call(
            functools.partial(_kvup_kernel, rc=tkv),
            grid=(m_all // tkv,),
            in_specs=[
                pl.BlockSpec((tkv, MLA_KV_RANK), lambda i: (i, (p_base + P_CKV) // MLA_KV_RANK)),
                pl.BlockSpec((tkv, 3 * LANES), lambda i: (i, (p_base + P_KR) // (3 * LANES))),
                pl.BlockSpec((None, 1, MLA_KV_RANK), lambda i: (l, 0, 0)),
                pl.BlockSpec((None, MLA_KV_RANK, MLA_HEADS * MLA_NOPE), lambda i: (l, 0, 0)),
                pl.BlockSpec((None, MLA_KV_RANK, MLA_HEADS * MLA_V), lambda i: (l, 0, 0)),
            ] + table_specs(tkv),
            out_specs=[pl.BlockSpec((tkv, MLA_HEADS * MLA_PAD), lambda i: (kv_block(i), 0)),
                       pl.BlockSpec((tkv, MLA_HEADS * MLA_PAD), lambda i: (kv_block(i), 0)),
                       pl.BlockSpec((tkv, 2 * LANES), lambda i: (i, 0)),
                       pl.BlockSpec((tkv, 4 * LANES), lambda i: (i, 0))],
            out_shape=[jax.ShapeDtypeStruct((m_all, MLA_HEADS * MLA_PAD), BF16),
                       jax.ShapeDtypeStruct((m_all, MLA_HEADS * MLA_PAD), BF16),
                       jax.ShapeDtypeStruct((m_all, 2 * LANES), BF16),
                       jax.ShapeDtypeStruct((m_all, 4 * LANES), BF16)],
            compiler_params=_cparams("parallel"),
            name="kv_up",
        )(p, p, g_kv3, wk, wv, cos, sin_up, sin_dn)

        tq = min(MLA_TQ, seq)
        nq = seq // tq
        def mla_scratch(rows, tk):
            return ([pltpu.VMEM((rows, tk), F32)] * MLA_UNROLL + [pltpu.VMEM((rows, tk), BF16)] * MLA_UNROLL
                    + [pltpu.VMEM((rows, LANES), F32)] * (MLA_UNROLL + 1) + [pltpu.VMEM((rows, MLA_PAD), F32)])

        o_mla = pl.pallas_call(
            functools.partial(_mla_attn_kernel, tk=MLA_TK, rg=MLA_RG, unroll=MLA_UNROLL),
            grid=(bsz, MLA_HEADS, nq),
            scratch_shapes=mla_scratch(tq, MLA_TK),
            in_specs=[
                pl.BlockSpec((tq, MLA_PAD), lambda b, h, i: (b * nq + i, h)),
                pl.BlockSpec((seq + n_ctx, MLA_PAD), lambda b, h, i: (b, h)),
                pl.BlockSpec((seq + n_ctx, MLA_PAD), lambda b, h, i: (b, h)),
            ],
            out_specs=pl.BlockSpec((tq, MLA_V), lambda b, h, i: (b * nq + i, h)),
            out_shape=jax.ShapeDtypeStruct((m_all, MLA_HEADS * MLA_V), BF16),
            compiler_params=_cparams("parallel", "parallel", "arbitrary"),
            name="mla_attn",
        )(q_mla, k_mla, v_mla)
        o_mla = o_mla if last else pl.pallas_call(
            functools.partial(_mla_attn_kernel, tk=n_ctx, rg=MLA_RG, unroll=MLA_UNROLL),
            grid=(bsz, MLA_HEADS),
            scratch_shapes=mla_scratch(n_ctx, n_ctx),
            in_specs=[
                pl.BlockSpec((n_ctx, MLA_PAD), lambda b, h: (ctx_blk0 + b, h)),
                pl.BlockSpec((n_ctx, MLA_PAD), lambda b, h: (b * (kv_per_batch + 1), h)),
                pl.BlockSpec((n_ctx, MLA_PAD), lambda b, h: (b * (kv_per_batch + 1), h)),
                pl.BlockSpec(memory_space=pl.ANY),
            ],
            out_specs=pl.BlockSpec((n_ctx, MLA_V), lambda b, h: (ctx_blk0 + b, h)),
            out_shape=jax.ShapeDtypeStruct((m_all, MLA_HEADS * MLA_V), BF16),
            input_output_aliases={3: 0},
            compiler_params=_cparams("parallel", "parallel"),
            name="mla_attn_ctx",
        )(q_mla, k_mla, v_mla, o_mla)

        def gqa_scratch(cols):
            return ([pltpu.VMEM((GQA_HEADS, WINDOW, cols), F32)] * 2
                    + [pltpu.VMEM((GQA_HEADS, WINDOW, cols), BF16)] * 2
                    + [pltpu.VMEM((GQA_HEADS, WINDOW, LANES), F32)] * 2)

        o_gqa = pl.pallas_call(
            functools.partial(_gqa_attn_kernel, seq=seq, n_sub=seq // WINDOW, latent=True),
            grid=(bsz,),
            in_specs=[
                pl.BlockSpec((seq, GQA_HEADS * GQA_HEAD_DIM), lambda b: (b, 0)),
                pl.BlockSpec((n_ctx, 2 * LANES), lambda b: (ctx_blk0 + b, 0)),
                pl.BlockSpec((n_ctx, 4 * LANES), lambda b: (ctx_blk0 + b, 0)),
                pl.BlockSpec((None, 1, GQA_HEADS), lambda b: (l, 0, 0)),
                pl.BlockSpec((seq, 2 * LANES), lambda b: (b, 0)),
                pl.BlockSpec((seq, 4 * LANES), lambda b: (b, 0)),
            ],
            out_specs=pl.BlockSpec((seq, GQA_HEADS * GQA_HEAD_DIM), lambda b: (b, 0)),
            out_shape=jax.ShapeDtypeStruct((m_all, GQA_HEADS * GQA_HEAD_DIM), BF16),
            scratch_shapes=gqa_scratch(n_ctx + 3 * WINDOW),
            compiler_params=_cparams("parallel"),
            name="gqa_attn",
        )(q_gqa, k_gqa, v_gqa, sink3, k_gqa, v_gqa)
        o_gqa = o_gqa if last else pl.pallas_call(
            functools.partial(_gqa_attn_kernel, seq=n_ctx, n_sub=n_ctx // WINDOW, latent=False),
            grid=(bsz,),
            scratch_shapes=gqa_scratch(n_ctx),
            in_specs=[
                pl.BlockSpec((n_ctx, GQA_HEADS * GQA_HEAD_DIM), lambda b: (ctx_blk0 + b, 0)),
                pl.BlockSpec((n_ctx, 2 * LANES), lambda b: (ctx_blk0 + b, 0)),
                pl.BlockSpec((n_ctx, 4 * LANES), lambda b: (ctx_blk0 + b, 0)),
                pl.BlockSpec((None, 1, GQA_HEADS), lambda b: (l, 0, 0)),
                pl.BlockSpec(memory_space=pl.ANY),
            ],
            out_specs=pl.BlockSpec((n_ctx, GQA_HEADS * GQA_HEAD_DIM), lambda b: (ctx_blk0 + b, 0)),
            out_shape=jax.ShapeDtypeStruct((m_all, GQA_HEADS * GQA_HEAD_DIM), BF16),
            input_output_aliases={4: 0},
            compiler_params=_cparams("parallel"),
            name="gqa_attn_ctx",
        )(q_gqa, k_gqa, v_gqa, sink3, o_gqa)

        def conv_call(length, blk0, prev):
            crc = 64
            in_specs = [
                pl.BlockSpec((length, CONV_CH), lambda b: (blk0 + b, (p_base + P_CONV_A) // CONV_CH)),
                pl.BlockSpec((length, CONV_CH), lambda b: (blk0 + b, (p_base + P_CONV_B) // CONV_CH)),
                pl.BlockSpec((None, CONV_WIDTH, CONV_CH), lambda b: (l, 0, 0)),
                pl.BlockSpec((None, 1, CONV_CH), lambda b: (l, 0, 0)),
                pl.BlockSpec((None, 1, CONV_CH), lambda b: (l, 0, 0)),
                pl.BlockSpec((None, 1, CONV_CH), lambda b: (l, 0, 0)),
            ]
            args = [p, p, conv_w, conv_b3, ln_g3, ln_b3]
            aliases = {}
            if prev is not None:
                in_specs.append(pl.BlockSpec(memory_space=pl.ANY))
                args.append(prev)
                aliases = {6: 0}

            def body(a_ref, b_ref, w_ref, cb_ref, lg_ref, lb_ref, *rest):
                o_ref, u_ref = rest[-2:]
                _conv_kernel(a_ref, b_ref, w_ref, cb_ref, lg_ref, lb_ref, o_ref, u_ref, seq=length, rc=crc)

            return pl.pallas_call(
                body,
                grid=(bsz,),
                in_specs=in_specs,
                out_specs=pl.BlockSpec((length, CONV_CH), lambda b: (blk0 + b, 0)),
                out_shape=jax.ShapeDtypeStruct((m_all, CONV_CH), BF16),
                scratch_shapes=[pltpu.VMEM((length + 32, CONV_CH), F32)],
                input_output_aliases=aliases,
                compiler_params=_cparams("parallel"),
                name="conv_branch" if prev is None else "conv_branch_ctx",
            )(*args)

        o_conv = conv_call(seq, 0, None)
        if not last:
            o_conv = conv_call(n_ctx, ctx_blk0, o_conv)

        y = pl.pallas_call(
            _merge_kernel,
            grid=(m_rows // ts, 1),
            in_specs=[
                pl.BlockSpec((ts, CONV_CH), lambda i, j: (i, 0)),
                pl.BlockSpec((ts, MLA_HEADS * MLA_V), lambda i, j: (i, 0)),
                pl.BlockSpec((ts, GQA_HEADS * GQA_HEAD_DIM), lambda i, j: (i, 0)),
                pl.BlockSpec((None, CONV_CH, d), lambda i, j: (l, 0, 0)),
                pl.BlockSpec((None, MLA_HEADS * MLA_V, d), lambda i, j: (l, 0, 0)),
                pl.BlockSpec((None, GQA_HEADS * GQA_HEAD_DIM, d), lambda i, j: (l, 0, 0)),
                pl.BlockSpec((ts, d), lambda i, j: (i, 0)),
                pl.BlockSpec((ts, d), lambda i, j: (i, 1)),
                pl.BlockSpec((ts, d), lambda i, j: (i, 2)),
            ],
            out_specs=pl.BlockSpec((ts, d), lambda i, j: (i, 0)),
            out_shape=jax.ShapeDtypeStruct((m_all, d), BF16),
            compiler_params=_cparams("parallel", "parallel"),
            name="merge",
        )(o_conv, o_mla, o_gqa, w_conv_out_b, w_mla_out_b, w_gqa_out_b, p, p, p)

        def mm_res(a, w, xin, part, tile, width, name):
            kdim = a.shape[1]
            return pl.pallas_call(
                _mm_res_kernel,
                grid=(m_rows // tile, d // width),
                in_specs=[
                    pl.BlockSpec((tile, kdim), lambda i, j: (i, 0)),
                    pl.BlockSpec((None, kdim, width), lambda i, j: (l, 0, j)),
                    pl.BlockSpec((tile, width), lambda i, j: (i, j)),
                    mod_spec(l, part, tile, width, True),
                ],
                out_specs=pl.BlockSpec((tile, width), lambda i, j: (i, j)),
                out_shape=jax.ShapeDtypeStruct((m_all, d), F32),
                compiler_params=_cparams("parallel", "parallel"),
                name=name,
            )(a, w, xin, mod)

        xa = mm_res(y, w_out_b, xa, 2, ts, d, "out_proj")

        nf = d_ff // tn
        act = pl.pallas_call(
            functools.partial(_ffn_in_kernel, rc=rc),
            grid=(m_rows // tm, nf),
            in_specs=[
                pl.BlockSpec((tm, d), lambda i, j: (i, 0)),
                pl.BlockSpec((None, 1, d), lambda i, j: (l, 0, 0)),
                mod_spec(l, 3, tm, d, False),
                mod_spec(l, 4, tm, d, False),
                pl.BlockSpec((None, d, tn), lambda i, j: (l, 0, j)),
                pl.BlockSpec((None, d, tn), lambda i, j: (l, 0, nf + j)),
            ],
            out_specs=pl.BlockSpec((tm, tn), lambda i, j: (i, j)),
            out_shape=jax.ShapeDtypeStruct((m_all, d_ff), BF16),
            scratch_shapes=[pltpu.VMEM((tm, d), BF16)],
            compiler_params=_cparams("parallel", "arbitrary"),
            name="ffn_in",
        )(xa, g_ffn3, mod, mod, w_ffn_in_b, w_ffn_in_b)
        xa = mm_res(act, w_ffn_out_b, xa, 5, tm, tn, "ffn_out")

    out = pl.pallas_call(
        functools.partial(_final_norm_kernel, rc=rc),
        grid=(m_lat // tm,),
        in_specs=[pl.BlockSpec((tm, d), lambda i: (i, 0)),
                  pl.BlockSpec((1, d), lambda i: (0, 0))],
        out_specs=pl.BlockSpec((tm, d), lambda i: (i, 0)),
        out_shape=jax.ShapeDtypeStruct((m_lat, d), F32),
        compiler_params=_cparams("parallel"),
        name="final_norm",
    )(xa, g_final.reshape(1, d))
    return out.reshape(bsz, seq, d)
```

```python
import functools

import jax
import jax.numpy as jnp
from jax import lax
from jax.experimental import pallas as pl
from jax.experimental.pallas import tpu as pltpu

F32 = jnp.float32
BF16 = jnp.bfloat16

GRID_W = 64
CONV_CH = 512
CONV_WIDTH = 31
MLA_HEADS = 8
MLA_Q_RANK = 512
MLA_KV_RANK = 256
MLA_NOPE = 128
MLA_ROPE = 64
MLA_V = 128
GQA_HEADS = 8
GQA_KV_HEADS = 2
GQA_HEAD_DIM = 64
WINDOW = 128
N_BRANCH = 3
ROPE_DIM = 64
ROPE_BASE = 10000.0
EPS = 1e-6
NEG_INF = -1e30

COL_MLA_KV = 0
COL_MLA_KR = COL_MLA_KV + MLA_KV_RANK
COL_GQA_K = COL_MLA_KR + MLA_ROPE
COL_GQA_V = COL_GQA_K + GQA_KV_HEADS * GQA_HEAD_DIM
KV_COLS = COL_GQA_V + GQA_KV_HEADS * GQA_HEAD_DIM
COL_MLA_Q = KV_COLS
COL_GQA_Q = COL_MLA_Q + MLA_Q_RANK
COL_CONV = COL_GQA_Q + GQA_HEADS * GQA_HEAD_DIM
COL_GATE = COL_CONV + 2 * CONV_CH

LANES = 128
MLA_PAD = 256
VMEM_LIMIT = 56 * 1024 * 1024
MLA_TQ = 1024
MLA_RG = 64
MLA_TK = 2048
MLA_UNROLL = 2
LOG2_E = 1.4426950408889634

P_MLA_Q = 0
P_GQA_Q = 512
P_CONV_A = 1024
P_CONV_B = 1536
P_CKV = 2048
P_KR = 2304
P_GQA_K = 2432
P_GQA_V = 2560
P_SMALL = 3072


def _cparams(*sem):
    return pltpu.CompilerParams(dimension_semantics=sem, vmem_limit_bytes=VMEM_LIMIT)


def _for_chunks(n_rows, rc, fn):
    n = n_rows // rc
    if n == 1:
        fn(0)
        return

    def body(i, carry):
        fn(pl.multiple_of(i * rc, rc))
        return carry

    lax.fori_loop(0, n, body, 0)


def _sigmoid(x):
    return 0.5 * jnp.tanh(0.5 * x) + 0.5


def _rope(x, cos, sin_up, sin_dn):
    return x * cos + pltpu.roll(x, LANES - 16, 1) * sin_up + pltpu.roll(x, 16, 1) * sin_dn


def _dot(a, b):
    return jnp.dot(a, b, preferred_element_type=F32)


def _dot_nt(a, b):
    return lax.dot_general(a, b, (((1,), (1,)), ((), ())), preferred_element_type=F32)


def _ada_kernel(c_ref, w_ref, b_ref, o_ref):
    c = c_ref[...]
    s = c * _sigmoid(c)
    o_ref[...] = jnp.dot(s, w_ref[...], preferred_element_type=F32,
                         precision=lax.Precision.HIGHEST) + b_ref[...]


def _ada(cc, w_ada, b_ada):
    depth, d, n = w_ada.shape
    rows = cc.shape[0]
    tn = 1024
    return pl.pallas_call(
        _ada_kernel,
        grid=(depth, n // tn),
        in_specs=[
            pl.BlockSpec((rows, d), lambda l, j: (0, 0)),
            pl.BlockSpec((None, d, tn), lambda l, j: (l, 0, j)),
            pl.BlockSpec((None, 1, tn), lambda l, j: (l, 0, j)),
        ],
        out_specs=pl.BlockSpec((None, rows, tn), lambda l, j: (l, 0, j)),
        out_shape=jax.ShapeDtypeStruct((depth, rows, n), F32),
        compiler_params=_cparams("parallel", "parallel"),
        name="ada_mod",
    )(cc, w_ada, b_ada.reshape(depth, 1, n))


def _norm_mod_store(x_ref, g_ref, sh_ref, sc_ref, h_ref, rc):
    gs = g_ref[...] * (1.0 + sc_ref[...])
    sh = sh_ref[...]

    def chunk(r0):
        x = x_ref[pl.ds(r0, rc), :]
        ms = jnp.mean(x * x, axis=-1, keepdims=True)
        h_ref[pl.ds(r0, rc), :] = (x * lax.rsqrt(ms + EPS) * gs + sh).astype(BF16)

    _for_chunks(x_ref.shape[0], rc, chunk)


def _in_proj_kernel(x_ref, g_ref, sh_ref, sc_ref, w_ref, *rest, n_gate, col0, rc):
    o_ref, h_ref = rest[-2:]
    j = pl.program_id(1)

    @pl.when(j == 0)
    def _():
        _norm_mod_store(x_ref, g_ref, sh_ref, sc_ref, h_ref, rc)

    acc = _dot(h_ref[...], w_ref[...])
    o_ref[...] = jnp.where(col0 + j < n_gate, _sigmoid(acc), acc).astype(BF16)


def _ffn_in_kernel(h_ref, w1_ref, w2_ref, o_ref):
    h = h_ref[...]
    u1 = _dot(h, w1_ref[...])
    u2 = _dot(h, w2_ref[...])
    o_ref[...] = (u1 * _sigmoid(u1) * u2).astype(BF16)


def _mm_res_kernel(a_ref, w_ref, x_ref, gt_ref, o_ref):
    o_ref[...] = x_ref[...] + gt_ref[...] * _dot(a_ref[...], w_ref[...])


def _out_proj_kernel(a_ref, w_ref, x_ref, gt_ref, g_ref, sh_ref, sc_ref, *rest):
    o_ref, h_ref = rest[-2:]
    gs = g_ref[...] * (1.0 + sc_ref[...])
    rows = a_ref.shape[0] // 2
    for r in (0, rows):
        x_new = x_ref[r:r + rows, :] + gt_ref[...] * _dot(a_ref[r:r + rows, :], w_ref[...])
        o_ref[r:r + rows, :] = x_new
        ms = jnp.mean(x_new * x_new, axis=-1, keepdims=True)
        h_ref[r:r + rows, :] = (x_new * lax.rsqrt(ms + EPS) * gs + sh_ref[...]).astype(BF16)


def _merge_kernel(oc_ref, om_ref, og_ref, wc_ref, wm_ref, wg_ref, g0_ref, g1_ref, g2_ref, y_ref):
    y = g0_ref[...].astype(F32) * _dot(oc_ref[...], wc_ref[...])
    y += g1_ref[...].astype(F32) * _dot(om_ref[...], wm_ref[...])
    y += g2_ref[...].astype(F32) * _dot(og_ref[...], wg_ref[...])
    y_ref[...] = y.astype(BF16)


def _rms_bf16(a, g):
    ms = jnp.mean(a * a, axis=-1, keepdims=True)
    return (a * lax.rsqrt(ms + EPS) * g).astype(BF16)


def _qup_kernel(pq_ref, g_ref, wq_ref, cos_ref, su_ref, sd_ref, q_ref, qg_ref, *, rc, s_mla, s_gqa):
    g = g_ref[...]

    def chunk(r0):
        rows = pl.ds(r0, rc)
        n = _rms_bf16(pq_ref[rows, 0:MLA_Q_RANK].astype(F32), g)
        q = _dot(n, wq_ref[...])
        cos = cos_ref[rows, :]
        su = su_ref[rows, :]
        sd = sd_ref[rows, :]
        for h in range(MLA_HEADS):
            c0 = h * MLA_PAD
            q_ref[rows, c0:c0 + LANES] = (q[:, c0:c0 + LANES] * s_mla).astype(BF16)
            r = _rope(q[:, c0 + LANES:c0 + 2 * LANES], cos, su, sd)
            q_ref[rows, c0 + LANES:c0 + 2 * LANES] = (r * s_mla).astype(BF16)
        for t in range(GQA_HEADS * GQA_HEAD_DIM // LANES):
            gq = pq_ref[rows, MLA_Q_RANK + t * LANES:MLA_Q_RANK + (t + 1) * LANES].astype(F32)
            qg_ref[rows, t * LANES:(t + 1) * LANES] = (_rope(gq, cos, su, sd) * s_gqa).astype(BF16)

    _for_chunks(pq_ref.shape[0], rc, chunk)


def _kvup_kernel(ckv_ref, kr_ref, g_ref, wk_ref, wv_ref, cos_ref, su_ref, sd_ref,
                 k_ref, v_ref, kg_ref, vg_ref, *, rc):
    g = g_ref[...]

    def chunk(r0):
        rows = pl.ds(r0, rc)
        n = _rms_bf16(ckv_ref[rows, :].astype(F32), g)
        kn = _dot(n, wk_ref[...])
        vv = _dot(n, wv_ref[...])
        cos = cos_ref[rows, :]
        su = su_ref[rows, :]
        sd = sd_ref[rows, :]
        kr = _rope(kr_ref[rows, 0:LANES].astype(F32), cos, su, sd).astype(BF16)
        ones_tile = jnp.ones((rc, LANES), BF16)
        for h in range(MLA_HEADS):
            c0 = h * MLA_PAD
            k_ref[rows, c0:c0 + LANES] = kn[:, h * LANES:(h + 1) * LANES].astype(BF16)
            k_ref[rows, c0 + LANES:c0 + 2 * LANES] = kr
            v_ref[rows, c0:c0 + LANES] = vv[:, h * LANES:(h + 1) * LANES].astype(BF16)
            v_ref[rows, c0 + LANES:c0 + 2 * LANES] = ones_tile
        low_half = lax.broadcasted_iota(jnp.int32, (rc, LANES), 1) < GQA_HEAD_DIM
        gk = _rope(kr_ref[rows, LANES:2 * LANES].astype(F32), cos, su, sd)
        gk_sw = pltpu.roll(gk, GQA_HEAD_DIM, 1)
        kg_ref[rows, 0:LANES] = jnp.where(low_half, gk, gk_sw).astype(BF16)
        kg_ref[rows, LANES:2 * LANES] = jnp.where(low_half, gk_sw, gk).astype(BF16)
        gv = kr_ref[rows, 2 * LANES:3 * LANES].astype(F32)
        gv_sw = pltpu.roll(gv, GQA_HEAD_DIM, 1)
        vg_ref[rows, 0:LANES] = jnp.where(low_half, gv, gv_sw).astype(BF16)
        vg_ref[rows, LANES:2 * LANES] = ones_tile
        vg_ref[rows, 2 * LANES:3 * LANES] = jnp.where(low_half, gv_sw, gv).astype(BF16)
        vg_ref[rows, 3 * LANES:4 * LANES] = ones_tile

    _for_chunks(ckv_ref.shape[0], rc, chunk)


def _mla_attn_kernel(q_ref, k_ref, v_ref, *rest, tk, rg, unroll):
    n_buf = 3 * unroll
    o_ref = rest[-n_buf - 3]
    bufs = rest[-n_buf - 2:-2]
    s_buf, p_buf, a_buf = bufs[:unroll], bufs[unroll:2 * unroll], bufs[2 * unroll:]
    m_ref, acc_ref = rest[-2:]
    tq = q_ref.shape[0]
    head = k_ref.shape[0] % tk
    has_head = 1 if head else 0
    n = k_ref.shape[0] // tk + has_head

    def start_of(c):
        return head + (c - has_head) * tk

    def size_of(c):
        return head if (has_head and c == 0) else tk

    def qk(slot, start, size):
        s_buf[slot][:, 0:size] = _dot_nt(q_ref[...], k_ref[pl.ds(start, size), :])

    def sm(slot, size):
        for r in range(0, tq, rg):
            s = s_buf[slot][r:r + rg, 0:size]
            m_old = m_ref[r:r + rg, :]
            m_new = jnp.maximum(m_old, jnp.max(s, axis=-1, keepdims=True))
            p_buf[slot][r:r + rg, 0:size] = jnp.exp2(s - pltpu.repeat(m_new, size // LANES, 1)).astype(BF16)
            a_buf[slot][r:r + rg, :] = jnp.exp2(m_old - m_new)
            m_ref[r:r + rg, :] = m_new

    def pv(slot, start, size):
        alpha = pltpu.repeat(a_buf[slot][...], acc_ref.shape[1] // LANES, 1)
        acc_ref[...] = alpha * acc_ref[...] + _dot(p_buf[slot][:, 0:size], v_ref[pl.ds(start, size), :])

    def static_start(c):
        return 0 if (has_head and c == 0) else start_of(c)

    def static_tick(t):
        if 0 <= t < n:
            qk(t % unroll, static_start(t), size_of(t))
        if 0 <= t - 2 < n:
            pv((t - 2) % unroll, static_start(t - 2), size_of(t - 2))
        if 0 <= t - 1 < n:
            sm((t - 1) % unroll, size_of(t - 1))

    m_ref[...] = jnp.full(m_ref.shape, NEG_INF, F32)
    acc_ref[...] = jnp.zeros(acc_ref.shape, F32)

    first_steady = 2 + has_head
    trips = max(n - first_steady, 0) // unroll
    for t in range(first_steady):
        static_tick(t)
    if trips:
        def trip(j, carry):
            for u in range(unroll):
                t = first_steady + unroll * j + u
                qk((first_steady + u) % unroll, pl.multiple_of(start_of(t), 2 * LANES), tk)
                pv((first_steady + u - 2) % unroll, pl.multiple_of(start_of(t - 2), 2 * LANES), tk)
                sm((first_steady + u - 1) % unroll, tk)
            return carry

        lax.fori_loop(0, trips, trip, 0)
    for t in range(first_steady + unroll * trips, n + 2):
        static_tick(t)
    acc = acc_ref[...]
    o_ref[...] = (acc[:, :MLA_V] / acc[:, MLA_V:MLA_V + 1]).astype(BF16)


def _gqa_attn_kernel(q_ref, kc_ref, vc_ref, sink_ref, *rest, seq, n_sub, latent):
    if latent:
        kl_ref, vl_ref = rest[0], rest[1]
    o_ref, s0, s1, p0, p1, e0, e1 = rest[-7:]
    s_buf, p_buf, e_buf = (s0, s1), (p0, p1), (e0, e1)
    span = 3 * WINDOW
    n_c = kc_ref.shape[0]
    rep = GQA_HEADS // GQA_KV_HEADS
    lane = lax.broadcasted_iota(jnp.int32, (1, LANES), 1)
    low_half = lane < GQA_HEAD_DIM
    half_masks = (jnp.where(low_half, 1.0, 0.0).astype(BF16), jnp.where(low_half, 0.0, 1.0).astype(BF16))
    sinks = sink_ref[...] * LOG2_E
    sinks = [sinks[:, h:h + 1] for h in range(GQA_HEADS)]

    def row0(r):
        return r * WINDOW if isinstance(r, int) else pl.multiple_of(r * WINDOW, WINDOW)

    def win_start(r):
        if isinstance(r, int):
            return min(max(r * WINDOW - WINDOW, 0), seq - span)
        return pl.multiple_of(jnp.clip(r * WINDOW - WINDOW, 0, seq - span), WINDOW)

    def qk(slot, r):
        rows = pl.ds(row0(r), WINDOW)
        for h in range(GQA_HEADS):
            g = h // rep
            q = q_ref[rows, (h // 2) * LANES:(h // 2 + 1) * LANES] * half_masks[h % 2]
            s_buf[slot][h, :, 0:n_c] = _dot_nt(q, kc_ref[:, g * LANES:(g + 1) * LANES])
            if latent:
                s_buf[slot][h, :, n_c:n_c + span] = _dot_nt(
                    q, kl_ref[pl.ds(win_start(r), span), g * LANES:(g + 1) * LANES])

    def sm(slot, r):
        if latent:
            qpos = r * WINDOW + lax.broadcasted_iota(jnp.int32, (WINDOW, span), 0)
            kpos = win_start(r) + lax.broadcasted_iota(jnp.int32, (WINDOW, span), 1)
            bias = jnp.where(jnp.abs(qpos - kpos) <= WINDOW, 0.0, NEG_INF)
        for h in range(GQA_HEADS):
            sc = s_buf[slot][h, :, 0:n_c]
            m = jnp.maximum(jnp.max(sc, axis=-1, keepdims=True), sinks[h])
            if latent:
                sw = s_buf[slot][h, :, n_c:n_c + span] + bias
                m = jnp.maximum(m, jnp.max(sw, axis=-1, keepdims=True))
                p_buf[slot][h, :, n_c:n_c + span] = jnp.exp2(sw - m).astype(BF16)
            p_buf[slot][h, :, 0:n_c] = jnp.exp2(sc - m).astype(BF16)
            e_buf[slot][h] = jnp.broadcast_to(jnp.exp2(sinks[h] - m), (WINDOW, LANES))

    def pv(slot, r):
        outs = []
        for h in range(GQA_HEADS):
            g = h // rep
            o = _dot(p_buf[slot][h, :, 0:n_c], vc_ref[:, g * 2 * LANES:(g + 1) * 2 * LANES])
            if latent:
                o = o + _dot(p_buf[slot][h, :, n_c:n_c + span],
                             vl_ref[pl.ds(win_start(r), span), g * 2 * LANES:(g + 1) * 2 * LANES])
            outs.append(o[:, :LANES] / (o[:, LANES:] + e_buf[slot][h]))
        rows = pl.ds(row0(r), WINDOW)
        for t in range(GQA_HEADS // 2):
            o_ref[rows, t * LANES:(t + 1) * LANES] = jnp.where(low_half, outs[2 * t], outs[2 * t + 1]).astype(BF16)

    def tick(t, par):
        qk(par, t)
        pv(par, t - 2)
        sm(1 - par, t - 1)

    def static_tick(t):
        if 0 <= t < n_sub:
            qk(t % 2, t)
        if 0 <= t - 2 < n_sub:
            pv(t % 2, t - 2)
        if 0 <= t - 1 < n_sub:
            sm((t - 1) % 2, t - 1)

    trips = max(n_sub - 2, 0) // 2
    for t in range(2):
        static_tick(t)
    if trips:
        def trip(j, carry):
            for u in range(2):
                tick(2 + 2 * j + u, u)
            return carry

        lax.fori_loop(0, trips, trip, 0)
    for t in range(2 + 2 * trips, n_sub + 2):
        static_tick(t)


def _conv_kernel(a_ref, b_ref, w_ref, cb_ref, lg_ref, lb_ref, o_ref, u_ref, *, seq, rc):
    halo = 16
    u_ref[0:halo, :] = jnp.zeros((halo, CONV_CH), F32)
    u_ref[halo + seq:2 * halo + seq, :] = jnp.zeros((halo, CONV_CH), F32)

    def fill(r0):
        a = a_ref[pl.ds(r0, 256), :].astype(F32)
        b = b_ref[pl.ds(r0, 256), :].astype(F32)
        u_ref[pl.ds(halo + r0, 256), :] = a * _sigmoid(b)

    _for_chunks(seq, 256, fill)

    cb = cb_ref[...]
    lg = lg_ref[...]
    lb = lb_ref[...]

    half = CONV_CH // 2
    first_tap = halo - CONV_WIDTH // 2

    def conv(r0):
        parts = []
        for c0 in (0, half):
            acc = jnp.zeros((rc, half), F32) + cb[:, c0:c0 + half]
            for s in range(8):
                part = None
                for a in range(4):
                    k = 8 * a + s - first_tap
                    if 0 <= k < CONV_WIDTH:
                        rows = u_ref[pl.ds(pl.multiple_of(r0 + 8 * a, 8), rc + 8), c0:c0 + half]
                        term = w_ref[k:k + 1, c0:c0 + half] * rows
                        part = term if part is None else part + term
                acc = acc + part[s:s + rc]
            parts.append(acc)
        acc = jnp.concatenate(parts, axis=-1)
        mu = jnp.mean(acc, axis=-1, keepdims=True)
        xc = acc - mu
        var = jnp.mean(xc * xc, axis=-1, keepdims=True)
        y = xc * lax.rsqrt(var + EPS) * lg + lb
        o_ref[pl.ds(r0, rc), :] = (y * _sigmoid(y)).astype(BF16)

    _for_chunks(seq, rc, conv)


def _final_norm_kernel(x_ref, g_ref, o_ref, *, rc):
    g = g_ref[...]

    def chunk(r0):
        x = x_ref[pl.ds(r0, rc), :]
        ms = jnp.mean(x * x, axis=-1, keepdims=True)
        o_ref[pl.ds(r0, rc), :] = x * lax.rsqrt(ms + EPS) * g

    _for_chunks(x_ref.shape[0], rc, chunk)


def _rope_tables(length, extra):
    rows = length // GRID_W
    row = jnp.repeat(jnp.arange(rows), GRID_W).astype(F32)
    col = jnp.tile(jnp.arange(GRID_W), rows).astype(F32)
    n_freq = ROPE_DIM // 4
    inv_freq = ROPE_BASE ** (-jnp.arange(n_freq, dtype=F32) / n_freq)
    a_row = row[:, None] * inv_freq[None, :]
    a_col = col[:, None] * inv_freq[None, :]
    ang = jnp.concatenate([a_row, a_row, a_col, a_col], axis=-1)
    cos = jnp.concatenate([jnp.cos(ang), jnp.ones((extra, ROPE_DIM), F32)], axis=0)
    sin = jnp.concatenate([jnp.sin(ang), jnp.zeros((extra, ROPE_DIM), F32)], axis=0)
    cos = jnp.tile(cos, (1, LANES // ROPE_DIM))
    sin = jnp.tile(sin, (1, LANES // ROPE_DIM))
    first = (jnp.arange(LANES) // 16) % 2 == 0
    sin_up = jnp.where(first[None, :], -sin, 0.0)
    sin_dn = jnp.where(first[None, :], 0.0, sin)
    return cos, sin_up, sin_dn


def _pick_tile(*sizes):
    for t in (1024, 512, 256):
        if all(s % t == 0 for s in sizes):
            return t
    raise ValueError(f"unsupported row counts {sizes}")


def kernel(x, c, ctx, c_ctx, w_ada, b_ada, g_mix, w_in, conv_w, conv_b, conv_ln_g, conv_ln_b, w_conv_out, g_q_a, w_q_up, g_kv_a, w_kv_up, w_mla_out, gqa_sink, w_gqa_out, w_out, g_ffn, w_ffn_in, w_ffn_out, g_final):
    bsz, seq, d = x.shape
    n_ctx = ctx.shape[1]
    depth = w_ada.shape[0]
    d_ff = w_ffn_out.shape[1]
    m_lat = bsz * seq
    m_ctx = bsz * n_ctx
    m_all = m_lat + m_ctx
    assert seq % GRID_W == 0 and n_ctx == 2 * WINDOW and seq % (4 * WINDOW) == 0
    assert bsz + 1 <= 16 and w_in.shape[2] == COL_GATE + N_BRANCH * d

    tm = _pick_tile(seq, m_ctx)
    ts = min(tm, 512)
    tn = 512
    rc = 256
    n_lat_tiles = m_lat // tm
    tiles_per_batch = seq // tm
    p_base = N_BRANCH * d
    p_cols = p_base + P_SMALL
    tn_in = 1536
    assert p_base % tn_in == 0 and p_cols % tn_in == 0 and p_cols - tn_in <= p_base + P_CKV
    ctx_blk0 = m_lat // n_ctx

    def mod_row(i, tile):
        return jnp.minimum(i // (seq // tile), bsz)

    def cols(a, n):
        return w_in[:, :, a:a + n].astype(BF16)

    def zeros(n):
        return jnp.zeros((depth, d, n), BF16)

    w_in_r = jnp.concatenate([
        cols(COL_GATE, N_BRANCH * d),
        cols(COL_MLA_Q, MLA_Q_RANK), cols(COL_GQA_Q, GQA_HEADS * GQA_HEAD_DIM),
        cols(COL_CONV, CONV_CH), cols(COL_CONV + CONV_CH, CONV_CH),
        cols(COL_MLA_KV, MLA_KV_RANK), cols(COL_MLA_KR, MLA_ROPE), zeros(LANES - MLA_ROPE),
        cols(COL_GQA_K, GQA_KV_HEADS * GQA_HEAD_DIM), cols(COL_GQA_V, GQA_KV_HEADS * GQA_HEAD_DIM),
        zeros(P_SMALL - P_GQA_V - GQA_KV_HEADS * GQA_HEAD_DIM)], axis=-1)
    wq = w_q_up.reshape(depth, MLA_Q_RANK, MLA_HEADS, MLA_NOPE + MLA_ROPE).astype(BF16)
    wq = jnp.pad(wq, ((0, 0), (0, 0), (0, 0), (0, MLA_PAD - MLA_NOPE - MLA_ROPE)))
    wq = wq.reshape(depth, MLA_Q_RANK, MLA_HEADS * MLA_PAD)
    wkv = w_kv_up.reshape(depth, MLA_KV_RANK, MLA_HEADS, MLA_NOPE + MLA_V).astype(BF16)
    wk = wkv[..., :MLA_NOPE].reshape(depth, MLA_KV_RANK, MLA_HEADS * MLA_NOPE)
    wv = wkv[..., MLA_NOPE:].reshape(depth, MLA_KV_RANK, MLA_HEADS * MLA_V)
    w_conv_out_b = w_conv_out.astype(BF16)
    w_mla_out_b = w_mla_out.astype(BF16)
    w_gqa_out_b = w_gqa_out.astype(BF16)
    w_out_b = w_out.astype(BF16)
    w_ffn_in_b = w_ffn_in.astype(BF16)
    w_ffn_out_b = w_ffn_out.astype(BF16)
    cos, sin_up, sin_dn = _rope_tables(seq, ts)

    cc = jnp.zeros((16, d), F32).at[:bsz].set(c).at[bsz].set(c_ctx)
    mod = _ada(cc, w_ada, b_ada).reshape(depth, 16, 1, 6 * d)

    x_lat = x.reshape(m_lat, d)
    x_ctx = ctx.reshape(m_ctx, d)
    xa = None

    def vec(a):
        return a.reshape(depth, 1, a.shape[-1])

    g_mix3, g_ffn3, g_q3, g_kv3 = vec(g_mix), vec(g_ffn), vec(g_q_a), vec(g_kv_a)
    conv_b3, ln_g3, ln_b3, sink3 = vec(conv_b), vec(conv_ln_g), vec(conv_ln_b), vec(gqa_sink)

    def mod_spec(l, part, tile, width, with_j, tile0=0):
        nb = d // width
        if with_j:
            return pl.BlockSpec((None, None, 1, width),
                                lambda i, j: (l, mod_row(tile0 + i, tile), 0, part * nb + j))
        return pl.BlockSpec((None, None, 1, width), lambda i, j: (l, mod_row(tile0 + i, tile), 0, part))

    def table_specs(tile):
        per_batch = seq // tile
        n_lat = m_lat // tile
        return [pl.BlockSpec((tile, LANES), lambda i: (jnp.where(i < n_lat, i % per_batch, per_batch), 0))] * 3

    for l in range(depth):
        last = l == depth - 1
        m_rows = m_lat if last else m_all

        lat_src, lat_tile0 = (x_lat, 0) if l == 0 else (xa, 0)
        ctx_src, ctx_tile0 = (x_ctx, 0) if l == 0 else (xa, n_lat_tiles)

        def in_proj_call(src, src_tile0, row_tile0, n_row_tiles, col_tile0, n_col_tiles, prev):
            in_specs = [
                pl.BlockSpec((tm, d), lambda i, j: (src_tile0 + i, 0)),
                pl.BlockSpec((None, 1, d), lambda i, j: (l, 0, 0)),
                mod_spec(l, 0, tm, d, False, row_tile0),
                mod_spec(l, 1, tm, d, False, row_tile0),
                pl.BlockSpec((None, d, tn_in), lambda i, j: (l, 0, col_tile0 + j)),
            ]
            args = [src, g_mix3, mod, mod, w_in_r]
            aliases = {}
            if prev is not None:
                in_specs.append(pl.BlockSpec(memory_space=pl.ANY))
                args.append(prev)
                aliases = {5: 0}
            return pl.pallas_call(
                functools.partial(_in_proj_kernel, n_gate=p_base // tn_in, col0=col_tile0, rc=rc),
                grid=(n_row_tiles, n_col_tiles),
                in_specs=in_specs,
                out_specs=pl.BlockSpec((tm, tn_in), lambda i, j: (row_tile0 + i, col_tile0 + j)),
                out_shape=jax.ShapeDtypeStruct((m_all, p_cols), BF16),
                scratch_shapes=[pltpu.VMEM((tm, d), BF16)],
                input_output_aliases=aliases,
                compiler_params=_cparams("parallel", "arbitrary"),
                name="in_proj" if prev is None else "in_proj_ctx",
            )(*args)

        n_col_tiles = p_cols // tn_in
        p = in_proj_call(lat_src, lat_tile0, 0, n_lat_tiles, 0, n_col_tiles, None)
        if last:
            p = in_proj_call(ctx_src, ctx_tile0, n_lat_tiles, m_ctx // tm, n_col_tiles - 1, 1, p)
        else:
            p = in_proj_call(ctx_src, ctx_tile0, n_lat_tiles, m_ctx // tm, 0, n_col_tiles, p)

        q_mla, q_gqa = pl.pallas_call(
            functools.partial(_qup_kernel, rc=rc, s_mla=float(LOG2_E * (MLA_NOPE + MLA_ROPE) ** -0.5),
                              s_gqa=float(LOG2_E * GQA_HEAD_DIM ** -0.5)),
            grid=(m_rows // ts,),
            in_specs=[
                pl.BlockSpec((ts, 1024), lambda i: (i, (p_base + P_MLA_Q) // 1024)),
                pl.BlockSpec((None, 1, MLA_Q_RANK), lambda i: (l, 0, 0)),
                pl.BlockSpec((None, MLA_Q_RANK, MLA_HEADS * MLA_PAD), lambda i: (l, 0, 0)),
            ] + table_specs(ts),
            out_specs=[pl.BlockSpec((ts, MLA_HEADS * MLA_PAD), lambda i: (i, 0)),
                       pl.BlockSpec((ts, GQA_HEADS * GQA_HEAD_DIM), lambda i: (i, 0))],
            out_shape=[jax.ShapeDtypeStruct((m_all, MLA_HEADS * MLA_PAD), BF16),
                       jax.ShapeDtypeStruct((m_all, GQA_HEADS * GQA_HEAD_DIM), BF16)],
            compiler_params=_cparams("parallel"),
            name="q_up",
        )(p, g_q3, wq, cos, sin_up, sin_dn)

        tkv = n_ctx
        kv_per_batch = seq // tkv
        kv_lat_tiles = m_lat // tkv

        def kv_block(i):
            lat = (i // kv_per_batch) * (kv_per_batch + 1) + 1 + i % kv_per_batch
            return jnp.where(i < kv_lat_tiles, lat, (i - kv_lat_tiles) * (kv_per_batch + 1))

        k_mla, v_mla, k_gqa, v_gqa = pl.pallas_call(
            functools.partial(_kvup_kernel, rc=tkv),
            grid=(m_all // tkv,),
            in_specs=[
                pl.BlockSpec((tkv, MLA_KV_RANK), lambda i: (i, (p_base + P_CKV) // MLA_KV_RANK)),
                pl.BlockSpec((tkv, 3 * LANES), lambda i: (i, (p_base + P_KR) // (3 * LANES))),
                pl.BlockSpec((None, 1, MLA_KV_RANK), lambda i: (l, 0, 0)),
                pl.BlockSpec((None, MLA_KV_RANK, MLA_HEADS * MLA_NOPE), lambda i: (l, 0, 0)),
                pl.BlockSpec((None, MLA_KV_RANK, MLA_HEADS * MLA_V), lambda i: (l, 0, 0)),
            ] + table_specs(tkv),
            out_specs=[pl.BlockSpec((tkv, MLA_HEADS * MLA_PAD), lambda i: (kv_block(i), 0)),
                       pl.BlockSpec((tkv, MLA_HEADS * MLA_PAD), lambda i: (kv_block(i), 0)),
                       pl.BlockSpec((tkv, 2 * LANES), lambda i: (i, 0)),
                       pl.BlockSpec((tkv, 4 * LANES), lambda i: (i, 0))],
            out_shape=[jax.ShapeDtypeStruct((m_all, MLA_HEADS * MLA_PAD), BF16),
                       jax.ShapeDtypeStruct((m_all, MLA_HEADS * MLA_PAD), BF16),
                       jax.ShapeDtypeStruct((m_all, 2 * LANES), BF16),
                       jax.ShapeDtypeStruct((m_all, 4 * LANES), BF16)],
            compiler_params=_cparams("parallel"),
            name="kv_up",
        )(p, p, g_kv3, wk, wv, cos, sin_up, sin_dn)

        tq = min(MLA_TQ, seq)
        nq = seq // tq
        def mla_scratch(rows, tk):
            return ([pltpu.VMEM((rows, tk), F32)] * MLA_UNROLL + [pltpu.VMEM((rows, tk), BF16)] * MLA_UNROLL
                    + [pltpu.VMEM((rows, LANES), F32)] * (MLA_UNROLL + 1) + [pltpu.VMEM((rows, MLA_PAD), F32)])

        o_mla = pl.pallas_call(
            functools.partial(_mla_attn_kernel, tk=MLA_TK, rg=MLA_RG, unroll=MLA_UNROLL),
            grid=(bsz, MLA_HEADS, nq),
            scratch_shapes=mla_scratch(tq, MLA_TK),
            in_specs=[
                pl.BlockSpec((tq, MLA_PAD), lambda b, h, i: (b * nq + i, h)),
                pl.BlockSpec((seq + n_ctx, MLA_PAD), lambda b, h, i: (b, h)),
                pl.BlockSpec((seq + n_ctx, MLA_PAD), lambda b, h, i: (b, h)),
            ],
            out_specs=pl.BlockSpec((tq, MLA_V), lambda b, h, i: (b * nq + i, h)),
            out_shape=jax.ShapeDtypeStruct((m_all, MLA_HEADS * MLA_V), BF16),
            compiler_params=_cparams("parallel", "parallel", "arbitrary"),
            name="mla_attn",
        )(q_mla, k_mla, v_mla)
        o_mla = o_mla if last else pl.pallas_call(
            functools.partial(_mla_attn_kernel, tk=n_ctx, rg=MLA_RG, unroll=MLA_UNROLL),
            grid=(bsz, MLA_HEADS),
            scratch_shapes=mla_scratch(n_ctx, n_ctx),
            in_specs=[
                pl.BlockSpec((n_ctx, MLA_PAD), lambda b, h: (ctx_blk0 + b, h)),
                pl.BlockSpec((n_ctx, MLA_PAD), lambda b, h: (b * (kv_per_batch + 1), h)),
                pl.BlockSpec((n_ctx, MLA_PAD), lambda b, h: (b * (kv_per_batch + 1), h)),
                pl.BlockSpec(memory_space=pl.ANY),
            ],
            out_specs=pl.BlockSpec((n_ctx, MLA_V), lambda b, h: (ctx_blk0 + b, h)),
            out_shape=jax.ShapeDtypeStruct((m_all, MLA_HEADS * MLA_V), BF16),
            input_output_aliases={3: 0},
            compiler_params=_cparams("parallel", "parallel"),
            name="mla_attn_ctx",
        )(q_mla, k_mla, v_mla, o_mla)

        def gqa_scratch(cols):
            return ([pltpu.VMEM((GQA_HEADS, WINDOW, cols), F32)] * 2
                    + [pltpu.VMEM((GQA_HEADS, WINDOW, cols), BF16)] * 2
                    + [pltpu.VMEM((GQA_HEADS, WINDOW, LANES), F32)] * 2)

        o_gqa = pl.pallas_call(
            functools.partial(_gqa_attn_kernel, seq=seq, n_sub=seq // WINDOW, latent=True),
            grid=(bsz,),
            in_specs=[
                pl.BlockSpec((seq, GQA_HEADS * GQA_HEAD_DIM), lambda b: (b, 0)),
                pl.BlockSpec((n_ctx, 2 * LANES), lambda b: (ctx_blk0 + b, 0)),
                pl.BlockSpec((n_ctx, 4 * LANES), lambda b: (ctx_blk0 + b, 0)),
                pl.BlockSpec((None, 1, GQA_HEADS), lambda b: (l, 0, 0)),
                pl.BlockSpec((seq, 2 * LANES), lambda b: (b, 0)),
                pl.BlockSpec((seq, 4 * LANES), lambda b: (b, 0)),
            ],
            out_specs=pl.BlockSpec((seq, GQA_HEADS * GQA_HEAD_DIM), lambda b: (b, 0)),
            out_shape=jax.ShapeDtypeStruct((m_all, GQA_HEADS * GQA_HEAD_DIM), BF16),
            scratch_shapes=gqa_scratch(n_ctx + 3 * WINDOW),
            compiler_params=_cparams("parallel"),
            name="gqa_attn",
        )(q_gqa, k_gqa, v_gqa, sink3, k_gqa, v_gqa)
        o_gqa = o_gqa if last else pl.pallas_call(
            functools.partial(_gqa_attn_kernel, seq=n_ctx, n_sub=n_ctx // WINDOW, latent=False),
            grid=(bsz,),
            scratch_shapes=gqa_scratch(n_ctx),
            in_specs=[
                pl.BlockSpec((n_ctx, GQA_HEADS * GQA_HEAD_DIM), lambda b: (ctx_blk0 + b, 0)),
                pl.BlockSpec((n_ctx, 2 * LANES), lambda b: (ctx_blk0 + b, 0)),
                pl.BlockSpec((n_ctx, 4 * LANES), lambda b: (ctx_blk0 + b, 0)),
                pl.BlockSpec((None, 1, GQA_HEADS), lambda b: (l, 0, 0)),
                pl.BlockSpec(memory_space=pl.ANY),
            ],
            out_specs=pl.BlockSpec((n_ctx, GQA_HEADS * GQA_HEAD_DIM), lambda b: (ctx_blk0 + b, 0)),
            out_shape=jax.ShapeDtypeStruct((m_all, GQA_HEADS * GQA_HEAD_DIM), BF16),
            input_output_aliases={4: 0},
            compiler_params=_cparams("parallel"),
            name="gqa_attn_ctx",
        )(q_gqa, k_gqa, v_gqa, sink3, o_gqa)

        def conv_call(length, blk0, prev):
            crc = 64
            in_specs = [
                pl.BlockSpec((length, CONV_CH), lambda b: (blk0 + b, (p_base + P_CONV_A) // CONV_CH)),
                pl.BlockSpec((length, CONV_CH), lambda b: (blk0 + b, (p_base + P_CONV_B) // CONV_CH)),
                pl.BlockSpec((None, CONV_WIDTH, CONV_CH), lambda b: (l, 0, 0)),
                pl.BlockSpec((None, 1, CONV_CH), lambda b: (l, 0, 0)),
                pl.BlockSpec((None, 1, CONV_CH), lambda b: (l, 0, 0)),
                pl.BlockSpec((None, 1, CONV_CH), lambda b: (l, 0, 0)),
            ]
            args = [p, p, conv_w, conv_b3, ln_g3, ln_b3]
            aliases = {}
            if prev is not None:
                in_specs.append(pl.BlockSpec(memory_space=pl.ANY))
                args.append(prev)
                aliases = {6: 0}

            def body(a_ref, b_ref, w_ref, cb_ref, lg_ref, lb_ref, *rest):
                o_ref, u_ref = rest[-2:]
                _conv_kernel(a_ref, b_ref, w_ref, cb_ref, lg_ref, lb_ref, o_ref, u_ref, seq=length, rc=crc)

            return pl.pallas_call(
                body,
                grid=(bsz,),
                in_specs=in_specs,
                out_specs=pl.BlockSpec((length, CONV_CH), lambda b: (blk0 + b, 0)),
                out_shape=jax.ShapeDtypeStruct((m_all, CONV_CH), BF16),
                scratch_shapes=[pltpu.VMEM((length + 32, CONV_CH), F32)],
                input_output_aliases=aliases,
                compiler_params=_cparams("parallel"),
                name="conv_branch" if prev is None else "conv_branch_ctx",
            )(*args)

        o_conv = conv_call(seq, 0, None)
        if not last:
            o_conv = conv_call(n_ctx, ctx_blk0, o_conv)

        y = pl.pallas_call(
            _merge_kernel,
            grid=(m_rows // ts, 1),
            in_specs=[
                pl.BlockSpec((ts, CONV_CH), lambda i, j: (i, 0)),
                pl.BlockSpec((ts, MLA_HEADS * MLA_V), lambda i, j: (i, 0)),
                pl.BlockSpec((ts, GQA_HEADS * GQA_HEAD_DIM), lambda i, j: (i, 0)),
                pl.BlockSpec((None, CONV_CH, d), lambda i, j: (l, 0, 0)),
                pl.BlockSpec((None, MLA_HEADS * MLA_V, d), lambda i, j: (l, 0, 0)),
                pl.BlockSpec((None, GQA_HEADS * GQA_HEAD_DIM, d), lambda i, j: (l, 0, 0)),
                pl.BlockSpec((ts, d), lambda i, j: (i, 0)),
                pl.BlockSpec((ts, d), lambda i, j: (i, 1)),
                pl.BlockSpec((ts, d), lambda i, j: (i, 2)),
            ],
            out_specs=pl.BlockSpec((ts, d), lambda i, j: (i, 0)),
            out_shape=jax.ShapeDtypeStruct((m_all, d), BF16),
            compiler_params=_cparams("parallel", "parallel"),
            name="merge",
        )(o_conv, o_mla, o_gqa, w_conv_out_b, w_mla_out_b, w_gqa_out_b, p, p, p)

        def mm_res(a, w, xin, part, tile, width, name):
            kdim = a.shape[1]
            return pl.pallas_call(
                _mm_res_kernel,
                grid=(m_rows // tile, d // width),
                in_specs=[
                    pl.BlockSpec((tile, kdim), lambda i, j: (i, 0)),
                    pl.BlockSpec((None, kdim, width), lambda i, j: (l, 0, j)),
                    pl.BlockSpec((tile, width), lambda i, j: (i, j)),
                    mod_spec(l, part, tile, width, True),
                ],
                out_specs=pl.BlockSpec((tile, width), lambda i, j: (i, j)),
                out_shape=jax.ShapeDtypeStruct((m_all, d), F32),
                compiler_params=_cparams("parallel", "parallel"),
                name=name,
            )(a, w, xin, mod)

        def out_proj_call(res_src, res_tile0, row_tile0, n_row_tiles, prev):
            in_specs = [
                pl.BlockSpec((ts, d), lambda i, j: (row_tile0 + i, 0)),
                pl.BlockSpec((None, d, d), lambda i, j: (l, 0, 0)),
                pl.BlockSpec((ts, d), lambda i, j: (res_tile0 + i, 0)),
                mod_spec(l, 2, ts, d, False, row_tile0),
                pl.BlockSpec((None, 1, d), lambda i, j: (l, 0, 0)),
                mod_spec(l, 3, ts, d, False, row_tile0),
                mod_spec(l, 4, ts, d, False, row_tile0),
            ]
            args = [y, w_out_b, res_src, mod, g_ffn3, mod, mod]
            aliases = {}
            if prev is not None:
                in_specs += [pl.BlockSpec(memory_space=pl.ANY)] * 2
                args += list(prev)
                aliases = {7: 0, 8: 1}
            return pl.pallas_call(
                _out_proj_kernel,
                grid=(n_row_tiles, 1),
                in_specs=in_specs,
                out_specs=[pl.BlockSpec((ts, d), lambda i, j: (row_tile0 + i, 0))] * 2,
                out_shape=[jax.ShapeDtypeStruct((m_all, d), F32), jax.ShapeDtypeStruct((m_all, d), BF16)],
                input_output_aliases=aliases,
                compiler_params=_cparams("parallel", "parallel"),
                name="out_proj" if prev is None else "out_proj_ctx",
            )(*args)

        ts_per_tm = tm // ts
        xa, h_ffn = out_proj_call(lat_src, lat_tile0 * ts_per_tm, 0, m_lat // ts, None)
        if not last:
            xa, h_ffn = out_proj_call(ctx_src, ctx_tile0 * ts_per_tm, m_lat // ts, m_ctx // ts, (xa, h_ffn))

        nf = d_ff // tn
        act = pl.pallas_call(
            _ffn_in_kernel,
            grid=(m_rows // tm, nf),
            in_specs=[
                pl.BlockSpec((tm, d), lambda i, j: (i, 0)),
                pl.BlockSpec((None, d, tn), lambda i, j: (l, 0, j)),
                pl.BlockSpec((None, d, tn), lambda i, j: (l, 0, nf + j)),
            ],
            out_specs=pl.BlockSpec((tm, tn), lambda i, j: (i, j)),
            out_shape=jax.ShapeDtypeStruct((m_all, d_ff), BF16),
            compiler_params=_cparams("parallel", "parallel"),
            name="ffn_in",
        )(h_ffn, w_ffn_in_b, w_ffn_in_b)
        xa = mm_res(act, w_ffn_out_b, xa, 5, tm, tn, "ffn_out")

    out = pl.pallas_call(
        functools.partial(_final_norm_kernel, rc=rc),
        grid=(m_lat // tm,),
        in_specs=[pl.BlockSpec((tm, d), lambda i: (i, 0)),
                  pl.BlockSpec((1, d), lambda i: (0, 0))],
        out_specs=pl.BlockSpec((tm, d), lambda i: (i, 0)),
        out_shape=jax.ShapeDtypeStruct((m_lat, d), F32),
        compiler_params=_cparams("parallel"),
        name="final_norm",
    )(xa, g_final.reshape(1, d))
    return out.reshape(bsz, seq, d)
```

```python
import functools

import jax
import jax.numpy as jnp
from jax import lax
from jax.experimental import pallas as pl
from jax.experimental.pallas import tpu as pltpu

F32 = jnp.float32
BF16 = jnp.bfloat16

GRID_W = 64
CONV_CH = 512
CONV_WIDTH = 31
MLA_HEADS = 8
MLA_Q_RANK = 512
MLA_KV_RANK = 256
MLA_NOPE = 128
MLA_ROPE = 64
MLA_V = 128
GQA_HEADS = 8
GQA_KV_HEADS = 2
GQA_HEAD_DIM = 64
WINDOW = 128
N_BRANCH = 3
ROPE_DIM = 64
ROPE_BASE = 10000.0
EPS = 1e-6
NEG_INF = -1e30

COL_MLA_KV = 0
COL_MLA_KR = COL_MLA_KV + MLA_KV_RANK
COL_GQA_K = COL_MLA_KR + MLA_ROPE
COL_GQA_V = COL_GQA_K + GQA_KV_HEADS * GQA_HEAD_DIM
KV_COLS = COL_GQA_V + GQA_KV_HEADS * GQA_HEAD_DIM
COL_MLA_Q = KV_COLS
COL_GQA_Q = COL_MLA_Q + MLA_Q_RANK
COL_CONV = COL_GQA_Q + GQA_HEADS * GQA_HEAD_DIM
COL_GATE = COL_CONV + 2 * CONV_CH

LANES = 128
MLA_PAD = 256
VMEM_LIMIT = 56 * 1024 * 1024
MLA_TQ = 1024
MLA_RG = 64
MLA_TK = 2048
MLA_UNROLL = 2
LOG2_E = 1.4426950408889634

P_MLA_Q = 0
P_GQA_Q = 512
P_CONV_A = 1024
P_CONV_B = 1536
P_CKV = 2048
P_KR = 2304
P_GQA_K = 2432
P_GQA_V = 2560
P_SMALL = 3072


def _cparams(*sem):
    return pltpu.CompilerParams(dimension_semantics=sem, vmem_limit_bytes=VMEM_LIMIT)


def _for_chunks(n_rows, rc, fn):
    n = n_rows // rc
    if n == 1:
        fn(0)
        return

    def body(i, carry):
        fn(pl.multiple_of(i * rc, rc))
        return carry

    lax.fori_loop(0, n, body, 0)


def _sigmoid(x):
    return 0.5 * jnp.tanh(0.5 * x) + 0.5


def _rope(x, cos, sin_up, sin_dn):
    return x * cos + pltpu.roll(x, LANES - 16, 1) * sin_up + pltpu.roll(x, 16, 1) * sin_dn


def _dot(a, b):
    return jnp.dot(a, b, preferred_element_type=F32)


def _dot_nt(a, b):
    return lax.dot_general(a, b, (((1,), (1,)), ((), ())), preferred_element_type=F32)


def _ada_kernel(c_ref, w_ref, b_ref, o_ref):
    c = c_ref[...]
    s = c * _sigmoid(c)
    o_ref[...] = jnp.dot(s, w_ref[...], preferred_element_type=F32,
                         precision=lax.Precision.HIGHEST) + b_ref[...]


def _ada(cc, w_ada, b_ada):
    depth, d, n = w_ada.shape
    rows = cc.shape[0]
    tn = 1024
    return pl.pallas_call(
        _ada_kernel,
        grid=(depth, n // tn),
        in_specs=[
            pl.BlockSpec((rows, d), lambda l, j: (0, 0)),
            pl.BlockSpec((None, d, tn), lambda l, j: (l, 0, j)),
            pl.BlockSpec((None, 1, tn), lambda l, j: (l, 0, j)),
        ],
        out_specs=pl.BlockSpec((None, rows, tn), lambda l, j: (l, 0, j)),
        out_shape=jax.ShapeDtypeStruct((depth, rows, n), F32),
        compiler_params=_cparams("parallel", "parallel"),
        name="ada_mod",
    )(cc, w_ada, b_ada.reshape(depth, 1, n))


def _norm_mod_store(x_ref, g_ref, sh_ref, sc_ref, h_ref, rc):
    gs = g_ref[...] * (1.0 + sc_ref[...])
    sh = sh_ref[...]

    def chunk(r0):
        x = x_ref[pl.ds(r0, rc), :]
        ms = jnp.mean(x * x, axis=-1, keepdims=True)
        h_ref[pl.ds(r0, rc), :] = (x * lax.rsqrt(ms + EPS) * gs + sh).astype(BF16)

    _for_chunks(x_ref.shape[0], rc, chunk)


def _in_proj_kernel(x_ref, g_ref, sh_ref, sc_ref, w_ref, *rest, n_gate, col0, rc):
    o_ref, h_ref = rest[-2:]
    j = pl.program_id(1)

    @pl.when(j == 0)
    def _():
        _norm_mod_store(x_ref, g_ref, sh_ref, sc_ref, h_ref, rc)

    acc = _dot(h_ref[...], w_ref[...])
    o_ref[...] = jnp.where(col0 + j < n_gate, _sigmoid(acc), acc).astype(BF16)


def _ffn_in_kernel(h_ref, w1_ref, w2_ref, o_ref):
    h = h_ref[...]
    u1 = _dot(h, w1_ref[...])
    u2 = _dot(h, w2_ref[...])
    o_ref[...] = (u1 * _sigmoid(u1) * u2).astype(BF16)


def _mm_res_kernel(a_ref, w_ref, x_ref, gt_ref, o_ref):
    o_ref[...] = x_ref[...] + gt_ref[...] * _dot(a_ref[...], w_ref[...])


def _out_proj_kernel(a_ref, w_ref, x_ref, gt_ref, g_ref, sh_ref, sc_ref, *rest):
    o_ref, h_ref = rest[-2:]
    gs = g_ref[...] * (1.0 + sc_ref[...])
    rows = a_ref.shape[0] // 2
    for r in (0, rows):
        x_new = x_ref[r:r + rows, :] + gt_ref[...] * _dot(a_ref[r:r + rows, :], w_ref[...])
        o_ref[r:r + rows, :] = x_new
        ms = jnp.mean(x_new * x_new, axis=-1, keepdims=True)
        h_ref[r:r + rows, :] = (x_new * lax.rsqrt(ms + EPS) * gs + sh_ref[...]).astype(BF16)


def _merge_kernel(oc_ref, om_ref, og_ref, wc_ref, wm_ref, wg_ref, g0_ref, g1_ref, g2_ref, y_ref):
    y = g0_ref[...].astype(F32) * _dot(oc_ref[...], wc_ref[...])
    y += g1_ref[...].astype(F32) * _dot(om_ref[...], wm_ref[...])
    y += g2_ref[...].astype(F32) * _dot(og_ref[...], wg_ref[...])
    y_ref[...] = y.astype(BF16)


def _rms_bf16(a, g):
    ms = jnp.mean(a * a, axis=-1, keepdims=True)
    return (a * lax.rsqrt(ms + EPS) * g).astype(BF16)


def _qup_kernel(pq_ref, g_ref, wq_ref, cos_ref, su_ref, sd_ref, q_ref, qg_ref, *, rc, s_mla, s_gqa):
    g = g_ref[...]

    def chunk(r0):
        rows = pl.ds(r0, rc)
        n = _rms_bf16(pq_ref[rows, 0:MLA_Q_RANK].astype(F32), g)
        q = _dot(n, wq_ref[...])
        cos = cos_ref[rows, :]
        su = su_ref[rows, :]
        sd = sd_ref[rows, :]
        for h in range(MLA_HEADS):
            c0 = h * MLA_PAD
            q_ref[rows, c0:c0 + LANES] = (q[:, c0:c0 + LANES] * s_mla).astype(BF16)
            r = _rope(q[:, c0 + LANES:c0 + 2 * LANES], cos, su, sd)
            q_ref[rows, c0 + LANES:c0 + 2 * LANES] = (r * s_mla).astype(BF16)
        for t in range(GQA_HEADS * GQA_HEAD_DIM // LANES):
            gq = pq_ref[rows, MLA_Q_RANK + t * LANES:MLA_Q_RANK + (t + 1) * LANES].astype(F32)
            qg_ref[rows, t * LANES:(t + 1) * LANES] = (_rope(gq, cos, su, sd) * s_gqa).astype(BF16)

    _for_chunks(pq_ref.shape[0], rc, chunk)


def _kvup_kernel(ckv_ref, kr_ref, g_ref, wk_ref, wv_ref, cos_ref, su_ref, sd_ref,
                 k_ref, krope_ref, v_ref, kg_ref, vg_ref, *, rc):
    g = g_ref[...]

    def chunk(r0):
        rows = pl.ds(r0, rc)
        n = _rms_bf16(ckv_ref[rows, :].astype(F32), g)
        k_ref[rows, :] = _dot(n, wk_ref[...]).astype(BF16)
        v_ref[rows, :] = _dot(n, wv_ref[...]).astype(BF16)
        cos = cos_ref[rows, :]
        su = su_ref[rows, :]
        sd = sd_ref[rows, :]
        krope_ref[rows, :] = _rope(kr_ref[rows, 0:LANES].astype(F32), cos, su, sd).astype(BF16)
        low_half = lax.broadcasted_iota(jnp.int32, (rc, LANES), 1) < GQA_HEAD_DIM
        gk = _rope(kr_ref[rows, LANES:2 * LANES].astype(F32), cos, su, sd)
        gk_sw = pltpu.roll(gk, GQA_HEAD_DIM, 1)
        kg_ref[rows, 0:LANES] = jnp.where(low_half, gk, gk_sw).astype(BF16)
        kg_ref[rows, LANES:2 * LANES] = jnp.where(low_half, gk_sw, gk).astype(BF16)
        gv = kr_ref[rows, 2 * LANES:3 * LANES].astype(F32)
        gv_sw = pltpu.roll(gv, GQA_HEAD_DIM, 1)
        vg_ref[rows, 0:LANES] = jnp.where(low_half, gv, gv_sw).astype(BF16)
        vg_ref[rows, LANES:2 * LANES] = jnp.where(low_half, gv_sw, gv).astype(BF16)

    _for_chunks(ckv_ref.shape[0], rc, chunk)


def _mla_attn_kernel(q_ref, k_ref, krope_ref, v_ref, *rest, tk, rg, unroll):
    n_buf = 3 * unroll
    o_ref = rest[-n_buf - 3]
    bufs = rest[-n_buf - 2:-2]
    s_buf, p_buf, a_buf = bufs[:unroll], bufs[unroll:2 * unroll], bufs[2 * unroll:]
    m_ref, acc_ref = rest[-2:]
    tq = q_ref.shape[0]
    head = k_ref.shape[0] % tk
    has_head = 1 if head else 0
    n = k_ref.shape[0] // tk + has_head

    def start_of(c):
        return head + (c - has_head) * tk

    def size_of(c):
        return head if (has_head and c == 0) else tk

    def qk(slot, start, size):
        k = jnp.concatenate([k_ref[pl.ds(start, size), :], krope_ref[pl.ds(start, size), :]], axis=1)
        s_buf[slot][:, 0:size] = _dot_nt(q_ref[...], k)

    def sm(slot, size):
        for r in range(0, tq, rg):
            s = s_buf[slot][r:r + rg, 0:size]
            m_old = m_ref[r:r + rg, :]
            m_new = jnp.maximum(m_old, jnp.max(s, axis=-1, keepdims=True))
            p_buf[slot][r:r + rg, 0:size] = jnp.exp2(s - pltpu.repeat(m_new, size // LANES, 1)).astype(BF16)
            a_buf[slot][r:r + rg, :] = jnp.exp2(m_old - m_new)
            m_ref[r:r + rg, :] = m_new

    def pv(slot, start, size):
        alpha = pltpu.repeat(a_buf[slot][...], acc_ref.shape[1] // LANES, 1)
        v = jnp.concatenate([v_ref[pl.ds(start, size), :], jnp.ones((size, LANES), BF16)], axis=1)
        acc_ref[...] = alpha * acc_ref[...] + _dot(p_buf[slot][:, 0:size], v)

    def static_start(c):
        return 0 if (has_head and c == 0) else start_of(c)

    def static_tick(t):
        if 0 <= t < n:
            qk(t % unroll, static_start(t), size_of(t))
        if 0 <= t - 2 < n:
            pv((t - 2) % unroll, static_start(t - 2), size_of(t - 2))
        if 0 <= t - 1 < n:
            sm((t - 1) % unroll, size_of(t - 1))

    m_ref[...] = jnp.full(m_ref.shape, NEG_INF, F32)
    acc_ref[...] = jnp.zeros(acc_ref.shape, F32)

    first_steady = 2 + has_head
    trips = max(n - first_steady, 0) // unroll
    for t in range(first_steady):
        static_tick(t)
    if trips:
        def trip(j, carry):
            for u in range(unroll):
                t = first_steady + unroll * j + u
                qk((first_steady + u) % unroll, pl.multiple_of(start_of(t), 2 * LANES), tk)
                pv((first_steady + u - 2) % unroll, pl.multiple_of(start_of(t - 2), 2 * LANES), tk)
                sm((first_steady + u - 1) % unroll, tk)
            return carry

        lax.fori_loop(0, trips, trip, 0)
    for t in range(first_steady + unroll * trips, n + 2):
        static_tick(t)
    acc = acc_ref[...]
    o_ref[...] = (acc[:, :MLA_V] / acc[:, MLA_V:MLA_V + 1]).astype(BF16)


def _gqa_attn_kernel(q_ref, kc_ref, vc_ref, sink_ref, *rest, seq, n_sub, latent):
    if latent:
        kl_ref, vl_ref = rest[0], rest[1]
    o_ref, s0, s1, p0, p1, e0, e1 = rest[-7:]
    s_buf, p_buf, e_buf = (s0, s1), (p0, p1), (e0, e1)
    span = 3 * WINDOW
    n_c = kc_ref.shape[0]
    rep = GQA_HEADS // GQA_KV_HEADS
    lane = lax.broadcasted_iota(jnp.int32, (1, LANES), 1)
    low_half = lane < GQA_HEAD_DIM
    half_masks = (jnp.where(low_half, 1.0, 0.0).astype(BF16), jnp.where(low_half, 0.0, 1.0).astype(BF16))
    sinks = sink_ref[...] * LOG2_E
    sinks = [sinks[:, h:h + 1] for h in range(GQA_HEADS)]

    def row0(r):
        return r * WINDOW if isinstance(r, int) else pl.multiple_of(r * WINDOW, WINDOW)

    def win_start(r):
        if isinstance(r, int):
            return min(max(r * WINDOW - WINDOW, 0), seq - span)
        return pl.multiple_of(jnp.clip(r * WINDOW - WINDOW, 0, seq - span), WINDOW)

    def qk(slot, r):
        rows = pl.ds(row0(r), WINDOW)
        for h in range(GQA_HEADS):
            g = h // rep
            q = q_ref[rows, (h // 2) * LANES:(h // 2 + 1) * LANES] * half_masks[h % 2]
            s_buf[slot][h, :, 0:n_c] = _dot_nt(q, kc_ref[:, g * LANES:(g + 1) * LANES])
            if latent:
                s_buf[slot][h, :, n_c:n_c + span] = _dot_nt(
                    q, kl_ref[pl.ds(win_start(r), span), g * LANES:(g + 1) * LANES])

    def sm(slot, r):
        if latent:
            qpos = r * WINDOW + lax.broadcasted_iota(jnp.int32, (WINDOW, span), 0)
            kpos = win_start(r) + lax.broadcasted_iota(jnp.int32, (WINDOW, span), 1)
            bias = jnp.where(jnp.abs(qpos - kpos) <= WINDOW, 0.0, NEG_INF)
        for h in range(GQA_HEADS):
            sc = s_buf[slot][h, :, 0:n_c]
            m = jnp.maximum(jnp.max(sc, axis=-1, keepdims=True), sinks[h])
            if latent:
                sw = s_buf[slot][h, :, n_c:n_c + span] + bias
                m = jnp.maximum(m, jnp.max(sw, axis=-1, keepdims=True))
                p_buf[slot][h, :, n_c:n_c + span] = jnp.exp2(sw - m).astype(BF16)
            p_buf[slot][h, :, 0:n_c] = jnp.exp2(sc - m).astype(BF16)
            e_buf[slot][h] = jnp.broadcast_to(jnp.exp2(sinks[h] - m), (WINDOW, LANES))

    def pv(slot, r):
        outs = []
        for h in range(GQA_HEADS):
            g = h // rep
            vc = jnp.concatenate([vc_ref[:, g * LANES:(g + 1) * LANES], jnp.ones((n_c, LANES), BF16)], axis=1)
            o = _dot(p_buf[slot][h, :, 0:n_c], vc)
            if latent:
                vw = jnp.concatenate([vl_ref[pl.ds(win_start(r), span), g * LANES:(g + 1) * LANES],
                                      jnp.ones((span, LANES), BF16)], axis=1)
                o = o + _dot(p_buf[slot][h, :, n_c:n_c + span], vw)
            outs.append(o[:, :LANES] / (o[:, LANES:] + e_buf[slot][h]))
        rows = pl.ds(row0(r), WINDOW)
        for t in range(GQA_HEADS // 2):
            o_ref[rows, t * LANES:(t + 1) * LANES] = jnp.where(low_half, outs[2 * t], outs[2 * t + 1]).astype(BF16)

    def tick(t, par):
        qk(par, t)
        pv(par, t - 2)
        sm(1 - par, t - 1)

    def static_tick(t):
        if 0 <= t < n_sub:
            qk(t % 2, t)
        if 0 <= t - 2 < n_sub:
            pv(t % 2, t - 2)
        if 0 <= t - 1 < n_sub:
            sm((t - 1) % 2, t - 1)

    trips = max(n_sub - 2, 0) // 2
    for t in range(2):
        static_tick(t)
    if trips:
        def trip(j, carry):
            for u in range(2):
                tick(2 + 2 * j + u, u)
            return carry

        lax.fori_loop(0, trips, trip, 0)
    for t in range(2 + 2 * trips, n_sub + 2):
        static_tick(t)


def _conv_kernel(a_ref, b_ref, w_ref, cb_ref, lg_ref, lb_ref, o_ref, u_ref, *, seq, rc):
    halo = 16
    u_ref[0:halo, :] = jnp.zeros((halo, CONV_CH), F32)
    u_ref[halo + seq:2 * halo + seq, :] = jnp.zeros((halo, CONV_CH), F32)

    def fill(r0):
        a = a_ref[pl.ds(r0, 256), :].astype(F32)
        b = b_ref[pl.ds(r0, 256), :].astype(F32)
        u_ref[pl.ds(halo + r0, 256), :] = a * _sigmoid(b)

    _for_chunks(seq, 256, fill)

    cb = cb_ref[...]
    lg = lg_ref[...]
    lb = lb_ref[...]

    half = CONV_CH // 2
    first_tap = halo - CONV_WIDTH // 2

    def conv(r0):
        parts = []
        for c0 in (0, half):
            acc = jnp.zeros((rc, half), F32) + cb[:, c0:c0 + half]
            for s in range(8):
                part = None
                for a in range(4):
                    k = 8 * a + s - first_tap
                    if 0 <= k < CONV_WIDTH:
                        rows = u_ref[pl.ds(pl.multiple_of(r0 + 8 * a, 8), rc + 8), c0:c0 + half]
                        term = w_ref[k:k + 1, c0:c0 + half] * rows
                        part = term if part is None else part + term
                acc = acc + part[s:s + rc]
            parts.append(acc)
        acc = jnp.concatenate(parts, axis=-1)
        mu = jnp.mean(acc, axis=-1, keepdims=True)
        xc = acc - mu
        var = jnp.mean(xc * xc, axis=-1, keepdims=True)
        y = xc * lax.rsqrt(var + EPS) * lg + lb
        o_ref[pl.ds(r0, rc), :] = (y * _sigmoid(y)).astype(BF16)

    _for_chunks(seq, rc, conv)


def _final_norm_kernel(x_ref, g_ref, o_ref, *, rc):
    g = g_ref[...]

    def chunk(r0):
        x = x_ref[pl.ds(r0, rc), :]
        ms = jnp.mean(x * x, axis=-1, keepdims=True)
        o_ref[pl.ds(r0, rc), :] = x * lax.rsqrt(ms + EPS) * g

    _for_chunks(x_ref.shape[0], rc, chunk)


def _rope_tables(length, extra):
    rows = length // GRID_W
    row = jnp.repeat(jnp.arange(rows), GRID_W).astype(F32)
    col = jnp.tile(jnp.arange(GRID_W), rows).astype(F32)
    n_freq = ROPE_DIM // 4
    inv_freq = ROPE_BASE ** (-jnp.arange(n_freq, dtype=F32) / n_freq)
    a_row = row[:, None] * inv_freq[None, :]
    a_col = col[:, None] * inv_freq[None, :]
    ang = jnp.concatenate([a_row, a_row, a_col, a_col], axis=-1)
    cos = jnp.concatenate([jnp.cos(ang), jnp.ones((extra, ROPE_DIM), F32)], axis=0)
    sin = jnp.concatenate([jnp.sin(ang), jnp.zeros((extra, ROPE_DIM), F32)], axis=0)
    cos = jnp.tile(cos, (1, LANES // ROPE_DIM))
    sin = jnp.tile(sin, (1, LANES // ROPE_DIM))
    first = (jnp.arange(LANES) // 16) % 2 == 0
    sin_up = jnp.where(first[None, :], -sin, 0.0)
    sin_dn = jnp.where(first[None, :], 0.0, sin)
    return cos, sin_up, sin_dn


def _pick_tile(*sizes):
    for t in (1024, 512, 256):
        if all(s % t == 0 for s in sizes):
            return t
    raise ValueError(f"unsupported row counts {sizes}")


def kernel(x, c, ctx, c_ctx, w_ada, b_ada, g_mix, w_in, conv_w, conv_b, conv_ln_g, conv_ln_b, w_conv_out, g_q_a, w_q_up, g_kv_a, w_kv_up, w_mla_out, gqa_sink, w_gqa_out, w_out, g_ffn, w_ffn_in, w_ffn_out, g_final):
    bsz, seq, d = x.shape
    n_ctx = ctx.shape[1]
    depth = w_ada.shape[0]
    d_ff = w_ffn_out.shape[1]
    m_lat = bsz * seq
    m_ctx = bsz * n_ctx
    m_all = m_lat + m_ctx
    assert seq % GRID_W == 0 and n_ctx == 2 * WINDOW and seq % (4 * WINDOW) == 0
    assert bsz + 1 <= 16 and w_in.shape[2] == COL_GATE + N_BRANCH * d

    tm = _pick_tile(seq, m_ctx)
    ts = min(tm, 512)
    tn = 512
    rc = 256
    n_lat_tiles = m_lat // tm
    tiles_per_batch = seq // tm
    p_base = N_BRANCH * d
    p_cols = p_base + P_SMALL
    tn_in = 1536
    assert p_base % tn_in == 0 and p_cols % tn_in == 0 and p_cols - tn_in <= p_base + P_CKV
    ctx_blk0 = m_lat // n_ctx

    def mod_row(i, tile):
        return jnp.minimum(i // (seq // tile), bsz)

    def cols(a, n):
        return w_in[:, :, a:a + n].astype(BF16)

    def zeros(n):
        return jnp.zeros((depth, d, n), BF16)

    w_in_r = jnp.concatenate([
        cols(COL_GATE, N_BRANCH * d),
        cols(COL_MLA_Q, MLA_Q_RANK), cols(COL_GQA_Q, GQA_HEADS * GQA_HEAD_DIM),
        cols(COL_CONV, CONV_CH), cols(COL_CONV + CONV_CH, CONV_CH),
        cols(COL_MLA_KV, MLA_KV_RANK), cols(COL_MLA_KR, MLA_ROPE), zeros(LANES - MLA_ROPE),
        cols(COL_GQA_K, GQA_KV_HEADS * GQA_HEAD_DIM), cols(COL_GQA_V, GQA_KV_HEADS * GQA_HEAD_DIM),
        zeros(P_SMALL - P_GQA_V - GQA_KV_HEADS * GQA_HEAD_DIM)], axis=-1)
    wq = w_q_up.reshape(depth, MLA_Q_RANK, MLA_HEADS, MLA_NOPE + MLA_ROPE).astype(BF16)
    wq = jnp.pad(wq, ((0, 0), (0, 0), (0, 0), (0, MLA_PAD - MLA_NOPE - MLA_ROPE)))
    wq = wq.reshape(depth, MLA_Q_RANK, MLA_HEADS * MLA_PAD)
    wkv = w_kv_up.reshape(depth, MLA_KV_RANK, MLA_HEADS, MLA_NOPE + MLA_V).astype(BF16)
    wk = wkv[..., :MLA_NOPE].reshape(depth, MLA_KV_RANK, MLA_HEADS * MLA_NOPE)
    wv = wkv[..., MLA_NOPE:].reshape(depth, MLA_KV_RANK, MLA_HEADS * MLA_V)
    w_conv_out_b = w_conv_out.astype(BF16)
    w_mla_out_b = w_mla_out.astype(BF16)
    w_gqa_out_b = w_gqa_out.astype(BF16)
    w_out_b = w_out.astype(BF16)
    w_ffn_in_b = w_ffn_in.astype(BF16)
    w_ffn_out_b = w_ffn_out.astype(BF16)
    cos, sin_up, sin_dn = _rope_tables(seq, ts)

    cc = jnp.zeros((16, d), F32).at[:bsz].set(c).at[bsz].set(c_ctx)
    mod = _ada(cc, w_ada, b_ada).reshape(depth, 16, 1, 6 * d)

    x_lat = x.reshape(m_lat, d)
    x_ctx = ctx.reshape(m_ctx, d)
    xa = None

    def vec(a):
        return a.reshape(depth, 1, a.shape[-1])

    g_mix3, g_ffn3, g_q3, g_kv3 = vec(g_mix), vec(g_ffn), vec(g_q_a), vec(g_kv_a)
    conv_b3, ln_g3, ln_b3, sink3 = vec(conv_b), vec(conv_ln_g), vec(conv_ln_b), vec(gqa_sink)

    def mod_spec(l, part, tile, width, with_j, tile0=0):
        nb = d // width
        if with_j:
            return pl.BlockSpec((None, None, 1, width),
                                lambda i, j: (l, mod_row(tile0 + i, tile), 0, part * nb + j))
        return pl.BlockSpec((None, None, 1, width), lambda i, j: (l, mod_row(tile0 + i, tile), 0, part))

    def table_specs(tile):
        per_batch = seq // tile
        n_lat = m_lat // tile
        return [pl.BlockSpec((tile, LANES), lambda i: (jnp.where(i < n_lat, i % per_batch, per_batch), 0))] * 3

    for l in range(depth):
        last = l == depth - 1
        m_rows = m_lat if last else m_all

        lat_src, lat_tile0 = (x_lat, 0) if l == 0 else (xa, 0)
        ctx_src, ctx_tile0 = (x_ctx, 0) if l == 0 else (xa, n_lat_tiles)

        def in_proj_call(src, src_tile0, row_tile0, n_row_tiles, col_tile0, n_col_tiles, prev):
            in_specs = [
                pl.BlockSpec((tm, d), lambda i, j: (src_tile0 + i, 0)),
                pl.BlockSpec((None, 1, d), lambda i, j: (l, 0, 0)),
                mod_spec(l, 0, tm, d, False, row_tile0),
                mod_spec(l, 1, tm, d, False, row_tile0),
                pl.BlockSpec((None, d, tn_in), lambda i, j: (l, 0, col_tile0 + j)),
            ]
            args = [src, g_mix3, mod, mod, w_in_r]
            aliases = {}
            if prev is not None:
                in_specs.append(pl.BlockSpec(memory_space=pl.ANY))
                args.append(prev)
                aliases = {5: 0}
            return pl.pallas_call(
                functools.partial(_in_proj_kernel, n_gate=p_base // tn_in, col0=col_tile0, rc=rc),
                grid=(n_row_tiles, n_col_tiles),
                in_specs=in_specs,
                out_specs=pl.BlockSpec((tm, tn_in), lambda i, j: (row_tile0 + i, col_tile0 + j)),
                out_shape=jax.ShapeDtypeStruct((m_all, p_cols), BF16),
                scratch_shapes=[pltpu.VMEM((tm, d), BF16)],
                input_output_aliases=aliases,
                compiler_params=_cparams("parallel", "arbitrary"),
                name="in_proj" if prev is None else "in_proj_ctx",
            )(*args)

        n_col_tiles = p_cols // tn_in
        p = in_proj_call(lat_src, lat_tile0, 0, n_lat_tiles, 0, n_col_tiles, None)
        if last:
            p = in_proj_call(ctx_src, ctx_tile0, n_lat_tiles, m_ctx // tm, n_col_tiles - 1, 1, p)
        else:
            p = in_proj_call(ctx_src, ctx_tile0, n_lat_tiles, m_ctx // tm, 0, n_col_tiles, p)

        q_mla, q_gqa = pl.pallas_call(
            functools.partial(_qup_kernel, rc=rc, s_mla=float(LOG2_E * (MLA_NOPE + MLA_ROPE) ** -0.5),
                              s_gqa=float(LOG2_E * GQA_HEAD_DIM ** -0.5)),
            grid=(m_rows // ts,),
            in_specs=[
                pl.BlockSpec((ts, 1024), lambda i: (i, (p_base + P_MLA_Q) // 1024)),
                pl.BlockSpec((None, 1, MLA_Q_RANK), lambda i: (l, 0, 0)),
                pl.BlockSpec((None, MLA_Q_RANK, MLA_HEADS * MLA_PAD), lambda i: (l, 0, 0)),
            ] + table_specs(ts),
            out_specs=[pl.BlockSpec((ts, MLA_HEADS * MLA_PAD), lambda i: (i, 0)),
                       pl.BlockSpec((ts, GQA_HEADS * GQA_HEAD_DIM), lambda i: (i, 0))],
            out_shape=[jax.ShapeDtypeStruct((m_all, MLA_HEADS * MLA_PAD), BF16),
                       jax.ShapeDtypeStruct((m_all, GQA_HEADS * GQA_HEAD_DIM), BF16)],
            compiler_params=_cparams("parallel"),
            name="q_up",
        )(p, g_q3, wq, cos, sin_up, sin_dn)

        tkv = n_ctx
        kv_per_batch = seq // tkv
        kv_lat_tiles = m_lat // tkv

        def kv_block(i):
            lat = (i // kv_per_batch) * (kv_per_batch + 1) + 1 + i % kv_per_batch
            return jnp.where(i < kv_lat_tiles, lat, (i - kv_lat_tiles) * (kv_per_batch + 1))

        k_mla, k_rope, v_mla, k_gqa, v_gqa = pl.pallas_call(
            functools.partial(_kvup_kernel, rc=tkv),
            grid=(m_all // tkv,),
            in_specs=[
                pl.BlockSpec((tkv, MLA_KV_RANK), lambda i: (i, (p_base + P_CKV) // MLA_KV_RANK)),
                pl.BlockSpec((tkv, 3 * LANES), lambda i: (i, (p_base + P_KR) // (3 * LANES))),
                pl.BlockSpec((None, 1, MLA_KV_RANK), lambda i: (l, 0, 0)),
                pl.BlockSpec((None, MLA_KV_RANK, MLA_HEADS * MLA_NOPE), lambda i: (l, 0, 0)),
                pl.BlockSpec((None, MLA_KV_RANK, MLA_HEADS * MLA_V), lambda i: (l, 0, 0)),
            ] + table_specs(tkv),
            out_specs=[pl.BlockSpec((tkv, MLA_HEADS * MLA_NOPE), lambda i: (kv_block(i), 0)),
                       pl.BlockSpec((tkv, LANES), lambda i: (kv_block(i), 0)),
                       pl.BlockSpec((tkv, MLA_HEADS * MLA_V), lambda i: (kv_block(i), 0)),
                       pl.BlockSpec((tkv, 2 * LANES), lambda i: (i, 0)),
                       pl.BlockSpec((tkv, 2 * LANES), lambda i: (i, 0))],
            out_shape=[jax.ShapeDtypeStruct((m_all, MLA_HEADS * MLA_NOPE), BF16),
                       jax.ShapeDtypeStruct((m_all, LANES), BF16),
                       jax.ShapeDtypeStruct((m_all, MLA_HEADS * MLA_V), BF16),
                       jax.ShapeDtypeStruct((m_all, 2 * LANES), BF16),
                       jax.ShapeDtypeStruct((m_all, 2 * LANES), BF16)],
            compiler_params=_cparams("parallel"),
            name="kv_up",
        )(p, p, g_kv3, wk, wv, cos, sin_up, sin_dn)

        tq = min(MLA_TQ, seq)
        nq = seq // tq
        def mla_scratch(rows, tk):
            return ([pltpu.VMEM((rows, tk), F32)] * MLA_UNROLL + [pltpu.VMEM((rows, tk), BF16)] * MLA_UNROLL
                    + [pltpu.VMEM((rows, LANES), F32)] * (MLA_UNROLL + 1) + [pltpu.VMEM((rows, MLA_PAD), F32)])

        o_mla = pl.pallas_call(
            functools.partial(_mla_attn_kernel, tk=MLA_TK, rg=MLA_RG, unroll=MLA_UNROLL),
            grid=(bsz, MLA_HEADS, nq),
            scratch_shapes=mla_scratch(tq, MLA_TK),
            in_specs=[
                pl.BlockSpec((tq, MLA_PAD), lambda b, h, i: (b * nq + i, h)),
                pl.BlockSpec((seq + n_ctx, MLA_NOPE), lambda b, h, i: (b, h)),
                pl.BlockSpec((seq + n_ctx, LANES), lambda b, h, i: (b, 0)),
                pl.BlockSpec((seq + n_ctx, MLA_V), lambda b, h, i: (b, h)),
            ],
            out_specs=pl.BlockSpec((tq, MLA_V), lambda b, h, i: (b * nq + i, h)),
            out_shape=jax.ShapeDtypeStruct((m_all, MLA_HEADS * MLA_V), BF16),
            compiler_params=_cparams("parallel", "parallel", "arbitrary"),
            name="mla_attn",
        )(q_mla, k_mla, k_rope, v_mla)
        o_mla = o_mla if last else pl.pallas_call(
            functools.partial(_mla_attn_kernel, tk=n_ctx, rg=MLA_RG, unroll=MLA_UNROLL),
            grid=(bsz, MLA_HEADS),
            scratch_shapes=mla_scratch(n_ctx, n_ctx),
            in_specs=[
                pl.BlockSpec((n_ctx, MLA_PAD), lambda b, h: (ctx_blk0 + b, h)),
                pl.BlockSpec((n_ctx, MLA_NOPE), lambda b, h: (b * (kv_per_batch + 1), h)),
                pl.BlockSpec((n_ctx, LANES), lambda b, h: (b * (kv_per_batch + 1), 0)),
                pl.BlockSpec((n_ctx, MLA_V), lambda b, h: (b * (kv_per_batch + 1), h)),
                pl.BlockSpec(memory_space=pl.ANY),
            ],
            out_specs=pl.BlockSpec((n_ctx, MLA_V), lambda b, h: (ctx_blk0 + b, h)),
            out_shape=jax.ShapeDtypeStruct((m_all, MLA_HEADS * MLA_V), BF16),
            input_output_aliases={4: 0},
            compiler_params=_cparams("parallel", "parallel"),
            name="mla_attn_ctx",
        )(q_mla, k_mla, k_rope, v_mla, o_mla)

        def gqa_scratch(cols):
            return ([pltpu.VMEM((GQA_HEADS, WINDOW, cols), F32)] * 2
                    + [pltpu.VMEM((GQA_HEADS, WINDOW, cols), BF16)] * 2
                    + [pltpu.VMEM((GQA_HEADS, WINDOW, LANES), F32)] * 2)

        o_gqa = pl.pallas_call(
            functools.partial(_gqa_attn_kernel, seq=seq, n_sub=seq // WINDOW, latent=True),
            grid=(bsz,),
            in_specs=[
                pl.BlockSpec((seq, GQA_HEADS * GQA_HEAD_DIM), lambda b: (b, 0)),
                pl.BlockSpec((n_ctx, 2 * LANES), lambda b: (ctx_blk0 + b, 0)),
                pl.BlockSpec((n_ctx, 2 * LANES), lambda b: (ctx_blk0 + b, 0)),
                pl.BlockSpec((None, 1, GQA_HEADS), lambda b: (l, 0, 0)),
                pl.BlockSpec((seq, 2 * LANES), lambda b: (b, 0)),
                pl.BlockSpec((seq, 2 * LANES), lambda b: (b, 0)),
            ],
            out_specs=pl.BlockSpec((seq, GQA_HEADS * GQA_HEAD_DIM), lambda b: (b, 0)),
            out_shape=jax.ShapeDtypeStruct((m_all, GQA_HEADS * GQA_HEAD_DIM), BF16),
            scratch_shapes=gqa_scratch(n_ctx + 3 * WINDOW),
            compiler_params=_cparams("parallel"),
            name="gqa_attn",
        )(q_gqa, k_gqa, v_gqa, sink3, k_gqa, v_gqa)
        o_gqa = o_gqa if last else pl.pallas_call(
            functools.partial(_gqa_attn_kernel, seq=n_ctx, n_sub=n_ctx // WINDOW, latent=False),
            grid=(bsz,),
            scratch_shapes=gqa_scratch(n_ctx),
            in_specs=[
                pl.BlockSpec((n_ctx, GQA_HEADS * GQA_HEAD_DIM), lambda b: (ctx_blk0 + b, 0)),
                pl.BlockSpec((n_ctx, 2 * LANES), lambda b: (ctx_blk0 + b, 0)),
                pl.BlockSpec((n_ctx, 2 * LANES), lambda b: (ctx_blk0 + b, 0)),
                pl.BlockSpec((None, 1, GQA_HEADS), lambda b: (l, 0, 0)),
                pl.BlockSpec(memory_space=pl.ANY),
            ],
            out_specs=pl.BlockSpec((n_ctx, GQA_HEADS * GQA_HEAD_DIM), lambda b: (ctx_blk0 + b, 0)),
            out_shape=jax.ShapeDtypeStruct((m_all, GQA_HEADS * GQA_HEAD_DIM), BF16),
            input_output_aliases={4: 0},
            compiler_params=_cparams("parallel"),
            name="gqa_attn_ctx",
        )(q_gqa, k_gqa, v_gqa, sink3, o_gqa)

        def conv_call(length, blk0, prev):
            crc = 64
            in_specs = [
                pl.BlockSpec((length, CONV_CH), lambda b: (blk0 + b, (p_base + P_CONV_A) // CONV_CH)),
                pl.BlockSpec((length, CONV_CH), lambda b: (blk0 + b, (p_base + P_CONV_B) // CONV_CH)),
                pl.BlockSpec((None, CONV_WIDTH, CONV_CH), lambda b: (l, 0, 0)),
                pl.BlockSpec((None, 1, CONV_CH), lambda b: (l, 0, 0)),
                pl.BlockSpec((None, 1, CONV_CH), lambda b: (l, 0, 0)),
                pl.BlockSpec((None, 1, CONV_CH), lambda b: (l, 0, 0)),
            ]
            args = [p, p, conv_w, conv_b3, ln_g3, ln_b3]
            aliases = {}
            if prev is not None:
                in_specs.append(pl.BlockSpec(memory_space=pl.ANY))
                args.append(prev)
                aliases = {6: 0}

            def body(a_ref, b_ref, w_ref, cb_ref, lg_ref, lb_ref, *rest):
                o_ref, u_ref = rest[-2:]
                _conv_kernel(a_ref, b_ref, w_ref, cb_ref, lg_ref, lb_ref, o_ref, u_ref, seq=length, rc=crc)

            return pl.pallas_call(
                body,
                grid=(bsz,),
                in_specs=in_specs,
                out_specs=pl.BlockSpec((length, CONV_CH), lambda b: (blk0 + b, 0)),
                out_shape=jax.ShapeDtypeStruct((m_all, CONV_CH), BF16),
                scratch_shapes=[pltpu.VMEM((length + 32, CONV_CH), F32)],
                input_output_aliases=aliases,
                compiler_params=_cparams("parallel"),
                name="conv_branch" if prev is None else "conv_branch_ctx",
            )(*args)

        o_conv = conv_call(seq, 0, None)
        if not last:
            o_conv = conv_call(n_ctx, ctx_blk0, o_conv)

        y = pl.pallas_call(
            _merge_kernel,
            grid=(m_rows // ts, 1),
            in_specs=[
                pl.BlockSpec((ts, CONV_CH), lambda i, j: (i, 0)),
                pl.BlockSpec((ts, MLA_HEADS * MLA_V), lambda i, j: (i, 0)),
                pl.BlockSpec((ts, GQA_HEADS * GQA_HEAD_DIM), lambda i, j: (i, 0)),
                pl.BlockSpec((None, CONV_CH, d), lambda i, j: (l, 0, 0)),
                pl.BlockSpec((None, MLA_HEADS * MLA_V, d), lambda i, j: (l, 0, 0)),
                pl.BlockSpec((None, GQA_HEADS * GQA_HEAD_DIM, d), lambda i, j: (l, 0, 0)),
                pl.BlockSpec((ts, d), lambda i, j: (i, 0)),
                pl.BlockSpec((ts, d), lambda i, j: (i, 1)),
                pl.BlockSpec((ts, d), lambda i, j: (i, 2)),
            ],
            out_specs=pl.BlockSpec((ts, d), lambda i, j: (i, 0)),
            out_shape=jax.ShapeDtypeStruct((m_all, d), BF16),
            compiler_params=_cparams("parallel", "parallel"),
            name="merge",
        )(o_conv, o_mla, o_gqa, w_conv_out_b, w_mla_out_b, w_gqa_out_b, p, p, p)

        def mm_res(a, w, xin, part, tile, width, name):
            kdim = a.shape[1]
            return pl.pallas_call(
                _mm_res_kernel,
                grid=(m_rows // tile, d // width),
                in_specs=[
                    pl.BlockSpec((tile, kdim), lambda i, j: (i, 0)),
                    pl.BlockSpec((None, kdim, width), lambda i, j: (l, 0, j)),
                    pl.BlockSpec((tile, width), lambda i, j: (i, j)),
                    mod_spec(l, part, tile, width, True),
                ],
                out_specs=pl.BlockSpec((tile, width), lambda i, j: (i, j)),
                out_shape=jax.ShapeDtypeStruct((m_all, d), F32),
                compiler_params=_cparams("parallel", "parallel"),
                name=name,
            )(a, w, xin, mod)

        def out_proj_call(res_src, res_tile0, row_tile0, n_row_tiles, prev):
            in_specs = [
                pl.BlockSpec((ts, d), lambda i, j: (row_tile0 + i, 0)),
                pl.BlockSpec((None, d, d), lambda i, j: (l, 0, 0)),
                pl.BlockSpec((ts, d), lambda i, j: (res_tile0 + i, 0)),
                mod_spec(l, 2, ts, d, False, row_tile0),
                pl.BlockSpec((None, 1, d), lambda i, j: (l, 0, 0)),
                mod_spec(l, 3, ts, d, False, row_tile0),
                mod_spec(l, 4, ts, d, False, row_tile0),
            ]
            args = [y, w_out_b, res_src, mod, g_ffn3, mod, mod]
            aliases = {}
            if prev is not None:
                in_specs += [pl.BlockSpec(memory_space=pl.ANY)] * 2
                args += list(prev)
                aliases = {7: 0, 8: 1}
            return pl.pallas_call(
                _out_proj_kernel,
                grid=(n_row_tiles, 1),
                in_specs=in_specs,
                out_specs=[pl.BlockSpec((ts, d), lambda i, j: (row_tile0 + i, 0))] * 2,
                out_shape=[jax.ShapeDtypeStruct((m_all, d), F32), jax.ShapeDtypeStruct((m_all, d), BF16)],
                input_output_aliases=aliases,
                compiler_params=_cparams("parallel", "parallel"),
                name="out_proj" if prev is None else "out_proj_ctx",
            )(*args)

        ts_per_tm = tm // ts
        xa, h_ffn = out_proj_call(lat_src, lat_tile0 * ts_per_tm, 0, m_lat // ts, None)
        if not last:
            xa, h_ffn = out_proj_call(ctx_src, ctx_tile0 * ts_per_tm, m_lat // ts, m_ctx // ts, (xa, h_ffn))

        nf = d_ff // tn
        act = pl.pallas_call(
            _ffn_in_kernel,
            grid=(m_rows // tm, nf),
            in_specs=[
                pl.BlockSpec((tm, d), lambda i, j: (i, 0)),
                pl.BlockSpec((None, d, tn), lambda i, j: (l, 0, j)),
                pl.BlockSpec((None, d, tn), lambda i, j: (l, 0, nf + j)),
            ],
            out_specs=pl.BlockSpec((tm, tn), lambda i, j: (i, j)),
            out_shape=jax.ShapeDtypeStruct((m_all, d_ff), BF16),
            compiler_params=_cparams("parallel", "parallel"),
            name="ffn_in",
        )(h_ffn, w_ffn_in_b, w_ffn_in_b)
        xa = mm_res(act, w_ffn_out_b, xa, 5, tm, tn, "ffn_out")

    out = pl.pallas_call(
        functools.partial(_final_norm_kernel, rc=rc),
        grid=(m_lat // tm,),
        in_specs=[pl.BlockSpec((tm, d), lambda i: (i, 0)),
                  pl.BlockSpec((1, d), lambda i: (0, 0))],
        out_specs=pl.BlockSpec((tm, d), lambda i: (i, 0)),
        out_shape=jax.ShapeDtypeStruct((m_lat, d), F32),
        compiler_params=_cparams("parallel"),
        name="final_norm",
    )(xa, g_final.reshape(1, d))
    return out.reshape(bsz, seq, d)
```

```python
import functools

import jax
import jax.numpy as jnp
from jax import lax
from jax.experimental import pallas as pl
from jax.experimental.pallas import tpu as pltpu

F32 = jnp.float32
BF16 = jnp.bfloat16

GRID_W = 64
CONV_CH = 512
CONV_WIDTH = 31
MLA_HEADS = 8
MLA_Q_RANK = 512
MLA_KV_RANK = 256
MLA_NOPE = 128
MLA_ROPE = 64
MLA_V = 128
GQA_HEADS = 8
GQA_KV_HEADS = 2
GQA_HEAD_DIM = 64
WINDOW = 128
N_BRANCH = 3
ROPE_DIM = 64
ROPE_HALF = ROPE_DIM // 2
ROPE_BASE = 10000.0
EPS = 1e-6
NEG_INF = -1e30

COL_MLA_KV = 0
COL_MLA_KR = COL_MLA_KV + MLA_KV_RANK
COL_GQA_K = COL_MLA_KR + MLA_ROPE
COL_GQA_V = COL_GQA_K + GQA_KV_HEADS * GQA_HEAD_DIM
KV_COLS = COL_GQA_V + GQA_KV_HEADS * GQA_HEAD_DIM
COL_MLA_Q = KV_COLS
COL_GQA_Q = COL_MLA_Q + MLA_Q_RANK
COL_CONV = COL_GQA_Q + GQA_HEADS * GQA_HEAD_DIM
COL_GATE = COL_CONV + 2 * CONV_CH

LANES = 128
MLA_PAD = 256
VMEM_LIMIT = 56 * 1024 * 1024
MLA_TQ = 1024
MLA_RG = 64
MLA_TK = 2048
MLA_UNROLL = 2
LOG2_E = 1.4426950408889634

P_MLA_Q = 0
P_GQA_Q = 512
P_CONV_A = 1024
P_CONV_B = 1536
P_CKV = 2048
P_KR = 2304
P_GQA_K = 2432
P_GQA_V = 2560
P_SMALL = 3072


def _cparams(*sem):
    return pltpu.CompilerParams(dimension_semantics=sem, vmem_limit_bytes=VMEM_LIMIT)


def _for_chunks(n_rows, rc, fn):
    n = n_rows // rc
    if n == 1:
        fn(0)
        return

    def body(i, carry):
        fn(pl.multiple_of(i * rc, rc))
        return carry

    lax.fori_loop(0, n, body, 0)


def _sigmoid(x):
    return 0.5 * jnp.tanh(0.5 * x) + 0.5


def _rope(x, cos, sin):
    return x * cos + pltpu.roll(x, LANES // 2, 1) * sin


def _dot(a, b):
    return jnp.dot(a, b, preferred_element_type=F32)


def _dot_nt(a, b):
    return lax.dot_general(a, b, (((1,), (1,)), ((), ())), preferred_element_type=F32)


def _ada_kernel(c_ref, w_ref, b_ref, o_ref):
    c = c_ref[...]
    s = c * _sigmoid(c)
    o_ref[...] = jnp.dot(s, w_ref[...], preferred_element_type=F32,
                         precision=lax.Precision.HIGHEST) + b_ref[...]


def _ada(cc, w_ada, b_ada):
    depth, d, n = w_ada.shape
    rows = cc.shape[0]
    tn = 1024
    return pl.pallas_call(
        _ada_kernel,
        grid=(depth, n // tn),
        in_specs=[
            pl.BlockSpec((rows, d), lambda l, j: (0, 0)),
            pl.BlockSpec((None, d, tn), lambda l, j: (l, 0, j)),
            pl.BlockSpec((None, 1, tn), lambda l, j: (l, 0, j)),
        ],
        out_specs=pl.BlockSpec((None, rows, tn), lambda l, j: (l, 0, j)),
        out_shape=jax.ShapeDtypeStruct((depth, rows, n), F32),
        compiler_params=_cparams("parallel", "parallel"),
        name="ada_mod",
    )(cc, w_ada, b_ada.reshape(depth, 1, n))


def _norm_mod_store(x_ref, g_ref, sh_ref, sc_ref, h_ref, rc):
    gs = g_ref[...] * (1.0 + sc_ref[...])
    sh = sh_ref[...]

    def chunk(r0):
        x = x_ref[pl.ds(r0, rc), :]
        ms = jnp.mean(x * x, axis=-1, keepdims=True)
        h_ref[pl.ds(r0, rc), :] = (x * lax.rsqrt(ms + EPS) * gs + sh).astype(BF16)

    _for_chunks(x_ref.shape[0], rc, chunk)


def _in_proj_kernel(x_ref, g_ref, sh_ref, sc_ref, w_ref, *rest, n_gate, col0, rc):
    o_ref, h_ref = rest[-2:]
    j = pl.program_id(1)

    @pl.when(j == 0)
    def _():
        _norm_mod_store(x_ref, g_ref, sh_ref, sc_ref, h_ref, rc)

    acc = _dot(h_ref[...], w_ref[...])
    o_ref[...] = jnp.where(col0 + j < n_gate, _sigmoid(acc), acc).astype(BF16)


def _ffn_in_kernel(h_ref, w1_ref, w2_ref, o_ref):
    h = h_ref[...]
    u1 = _dot(h, w1_ref[...])
    u2 = _dot(h, w2_ref[...])
    o_ref[...] = (u1 * _sigmoid(u1) * u2).astype(BF16)


def _mm_res_kernel(a_ref, w_ref, x_ref, gt_ref, o_ref):
    o_ref[...] = x_ref[...] + gt_ref[...] * _dot(a_ref[...], w_ref[...])


def _out_proj_kernel(a_ref, w_ref, x_ref, gt_ref, g_ref, sh_ref, sc_ref, *rest):
    o_ref, h_ref = rest[-2:]
    gs = g_ref[...] * (1.0 + sc_ref[...])
    rows = a_ref.shape[0] // 2
    for r in (0, rows):
        x_new = x_ref[r:r + rows, :] + gt_ref[...] * _dot(a_ref[r:r + rows, :], w_ref[...])
        o_ref[r:r + rows, :] = x_new
        ms = jnp.mean(x_new * x_new, axis=-1, keepdims=True)
        h_ref[r:r + rows, :] = (x_new * lax.rsqrt(ms + EPS) * gs + sh_ref[...]).astype(BF16)


def _merge_kernel(oc_ref, om_ref, og_ref, wc_ref, wm_ref, wg_ref, g0_ref, g1_ref, g2_ref, y_ref):
    y = g0_ref[...].astype(F32) * _dot(oc_ref[...], wc_ref[...])
    y += g1_ref[...].astype(F32) * _dot(om_ref[...], wm_ref[...])
    y += g2_ref[...].astype(F32) * _dot(og_ref[...], wg_ref[...])
    y_ref[...] = y.astype(BF16)


def _rms_bf16(a, g):
    ms = jnp.mean(a * a, axis=-1, keepdims=True)
    return (a * lax.rsqrt(ms + EPS) * g).astype(BF16)


def _qup_kernel(pq_ref, g_ref, wq_ref, cos_ref, sin_ref, q_ref, qg_ref, *, rc, s_mla, s_gqa):
    g = g_ref[...]

    def chunk(r0):
        rows = pl.ds(r0, rc)
        n = _rms_bf16(pq_ref[rows, 0:MLA_Q_RANK].astype(F32), g)
        q = _dot(n, wq_ref[...])
        cos = cos_ref[rows, :]
        sin = sin_ref[rows, :]
        for h in range(MLA_HEADS):
            c0 = h * MLA_PAD
            q_ref[rows, c0:c0 + LANES] = (q[:, c0:c0 + LANES] * s_mla).astype(BF16)
            r = _rope(q[:, c0 + LANES:c0 + 2 * LANES], cos, sin)
            q_ref[rows, c0 + LANES:c0 + 2 * LANES] = (r * s_mla).astype(BF16)
        for t in range(GQA_HEADS * GQA_HEAD_DIM // LANES):
            gq = pq_ref[rows, MLA_Q_RANK + t * LANES:MLA_Q_RANK + (t + 1) * LANES].astype(F32)
            qg_ref[rows, t * LANES:(t + 1) * LANES] = (_rope(gq, cos, sin) * s_gqa).astype(BF16)

    _for_chunks(pq_ref.shape[0], rc, chunk)


def _kvup_kernel(ckv_ref, kr_ref, g_ref, wk_ref, wv_ref, cos_ref, sin_ref,
                 k_ref, krope_ref, v_ref, kg_ref, vg_ref, *, rc):
    g = g_ref[...]

    def chunk(r0):
        rows = pl.ds(r0, rc)
        n = _rms_bf16(ckv_ref[rows, :].astype(F32), g)
        k_ref[rows, :] = _dot(n, wk_ref[...]).astype(BF16)
        v_ref[rows, :] = _dot(n, wv_ref[...]).astype(BF16)
        cos = cos_ref[rows, :]
        sin = sin_ref[rows, :]
        krope_ref[rows, :] = _rope(kr_ref[rows, 0:LANES].astype(F32), cos, sin).astype(BF16)
        lane = lax.broadcasted_iota(jnp.int32, (rc, LANES), 1)
        even_quarter = (lane // ROPE_HALF) % 2 == 0
        gk = _rope(kr_ref[rows, LANES:2 * LANES].astype(F32), cos, sin)
        kg_ref[rows, 0:LANES] = jnp.where(even_quarter, gk, pltpu.roll(gk, ROPE_HALF, 1)).astype(BF16)
        kg_ref[rows, LANES:2 * LANES] = jnp.where(even_quarter, pltpu.roll(gk, LANES - ROPE_HALF, 1),
                                                   gk).astype(BF16)
        low_half = lane < GQA_HEAD_DIM
        gv = kr_ref[rows, 2 * LANES:3 * LANES].astype(F32)
        gv_sw = pltpu.roll(gv, GQA_HEAD_DIM, 1)
        vg_ref[rows, 0:LANES] = jnp.where(low_half, gv, gv_sw).astype(BF16)
        vg_ref[rows, LANES:2 * LANES] = jnp.where(low_half, gv_sw, gv).astype(BF16)

    _for_chunks(ckv_ref.shape[0], rc, chunk)


def _mla_attn_kernel(q_ref, k_ref, krope_ref, v_ref, *rest, tk, rg, unroll):
    n_buf = 3 * unroll
    o_ref = rest[-n_buf - 3]
    bufs = rest[-n_buf - 2:-2]
    s_buf, p_buf, a_buf = bufs[:unroll], bufs[unroll:2 * unroll], bufs[2 * unroll:]
    m_ref, acc_ref = rest[-2:]
    tq = q_ref.shape[0]
    head = k_ref.shape[0] % tk
    has_head = 1 if head else 0
    n = k_ref.shape[0] // tk + has_head

    def start_of(c):
        return head + (c - has_head) * tk

    def size_of(c):
        return head if (has_head and c == 0) else tk

    def qk(slot, start, size):
        k = jnp.concatenate([k_ref[pl.ds(start, size), :], krope_ref[pl.ds(start, size), :]], axis=1)
        s_buf[slot][:, 0:size] = _dot_nt(q_ref[...], k)

    def sm(slot, size):
        for r in range(0, tq, rg):
            s = s_buf[slot][r:r + rg, 0:size]
            m_old = m_ref[r:r + rg, :]
            m_new = jnp.maximum(m_old, jnp.max(s, axis=-1, keepdims=True))
            p_buf[slot][r:r + rg, 0:size] = jnp.exp2(s - pltpu.repeat(m_new, size // LANES, 1)).astype(BF16)
            a_buf[slot][r:r + rg, :] = jnp.exp2(m_old - m_new)
            m_ref[r:r + rg, :] = m_new

    def pv(slot, start, size):
        alpha = pltpu.repeat(a_buf[slot][...], acc_ref.shape[1] // LANES, 1)
        v = jnp.concatenate([v_ref[pl.ds(start, size), :], jnp.ones((size, LANES), BF16)], axis=1)
        acc_ref[...] = alpha * acc_ref[...] + _dot(p_buf[slot][:, 0:size], v)

    def static_start(c):
        return 0 if (has_head and c == 0) else start_of(c)

    def static_tick(t):
        if 0 <= t < n:
            qk(t % unroll, static_start(t), size_of(t))
        if 0 <= t - 2 < n:
            pv((t - 2) % unroll, static_start(t - 2), size_of(t - 2))
        if 0 <= t - 1 < n:
            sm((t - 1) % unroll, size_of(t - 1))

    m_ref[...] = jnp.full(m_ref.shape, NEG_INF, F32)
    acc_ref[...] = jnp.zeros(acc_ref.shape, F32)

    first_steady = 2 + has_head
    trips = max(n - first_steady, 0) // unroll
    for t in range(first_steady):
        static_tick(t)
    if trips:
        def trip(j, carry):
            for u in range(unroll):
                t = first_steady + unroll * j + u
                qk((first_steady + u) % unroll, pl.multiple_of(start_of(t), 2 * LANES), tk)
                pv((first_steady + u - 2) % unroll, pl.multiple_of(start_of(t - 2), 2 * LANES), tk)
                sm((first_steady + u - 1) % unroll, tk)
            return carry

        lax.fori_loop(0, trips, trip, 0)
    for t in range(first_steady + unroll * trips, n + 2):
        static_tick(t)
    acc = acc_ref[...]
    o_ref[...] = (acc[:, :MLA_V] / acc[:, MLA_V:MLA_V + 1]).astype(BF16)


def _gqa_attn_kernel(q_ref, kc_ref, vc_ref, sink_ref, *rest, seq, n_sub, latent):
    if latent:
        kl_ref, vl_ref = rest[0], rest[1]
    o_ref, s0, s1, p0, p1, e0, e1 = rest[-7:]
    s_buf, p_buf, e_buf = (s0, s1), (p0, p1), (e0, e1)
    span = 3 * WINDOW
    n_c = kc_ref.shape[0]
    rep = GQA_HEADS // GQA_KV_HEADS
    lane = lax.broadcasted_iota(jnp.int32, (1, LANES), 1)
    low_half = lane < GQA_HEAD_DIM
    even_quarter = (lane // ROPE_HALF) % 2 == 0
    half_masks = (jnp.where(even_quarter, 1.0, 0.0).astype(BF16), jnp.where(even_quarter, 0.0, 1.0).astype(BF16))
    sinks = sink_ref[...] * LOG2_E
    sinks = [sinks[:, h:h + 1] for h in range(GQA_HEADS)]

    def row0(r):
        return r * WINDOW if isinstance(r, int) else pl.multiple_of(r * WINDOW, WINDOW)

    def win_start(r):
        if isinstance(r, int):
            return min(max(r * WINDOW - WINDOW, 0), seq - span)
        return pl.multiple_of(jnp.clip(r * WINDOW - WINDOW, 0, seq - span), WINDOW)

    def qk(slot, r):
        rows = pl.ds(row0(r), WINDOW)
        for h in range(GQA_HEADS):
            g = h // rep
            q = q_ref[rows, (h // 2) * LANES:(h // 2 + 1) * LANES] * half_masks[h % 2]
            s_buf[slot][h, :, 0:n_c] = _dot_nt(q, kc_ref[:, g * LANES:(g + 1) * LANES])
            if latent:
                s_buf[slot][h, :, n_c:n_c + span] = _dot_nt(
                    q, kl_ref[pl.ds(win_start(r), span), g * LANES:(g + 1) * LANES])

    def sm(slot, r):
        if latent:
            qpos = r * WINDOW + lax.broadcasted_iota(jnp.int32, (WINDOW, span), 0)
            kpos = win_start(r) + lax.broadcasted_iota(jnp.int32, (WINDOW, span), 1)
            bias = jnp.where(jnp.abs(qpos - kpos) <= WINDOW, 0.0, NEG_INF)
        for h in range(GQA_HEADS):
            sc = s_buf[slot][h, :, 0:n_c]
            m = jnp.maximum(jnp.max(sc, axis=-1, keepdims=True), sinks[h])
            if latent:
                sw = s_buf[slot][h, :, n_c:n_c + span] + bias
                m = jnp.maximum(m, jnp.max(sw, axis=-1, keepdims=True))
                p_buf[slot][h, :, n_c:n_c + span] = jnp.exp2(sw - m).astype(BF16)
            p_buf[slot][h, :, 0:n_c] = jnp.exp2(sc - m).astype(BF16)
            e_buf[slot][h] = jnp.broadcast_to(jnp.exp2(sinks[h] - m), (WINDOW, LANES))

    def pv(slot, r):
        outs = []
        for h in range(GQA_HEADS):
            g = h // rep
            vc = jnp.concatenate([vc_ref[:, g * LANES:(g + 1) * LANES], jnp.ones((n_c, LANES), BF16)], axis=1)
            o = _dot(p_buf[slot][h, :, 0:n_c], vc)
            if latent:
                vw = jnp.concatenate([vl_ref[pl.ds(win_start(r), span), g * LANES:(g + 1) * LANES],
                                      jnp.ones((span, LANES), BF16)], axis=1)
                o = o + _dot(p_buf[slot][h, :, n_c:n_c + span], vw)
            outs.append(o[:, :LANES] / (o[:, LANES:] + e_buf[slot][h]))
        rows = pl.ds(row0(r), WINDOW)
        for t in range(GQA_HEADS // 2):
            o_ref[rows, t * LANES:(t + 1) * LANES] = jnp.where(low_half, outs[2 * t], outs[2 * t + 1]).astype(BF16)

    def tick(t, par):
        qk(par, t)
        pv(par, t - 2)
        sm(1 - par, t - 1)

    def static_tick(t):
        if 0 <= t < n_sub:
            qk(t % 2, t)
        if 0 <= t - 2 < n_sub:
            pv(t % 2, t - 2)
        if 0 <= t - 1 < n_sub:
            sm((t - 1) % 2, t - 1)

    trips = max(n_sub - 2, 0) // 2
    for t in range(2):
        static_tick(t)
    if trips:
        def trip(j, carry):
            for u in range(2):
                tick(2 + 2 * j + u, u)
            return carry

        lax.fori_loop(0, trips, trip, 0)
    for t in range(2 + 2 * trips, n_sub + 2):
        static_tick(t)


def _conv_kernel(a_ref, b_ref, w_ref, cb_ref, lg_ref, lb_ref, o_ref, u_ref, *, seq, rc):
    halo = 16
    u_ref[0:halo, :] = jnp.zeros((halo, CONV_CH), F32)
    u_ref[halo + seq:2 * halo + seq, :] = jnp.zeros((halo, CONV_CH), F32)

    def fill(r0):
        a = a_ref[pl.ds(r0, 256), :].astype(F32)
        b = b_ref[pl.ds(r0, 256), :].astype(F32)
        u_ref[pl.ds(halo + r0, 256), :] = a * _sigmoid(b)

    _for_chunks(seq, 256, fill)

    cb = cb_ref[...]
    lg = lg_ref[...]
    lb = lb_ref[...]

    half = CONV_CH // 2
    first_tap = halo - CONV_WIDTH // 2

    def conv(r0):
        parts = []
        for c0 in (0, half):
            acc = jnp.zeros((rc, half), F32) + cb[:, c0:c0 + half]
            for s in range(8):
                part = None
                for a in range(4):
                    k = 8 * a + s - first_tap
                    if 0 <= k < CONV_WIDTH:
                        rows = u_ref[pl.ds(pl.multiple_of(r0 + 8 * a, 8), rc + 8), c0:c0 + half]
                        term = w_ref[k:k + 1, c0:c0 + half] * rows
                        part = term if part is None else part + term
                acc = acc + part[s:s + rc]
            parts.append(acc)
        acc = jnp.concatenate(parts, axis=-1)
        mu = jnp.mean(acc, axis=-1, keepdims=True)
        xc = acc - mu
        var = jnp.mean(xc * xc, axis=-1, keepdims=True)
        y = xc * lax.rsqrt(var + EPS) * lg + lb
        o_ref[pl.ds(r0, rc), :] = (y * _sigmoid(y)).astype(BF16)

    _for_chunks(seq, rc, conv)


def _final_norm_kernel(x_ref, g_ref, o_ref, *, rc):
    g = g_ref[...]

    def chunk(r0):
        x = x_ref[pl.ds(r0, rc), :]
        ms = jnp.mean(x * x, axis=-1, keepdims=True)
        o_ref[pl.ds(r0, rc), :] = x * lax.rsqrt(ms + EPS) * g

    _for_chunks(x_ref.shape[0], rc, chunk)


def _rope_tables(length, extra):
    rows = length // GRID_W
    row = jnp.repeat(jnp.arange(rows), GRID_W).astype(F32)
    col = jnp.tile(jnp.arange(GRID_W), rows).astype(F32)
    n_freq = ROPE_DIM // 4
    inv_freq = ROPE_BASE ** (-jnp.arange(n_freq, dtype=F32) / n_freq)
    a_row = row[:, None] * inv_freq[None, :]
    a_col = col[:, None] * inv_freq[None, :]
    ang = jnp.concatenate([a_row, a_col], axis=-1)
    cos = jnp.concatenate([jnp.cos(ang), jnp.ones((extra, ROPE_HALF), F32)], axis=0)
    sin = jnp.concatenate([jnp.sin(ang), jnp.zeros((extra, ROPE_HALF), F32)], axis=0)
    cos = jnp.tile(cos, (1, LANES // ROPE_HALF))
    sin = jnp.concatenate([-sin, -sin, sin, sin], axis=-1)
    return cos, sin


def _rope_first_second():
    q = ROPE_DIM // 4
    first = list(range(0, q)) + list(range(2 * q, 3 * q))
    second = list(range(q, 2 * q)) + list(range(3 * q, 4 * q))
    return jnp.array(first), jnp.array(second)


def _pick_tile(*sizes):
    for t in (1024, 512, 256):
        if all(s % t == 0 for s in sizes):
            return t
    raise ValueError(f"unsupported row counts {sizes}")


def kernel(x, c, ctx, c_ctx, w_ada, b_ada, g_mix, w_in, conv_w, conv_b, conv_ln_g, conv_ln_b, w_conv_out, g_q_a, w_q_up, g_kv_a, w_kv_up, w_mla_out, gqa_sink, w_gqa_out, w_out, g_ffn, w_ffn_in, w_ffn_out, g_final):
    bsz, seq, d = x.shape
    n_ctx = ctx.shape[1]
    depth = w_ada.shape[0]
    d_ff = w_ffn_out.shape[1]
    m_lat = bsz * seq
    m_ctx = bsz * n_ctx
    m_all = m_lat + m_ctx
    assert seq % GRID_W == 0 and n_ctx == 2 * WINDOW and seq % (4 * WINDOW) == 0
    ada_rows = -(-(bsz + 1) // 8) * 8
    assert w_in.shape[2] == COL_GATE + N_BRANCH * d

    tm = _pick_tile(seq, m_ctx)
    ts = min(tm, 512)
    tn = 512
    rc = 256
    n_lat_tiles = m_lat // tm
    tiles_per_batch = seq // tm
    p_base = N_BRANCH * d
    p_cols = p_base + P_SMALL
    tn_in = 1536
    assert p_base % tn_in == 0 and p_cols % tn_in == 0 and p_cols - tn_in <= p_base + P_CKV
    ctx_blk0 = m_lat // n_ctx

    def mod_row(i, tile):
        return jnp.minimum(i // (seq // tile), bsz)

    def cols(a, n):
        return w_in[:, :, a:a + n].astype(BF16)

    def zeros(n):
        return jnp.zeros((depth, d, n), BF16)

    first, second = _rope_first_second()

    def rope_single(w):
        z = jnp.zeros(w.shape[:-1] + (ROPE_HALF,), w.dtype)
        return jnp.concatenate([w[..., first], z, w[..., second], z], axis=-1)

    def rope_pairs(w):
        wp = w.reshape(w.shape[:-1] + (-1, 2, ROPE_DIM))
        t = jnp.concatenate([wp[..., 0, :][..., first], wp[..., 1, :][..., first],
                             wp[..., 0, :][..., second], wp[..., 1, :][..., second]], axis=-1)
        return t.reshape(w.shape)

    w_in_r = jnp.concatenate([
        cols(COL_GATE, N_BRANCH * d),
        cols(COL_MLA_Q, MLA_Q_RANK), rope_pairs(cols(COL_GQA_Q, GQA_HEADS * GQA_HEAD_DIM)),
        cols(COL_CONV, CONV_CH), cols(COL_CONV + CONV_CH, CONV_CH),
        cols(COL_MLA_KV, MLA_KV_RANK), rope_single(cols(COL_MLA_KR, MLA_ROPE)),
        rope_pairs(cols(COL_GQA_K, GQA_KV_HEADS * GQA_HEAD_DIM)), cols(COL_GQA_V, GQA_KV_HEADS * GQA_HEAD_DIM),
        zeros(P_SMALL - P_GQA_V - GQA_KV_HEADS * GQA_HEAD_DIM)], axis=-1)
    wq = w_q_up.reshape(depth, MLA_Q_RANK, MLA_HEADS, MLA_NOPE + MLA_ROPE).astype(BF16)
    wq = jnp.concatenate([wq[..., :MLA_NOPE], rope_single(wq[..., MLA_NOPE:])], axis=-1)
    wq = wq.reshape(depth, MLA_Q_RANK, MLA_HEADS * MLA_PAD)
    wkv = w_kv_up.reshape(depth, MLA_KV_RANK, MLA_HEADS, MLA_NOPE + MLA_V).astype(BF16)
    wk = wkv[..., :MLA_NOPE].reshape(depth, MLA_KV_RANK, MLA_HEADS * MLA_NOPE)
    wv = wkv[..., MLA_NOPE:].reshape(depth, MLA_KV_RANK, MLA_HEADS * MLA_V)
    w_conv_out_b = w_conv_out.astype(BF16)
    w_mla_out_b = w_mla_out.astype(BF16)
    w_gqa_out_b = w_gqa_out.astype(BF16)
    w_out_b = w_out.astype(BF16)
    w_ffn_in_b = w_ffn_in.astype(BF16)
    w_ffn_out_b = w_ffn_out.astype(BF16)
    cos, sin = _rope_tables(seq, ts)

    cc = jnp.zeros((ada_rows, d), F32).at[:bsz].set(c).at[bsz].set(c_ctx)
    mod = _ada(cc, w_ada, b_ada).reshape(depth, ada_rows, 1, 6 * d)

    x_lat = x.reshape(m_lat, d)
    x_ctx = ctx.reshape(m_ctx, d)
    xa = None

    def vec(a):
        return a.reshape(depth, 1, a.shape[-1])

    g_mix3, g_ffn3, g_q3, g_kv3 = vec(g_mix), vec(g_ffn), vec(g_q_a), vec(g_kv_a)
    conv_b3, ln_g3, ln_b3, sink3 = vec(conv_b), vec(conv_ln_g), vec(conv_ln_b), vec(gqa_sink)

    def mod_spec(l, part, tile, width, with_j, tile0=0):
        nb = d // width
        if with_j:
            return pl.BlockSpec((None, None, 1, width),
                                lambda i, j: (l, mod_row(tile0 + i, tile), 0, part * nb + j))
        return pl.BlockSpec((None, None, 1, width), lambda i, j: (l, mod_row(tile0 + i, tile), 0, part))

    def table_specs(tile):
        per_batch = seq // tile
        n_lat = m_lat // tile
        return [pl.BlockSpec((tile, LANES), lambda i: (jnp.where(i < n_lat, i % per_batch, per_batch), 0))] * 2

    for l in range(depth):
        last = l == depth - 1
        m_rows = m_lat if last else m_all

        lat_src, lat_tile0 = (x_lat, 0) if l == 0 else (xa, 0)
        ctx_src, ctx_tile0 = (x_ctx, 0) if l == 0 else (xa, n_lat_tiles)

        def in_proj_call(src, src_tile0, row_tile0, n_row_tiles, col_tile0, n_col_tiles, prev):
            in_specs = [
                pl.BlockSpec((tm, d), lambda i, j: (src_tile0 + i, 0)),
                pl.BlockSpec((None, 1, d), lambda i, j: (l, 0, 0)),
                mod_spec(l, 0, tm, d, False, row_tile0),
                mod_spec(l, 1, tm, d, False, row_tile0),
                pl.BlockSpec((None, d, tn_in), lambda i, j: (l, 0, col_tile0 + j)),
            ]
            args = [src, g_mix3, mod, mod, w_in_r]
            aliases = {}
            if prev is not None:
                in_specs.append(pl.BlockSpec(memory_space=pl.ANY))
                args.append(prev)
                aliases = {5: 0}
            return pl.pallas_call(
                functools.partial(_in_proj_kernel, n_gate=p_base // tn_in, col0=col_tile0, rc=rc),
                grid=(n_row_tiles, n_col_tiles),
                in_specs=in_specs,
                out_specs=pl.BlockSpec((tm, tn_in), lambda i, j: (row_tile0 + i, col_tile0 + j)),
                out_shape=jax.ShapeDtypeStruct((m_all, p_cols), BF16),
                scratch_shapes=[pltpu.VMEM((tm, d), BF16)],
                input_output_aliases=aliases,
                compiler_params=_cparams("parallel", "arbitrary"),
                name="in_proj" if prev is None else "in_proj_ctx",
            )(*args)

        n_col_tiles = p_cols // tn_in
        p = in_proj_call(lat_src, lat_tile0, 0, n_lat_tiles, 0, n_col_tiles, None)
        if last:
            p = in_proj_call(ctx_src, ctx_tile0, n_lat_tiles, m_ctx // tm, n_col_tiles - 1, 1, p)
        else:
            p = in_proj_call(ctx_src, ctx_tile0, n_lat_tiles, m_ctx // tm, 0, n_col_tiles, p)

        q_mla, q_gqa = pl.pallas_call(
            functools.partial(_qup_kernel, rc=rc, s_mla=float(LOG2_E * (MLA_NOPE + MLA_ROPE) ** -0.5),
                              s_gqa=float(LOG2_E * GQA_HEAD_DIM ** -0.5)),
            grid=(m_rows // ts,),
            in_specs=[
                pl.BlockSpec((ts, P_CONV_A - P_MLA_Q), lambda i: (i, (p_base + P_MLA_Q) // (P_CONV_A - P_MLA_Q))),
                pl.BlockSpec((None, 1, MLA_Q_RANK), lambda i: (l, 0, 0)),
                pl.BlockSpec((None, MLA_Q_RANK, MLA_HEADS * MLA_PAD), lambda i: (l, 0, 0)),
            ] + table_specs(ts),
            out_specs=[pl.BlockSpec((ts, MLA_HEADS * MLA_PAD), lambda i: (i, 0)),
                       pl.BlockSpec((ts, GQA_HEADS * GQA_HEAD_DIM), lambda i: (i, 0))],
            out_shape=[jax.ShapeDtypeStruct((m_all, MLA_HEADS * MLA_PAD), BF16),
                       jax.ShapeDtypeStruct((m_all, GQA_HEADS * GQA_HEAD_DIM), BF16)],
            compiler_params=_cparams("parallel"),
            name="q_up",
        )(p, g_q3, wq, cos, sin)

        tkv = n_ctx
        kv_per_batch = seq // tkv
        kv_lat_tiles = m_lat // tkv

        def kv_block(i):
            lat = (i // kv_per_batch) * (kv_per_batch + 1) + 1 + i % kv_per_batch
            return jnp.where(i < kv_lat_tiles, lat, (i - kv_lat_tiles) * (kv_per_batch + 1))

        k_mla, k_rope, v_mla, k_gqa, v_gqa = pl.pallas_call(
            functools.partial(_kvup_kernel, rc=tkv),
            grid=(m_all // tkv,),
            in_specs=[
                pl.BlockSpec((tkv, MLA_KV_RANK), lambda i: (i, (p_base + P_CKV) // MLA_KV_RANK)),
                pl.BlockSpec((tkv, 3 * LANES), lambda i: (i, (p_base + P_KR) // (3 * LANES))),
                pl.BlockSpec((None, 1, MLA_KV_RANK), lambda i: (l, 0, 0)),
                pl.BlockSpec((None, MLA_KV_RANK, MLA_HEADS * MLA_NOPE), lambda i: (l, 0, 0)),
                pl.BlockSpec((None, MLA_KV_RANK, MLA_HEADS * MLA_V), lambda i: (l, 0, 0)),
            ] + table_specs(tkv),
            out_specs=[pl.BlockSpec((tkv, MLA_HEADS * MLA_NOPE), lambda i: (kv_block(i), 0)),
                       pl.BlockSpec((tkv, LANES), lambda i: (kv_block(i), 0)),
                       pl.BlockSpec((tkv, MLA_HEADS * MLA_V), lambda i: (kv_block(i), 0)),
                       pl.BlockSpec((tkv, 2 * LANES), lambda i: (i, 0)),
                       pl.BlockSpec((tkv, 2 * LANES), lambda i: (i, 0))],
            out_shape=[jax.ShapeDtypeStruct((m_all, MLA_HEADS * MLA_NOPE), BF16),
                       jax.ShapeDtypeStruct((m_all, LANES), BF16),
                       jax.ShapeDtypeStruct((m_all, MLA_HEADS * MLA_V), BF16),
                       jax.ShapeDtypeStruct((m_all, 2 * LANES), BF16),
                       jax.ShapeDtypeStruct((m_all, 2 * LANES), BF16)],
            compiler_params=_cparams("parallel"),
            name="kv_up",
        )(p, p, g_kv3, wk, wv, cos, sin)

        tq = min(MLA_TQ, seq)
        nq = seq // tq
        def mla_scratch(rows, tk):
            return ([pltpu.VMEM((rows, tk), F32)] * MLA_UNROLL + [pltpu.VMEM((rows, tk), BF16)] * MLA_UNROLL
                    + [pltpu.VMEM((rows, LANES), F32)] * (MLA_UNROLL + 1) + [pltpu.VMEM((rows, MLA_PAD), F32)])

        o_mla = pl.pallas_call(
            functools.partial(_mla_attn_kernel, tk=MLA_TK, rg=MLA_RG, unroll=MLA_UNROLL),
            grid=(bsz, MLA_HEADS, nq),
            scratch_shapes=mla_scratch(tq, MLA_TK),
            in_specs=[
                pl.BlockSpec((tq, MLA_PAD), lambda b, h, i: (b * nq + i, h)),
                pl.BlockSpec((seq + n_ctx, MLA_NOPE), lambda b, h, i: (b, h)),
                pl.BlockSpec((seq + n_ctx, LANES), lambda b, h, i: (b, 0)),
                pl.BlockSpec((seq + n_ctx, MLA_V), lambda b, h, i: (b, h)),
            ],
            out_specs=pl.BlockSpec((tq, MLA_V), lambda b, h, i: (b * nq + i, h)),
            out_shape=jax.ShapeDtypeStruct((m_all, MLA_HEADS * MLA_V), BF16),
            compiler_params=_cparams("parallel", "parallel", "arbitrary"),
            name="mla_attn",
        )(q_mla, k_mla, k_rope, v_mla)
        o_mla = o_mla if last else pl.pallas_call(
            functools.partial(_mla_attn_kernel, tk=n_ctx, rg=MLA_RG, unroll=MLA_UNROLL),
            grid=(bsz, MLA_HEADS),
            scratch_shapes=mla_scratch(n_ctx, n_ctx),
            in_specs=[
                pl.BlockSpec((n_ctx, MLA_PAD), lambda b, h: (ctx_blk0 + b, h)),
                pl.BlockSpec((n_ctx, MLA_NOPE), lambda b, h: (b * (kv_per_batch + 1), h)),
                pl.BlockSpec((n_ctx, LANES), lambda b, h: (b * (kv_per_batch + 1), 0)),
                pl.BlockSpec((n_ctx, MLA_V), lambda b, h: (b * (kv_per_batch + 1), h)),
                pl.BlockSpec(memory_space=pl.ANY),
            ],
            out_specs=pl.BlockSpec((n_ctx, MLA_V), lambda b, h: (ctx_blk0 + b, h)),
            out_shape=jax.ShapeDtypeStruct((m_all, MLA_HEADS * MLA_V), BF16),
            input_output_aliases={4: 0},
            compiler_params=_cparams("parallel", "parallel"),
            name="mla_attn_ctx",
        )(q_mla, k_mla, k_rope, v_mla, o_mla)

        def gqa_scratch(cols):
            return ([pltpu.VMEM((GQA_HEADS, WINDOW, cols), F32)] * 2
                    + [pltpu.VMEM((GQA_HEADS, WINDOW, cols), BF16)] * 2
                    + [pltpu.VMEM((GQA_HEADS, WINDOW, LANES), F32)] * 2)

        o_gqa = pl.pallas_call(
            functools.partial(_gqa_attn_kernel, seq=seq, n_sub=seq // WINDOW, latent=True),
            grid=(bsz,),
            in_specs=[
                pl.BlockSpec((seq, GQA_HEADS * GQA_HEAD_DIM), lambda b: (b, 0)),
                pl.BlockSpec((n_ctx, 2 * LANES), lambda b: (ctx_blk0 + b, 0)),
                pl.BlockSpec((n_ctx, 2 * LANES), lambda b: (ctx_blk0 + b, 0)),
                pl.BlockSpec((None, 1, GQA_HEADS), lambda b: (l, 0, 0)),
                pl.BlockSpec((seq, 2 * LANES), lambda b: (b, 0)),
                pl.BlockSpec((seq, 2 * LANES), lambda b: (b, 0)),
            ],
            out_specs=pl.BlockSpec((seq, GQA_HEADS * GQA_HEAD_DIM), lambda b: (b, 0)),
            out_shape=jax.ShapeDtypeStruct((m_all, GQA_HEADS * GQA_HEAD_DIM), BF16),
            scratch_shapes=gqa_scratch(n_ctx + 3 * WINDOW),
            compiler_params=_cparams("parallel"),
            name="gqa_attn",
        )(q_gqa, k_gqa, v_gqa, sink3, k_gqa, v_gqa)
        o_gqa = o_gqa if last else pl.pallas_call(
            functools.partial(_gqa_attn_kernel, seq=n_ctx, n_sub=n_ctx // WINDOW, latent=False),
            grid=(bsz,),
            scratch_shapes=gqa_scratch(n_ctx),
            in_specs=[
                pl.BlockSpec((n_ctx, GQA_HEADS * GQA_HEAD_DIM), lambda b: (ctx_blk0 + b, 0)),
                pl.BlockSpec((n_ctx, 2 * LANES), lambda b: (ctx_blk0 + b, 0)),
                pl.BlockSpec((n_ctx, 2 * LANES), lambda b: (ctx_blk0 + b, 0)),
                pl.BlockSpec((None, 1, GQA_HEADS), lambda b: (l, 0, 0)),
                pl.BlockSpec(memory_space=pl.ANY),
            ],
            out_specs=pl.BlockSpec((n_ctx, GQA_HEADS * GQA_HEAD_DIM), lambda b: (ctx_blk0 + b, 0)),
            out_shape=jax.ShapeDtypeStruct((m_all, GQA_HEADS * GQA_HEAD_DIM), BF16),
            input_output_aliases={4: 0},
            compiler_params=_cparams("parallel"),
            name="gqa_attn_ctx",
        )(q_gqa, k_gqa, v_gqa, sink3, o_gqa)

        def conv_call(length, blk0, prev):
            crc = 64
            in_specs = [
                pl.BlockSpec((length, CONV_CH), lambda b: (blk0 + b, (p_base + P_CONV_A) // CONV_CH)),
                pl.BlockSpec((length, CONV_CH), lambda b: (blk0 + b, (p_base + P_CONV_B) // CONV_CH)),
                pl.BlockSpec((None, CONV_WIDTH, CONV_CH), lambda b: (l, 0, 0)),
                pl.BlockSpec((None, 1, CONV_CH), lambda b: (l, 0, 0)),
                pl.BlockSpec((None, 1, CONV_CH), lambda b: (l, 0, 0)),
                pl.BlockSpec((None, 1, CONV_CH), lambda b: (l, 0, 0)),
            ]
            args = [p, p, conv_w, conv_b3, ln_g3, ln_b3]
            aliases = {}
            if prev is not None:
                in_specs.append(pl.BlockSpec(memory_space=pl.ANY))
                args.append(prev)
                aliases = {6: 0}

            def body(a_ref, b_ref, w_ref, cb_ref, lg_ref, lb_ref, *rest):
                o_ref, u_ref = rest[-2:]
                _conv_kernel(a_ref, b_ref, w_ref, cb_ref, lg_ref, lb_ref, o_ref, u_ref, seq=length, rc=crc)

            return pl.pallas_call(
                body,
                grid=(bsz,),
                in_specs=in_specs,
                out_specs=pl.BlockSpec((length, CONV_CH), lambda b: (blk0 + b, 0)),
                out_shape=jax.ShapeDtypeStruct((m_all, CONV_CH), BF16),
                scratch_shapes=[pltpu.VMEM((length + 32, CONV_CH), F32)],
                input_output_aliases=aliases,
                compiler_params=_cparams("parallel"),
                name="conv_branch" if prev is None else "conv_branch_ctx",
            )(*args)

        o_conv = conv_call(seq, 0, None)
        if not last:
            o_conv = conv_call(n_ctx, ctx_blk0, o_conv)

        y = pl.pallas_call(
            _merge_kernel,
            grid=(m_rows // ts, 1),
            in_specs=[
                pl.BlockSpec((ts, CONV_CH), lambda i, j: (i, 0)),
                pl.BlockSpec((ts, MLA_HEADS * MLA_V), lambda i, j: (i, 0)),
                pl.BlockSpec((ts, GQA_HEADS * GQA_HEAD_DIM), lambda i, j: (i, 0)),
                pl.BlockSpec((None, CONV_CH, d), lambda i, j: (l, 0, 0)),
                pl.BlockSpec((None, MLA_HEADS * MLA_V, d), lambda i, j: (l, 0, 0)),
                pl.BlockSpec((None, GQA_HEADS * GQA_HEAD_DIM, d), lambda i, j: (l, 0, 0)),
                pl.BlockSpec((ts, d), lambda i, j: (i, 0)),
                pl.BlockSpec((ts, d), lambda i, j: (i, 1)),
                pl.BlockSpec((ts, d), lambda i, j: (i, 2)),
            ],
            out_specs=pl.BlockSpec((ts, d), lambda i, j: (i, 0)),
            out_shape=jax.ShapeDtypeStruct((m_all, d), BF16),
            compiler_params=_cparams("parallel", "parallel"),
            name="merge",
        )(o_conv, o_mla, o_gqa, w_conv_out_b, w_mla_out_b, w_gqa_out_b, p, p, p)

        def mm_res(a, w, xin, part, tile, width, name):
            kdim = a.shape[1]
            return pl.pallas_call(
                _mm_res_kernel,
                grid=(m_rows // tile, d // width),
                in_specs=[
                    pl.BlockSpec((tile, kdim), lambda i, j: (i, 0)),
                    pl.BlockSpec((None, kdim, width), lambda i, j: (l, 0, j)),
                    pl.BlockSpec((tile, width), lambda i, j: (i, j)),
                    mod_spec(l, part, tile, width, True),
                ],
                out_specs=pl.BlockSpec((tile, width), lambda i, j: (i, j)),
                out_shape=jax.ShapeDtypeStruct((m_all, d), F32),
                compiler_params=_cparams("parallel", "parallel"),
                name=name,
            )(a, w, xin, mod)

        def out_proj_call(res_src, res_tile0, row_tile0, n_row_tiles, prev):
            in_specs = [
                pl.BlockSpec((ts, d), lambda i, j: (row_tile0 + i, 0)),
                pl.BlockSpec((None, d, d), lambda i, j: (l, 0, 0)),
                pl.BlockSpec((ts, d), lambda i, j: (res_tile0 + i, 0)),
                mod_spec(l, 2, ts, d, False, row_tile0),
                pl.BlockSpec((None, 1, d), lambda i, j: (l, 0, 0)),
                mod_spec(l, 3, ts, d, False, row_tile0),
                mod_spec(l, 4, ts, d, False, row_tile0),
            ]
            args = [y, w_out_b, res_src, mod, g_ffn3, mod, mod]
            aliases = {}
            if prev is not None:
                in_specs += [pl.BlockSpec(memory_space=pl.ANY)] * 2
                args += list(prev)
                aliases = {7: 0, 8: 1}
            return pl.pallas_call(
                _out_proj_kernel,
                grid=(n_row_tiles, 1),
                in_specs=in_specs,
                out_specs=[pl.BlockSpec((ts, d), lambda i, j: (row_tile0 + i, 0))] * 2,
                out_shape=[jax.ShapeDtypeStruct((m_all, d), F32), jax.ShapeDtypeStruct((m_all, d), BF16)],
                input_output_aliases=aliases,
                compiler_params=_cparams("parallel", "parallel"),
                name="out_proj" if prev is None else "out_proj_ctx",
            )(*args)

        ts_per_tm = tm // ts
        xa, h_ffn = out_proj_call(lat_src, lat_tile0 * ts_per_tm, 0, m_lat // ts, None)
        if not last:
            xa, h_ffn = out_proj_call(ctx_src, ctx_tile0 * ts_per_tm, m_lat // ts, m_ctx // ts, (xa, h_ffn))

        nf = d_ff // tn
        act = pl.pallas_call(
            _ffn_in_kernel,
            grid=(m_rows // tm, nf),
            in_specs=[
                pl.BlockSpec((tm, d), lambda i, j: (i, 0)),
                pl.BlockSpec((None, d, tn), lambda i, j: (l, 0, j)),
                pl.BlockSpec((None, d, tn), lambda i, j: (l, 0, nf + j)),
            ],
            out_specs=pl.BlockSpec((tm, tn), lambda i, j: (i, j)),
            out_shape=jax.ShapeDtypeStruct((m_all, d_ff), BF16),
            compiler_params=_cparams("parallel", "parallel"),
            name="ffn_in",
        )(h_ffn, w_ffn_in_b, w_ffn_in_b)
        xa = mm_res(act, w_ffn_out_b, xa, 5, tm, tn, "ffn_out")

    out = pl.pallas_call(
        functools.partial(_final_norm_kernel, rc=rc),
        grid=(m_lat // tm,),
        in_specs=[pl.BlockSpec((tm, d), lambda i: (i, 0)),
                  pl.BlockSpec((1, d), lambda i: (0, 0))],
        out_specs=pl.BlockSpec((tm, d), lambda i: (i, 0)),
        out_shape=jax.ShapeDtypeStruct((m_lat, d), F32),
        compiler_params=_cparams("parallel"),
        name="final_norm",
    )(xa, g_final.reshape(1, d))
    return out.reshape(bsz, seq, d)
```

```python
import functools

import jax
import jax.numpy as jnp
from jax import lax
from jax.experimental import pallas as pl
from jax.experimental.pallas import tpu as pltpu

F32 = jnp.float32
BF16 = jnp.bfloat16

GRID_W = 64
CONV_CH = 512
CONV_WIDTH = 31
MLA_HEADS = 8
MLA_Q_RANK = 512
MLA_KV_RANK = 256
MLA_NOPE = 128
MLA_ROPE = 64
MLA_V = 128
GQA_HEADS = 8
GQA_KV_HEADS = 2
GQA_HEAD_DIM = 64
WINDOW = 128
N_BRANCH = 3
ROPE_DIM = 64
ROPE_QUARTER = ROPE_DIM // 4
ROPE_HALF = ROPE_DIM // 2
ROPE_BASE = 10000.0
EPS = 1e-6
NEG_INF = -1e30

COL_MLA_KV = 0
COL_MLA_KR = COL_MLA_KV + MLA_KV_RANK
COL_GQA_K = COL_MLA_KR + MLA_ROPE
COL_GQA_V = COL_GQA_K + GQA_KV_HEADS * GQA_HEAD_DIM
KV_COLS = COL_GQA_V + GQA_KV_HEADS * GQA_HEAD_DIM
COL_MLA_Q = KV_COLS
COL_GQA_Q = COL_MLA_Q + MLA_Q_RANK
COL_CONV = COL_GQA_Q + GQA_HEADS * GQA_HEAD_DIM
COL_GATE = COL_CONV + 2 * CONV_CH

LANES = 128
MLA_PAD = 256
VMEM_LIMIT = 56 * 1024 * 1024
MLA_TQ = 1024
MLA_RG = 64
MLA_TK = 2048
MLA_UNROLL = 2
LOG2_E = 1.4426950408889634

P_MLA_Q = 0
P_GQA_Q = 512
P_CONV_A = 1024
P_CONV_B = 1536
P_CKV = 2048
P_KR = 2304
P_GQA_K = 2432
P_GQA_V = 2560
P_SMALL = 3072


def _cparams(*sem):
    return pltpu.CompilerParams(dimension_semantics=sem, vmem_limit_bytes=VMEM_LIMIT)


def _for_chunks(n_rows, rc, fn):
    n = n_rows // rc
    if n == 1:
        fn(0)
        return

    def body(i, carry):
        fn(pl.multiple_of(i * rc, rc))
        return carry

    lax.fori_loop(0, n, body, 0)


def _sigmoid(x):
    return 0.5 * jnp.tanh(0.5 * x) + 0.5


def _rope(x, cos, sin):
    return x * cos + pltpu.roll(x, LANES // 2, 1) * sin


def _dot(a, b):
    return jnp.dot(a, b, preferred_element_type=F32)


def _dot_nt(a, b):
    return lax.dot_general(a, b, (((1,), (1,)), ((), ())), preferred_element_type=F32)


def _ada_kernel(c_ref, w_ref, b_ref, o_ref):
    c = c_ref[...]
    s = c * _sigmoid(c)
    o_ref[...] = jnp.dot(s, w_ref[...], preferred_element_type=F32,
                         precision=lax.Precision.HIGHEST) + b_ref[...]


def _ada(cc, w_ada, b_ada):
    depth, d, n = w_ada.shape
    rows = cc.shape[0]
    tn = 1024
    return pl.pallas_call(
        _ada_kernel,
        grid=(depth, n // tn),
        in_specs=[
            pl.BlockSpec((rows, d), lambda l, j: (0, 0)),
            pl.BlockSpec((None, d, tn), lambda l, j: (l, 0, j)),
            pl.BlockSpec((None, 1, tn), lambda l, j: (l, 0, j)),
        ],
        out_specs=pl.BlockSpec((None, rows, tn), lambda l, j: (l, 0, j)),
        out_shape=jax.ShapeDtypeStruct((depth, rows, n), F32),
        compiler_params=_cparams("parallel", "parallel"),
        name="ada_mod",
    )(cc, w_ada, b_ada.reshape(depth, 1, n))


def _norm_mod_store(x_ref, g_ref, sh_ref, sc_ref, h_ref, rc):
    gs = g_ref[...] * (1.0 + sc_ref[...])
    sh = sh_ref[...]

    def chunk(r0):
        x = x_ref[pl.ds(r0, rc), :]
        ms = jnp.mean(x * x, axis=-1, keepdims=True)
        h_ref[pl.ds(r0, rc), :] = (x * lax.rsqrt(ms + EPS) * gs + sh).astype(BF16)

    _for_chunks(x_ref.shape[0], rc, chunk)


def _in_proj_kernel(x_ref, g_ref, sh_ref, sc_ref, w_ref, *rest, n_gate, col0, rc):
    o_ref, h_ref = rest[-2:]
    j = pl.program_id(1)

    @pl.when(j == 0)
    def _():
        _norm_mod_store(x_ref, g_ref, sh_ref, sc_ref, h_ref, rc)

    acc = _dot(h_ref[...], w_ref[...])
    o_ref[...] = jnp.where(col0 + j < n_gate, _sigmoid(acc), acc).astype(BF16)


def _ffn_in_kernel(h_ref, w1_ref, w2_ref, o_ref):
    h = h_ref[...]
    u1 = _dot(h, w1_ref[...])
    u2 = _dot(h, w2_ref[...])
    o_ref[...] = (u1 * _sigmoid(u1) * u2).astype(BF16)


def _mm_res_kernel(a_ref, w_ref, x_ref, gt_ref, o_ref):
    o_ref[...] = x_ref[...] + gt_ref[...] * _dot(a_ref[...], w_ref[...])


def _out_proj_kernel(a_ref, w_ref, x_ref, gt_ref, g_ref, sh_ref, sc_ref, *rest):
    o_ref, h_ref = rest[-2:]
    gs = g_ref[...] * (1.0 + sc_ref[...])
    rows = a_ref.shape[0] // 2
    for r in (0, rows):
        x_new = x_ref[r:r + rows, :] + gt_ref[...] * _dot(a_ref[r:r + rows, :], w_ref[...])
        o_ref[r:r + rows, :] = x_new
        ms = jnp.mean(x_new * x_new, axis=-1, keepdims=True)
        h_ref[r:r + rows, :] = (x_new * lax.rsqrt(ms + EPS) * gs + sh_ref[...]).astype(BF16)


def _merge_kernel(oc_ref, om_ref, og_ref, wc_ref, wm_ref, wg_ref, g0_ref, g1_ref, g2_ref, y_ref):
    y = g0_ref[...].astype(F32) * _dot(oc_ref[...], wc_ref[...])
    y += g1_ref[...].astype(F32) * _dot(om_ref[...], wm_ref[...])
    y += g2_ref[...].astype(F32) * _dot(og_ref[...], wg_ref[...])
    y_ref[...] = y.astype(BF16)


def _rms_bf16(a, g):
    ms = jnp.mean(a * a, axis=-1, keepdims=True)
    return (a * lax.rsqrt(ms + EPS) * g).astype(BF16)


def _qup_kernel(pq_ref, g_ref, wq_ref, cos_ref, sin_ref, q_ref, qg_ref, *, rc, s_mla, s_gqa):
    g = g_ref[...]

    def chunk(r0):
        rows = pl.ds(r0, rc)
        n = _rms_bf16(pq_ref[rows, 0:MLA_Q_RANK].astype(F32), g)
        q = _dot(n, wq_ref[...])
        cos = cos_ref[rows, :]
        sin = sin_ref[rows, :]
        for h in range(MLA_HEADS):
            c0 = h * MLA_PAD
            q_ref[rows, c0:c0 + LANES] = (q[:, c0:c0 + LANES] * s_mla).astype(BF16)
            r = _rope(q[:, c0 + LANES:c0 + 2 * LANES], cos, sin)
            q_ref[rows, c0 + LANES:c0 + 2 * LANES] = (r * s_mla).astype(BF16)
        for t in range(GQA_HEADS * GQA_HEAD_DIM // LANES):
            gq = pq_ref[rows, MLA_Q_RANK + t * LANES:MLA_Q_RANK + (t + 1) * LANES].astype(F32)
            qg_ref[rows, t * LANES:(t + 1) * LANES] = (_rope(gq, cos, sin) * s_gqa).astype(BF16)

    _for_chunks(pq_ref.shape[0], rc, chunk)


def _kvup_kernel(ckv_ref, kr_ref, g_ref, wk_ref, wv_ref, cos_ref, sin_ref,
                 k_ref, krope_ref, v_ref, kg_ref, vg_ref, *, rc):
    g = g_ref[...]

    def chunk(r0):
        rows = pl.ds(r0, rc)
        n = _rms_bf16(ckv_ref[rows, :].astype(F32), g)
        k_ref[rows, :] = _dot(n, wk_ref[...]).astype(BF16)
        v_ref[rows, :] = _dot(n, wv_ref[...]).astype(BF16)
        cos = cos_ref[rows, :]
        sin = sin_ref[rows, :]
        krope_ref[rows, :] = _rope(kr_ref[rows, 0:LANES].astype(F32), cos, sin).astype(BF16)
        lane = lax.broadcasted_iota(jnp.int32, (rc, LANES), 1)
        even_quarter = (lane // ROPE_HALF) % 2 == 0
        gk = _rope(kr_ref[rows, LANES:2 * LANES].astype(F32), cos, sin)
        kg_ref[rows, 0:LANES] = jnp.where(even_quarter, gk, pltpu.roll(gk, ROPE_HALF, 1)).astype(BF16)
        kg_ref[rows, LANES:2 * LANES] = jnp.where(even_quarter, pltpu.roll(gk, LANES - ROPE_HALF, 1),
                                                   gk).astype(BF16)
        low_half = lane < GQA_HEAD_DIM
        gv = kr_ref[rows, 2 * LANES:3 * LANES].astype(F32)
        gv_sw = pltpu.roll(gv, GQA_HEAD_DIM, 1)
        vg_ref[rows, 0:LANES] = jnp.where(low_half, gv, gv_sw).astype(BF16)
        vg_ref[rows, LANES:2 * LANES] = jnp.where(low_half, gv_sw, gv).astype(BF16)

    _for_chunks(ckv_ref.shape[0], rc, chunk)


def _mla_attn_kernel(q_ref, k_ref, krope_ref, v_ref, *rest, tk, rg, unroll):
    n_buf = 3 * unroll
    o_ref = rest[-n_buf - 3]
    bufs = rest[-n_buf - 2:-2]
    s_buf, p_buf, a_buf = bufs[:unroll], bufs[unroll:2 * unroll], bufs[2 * unroll:]
    m_ref, acc_ref = rest[-2:]
    tq = q_ref.shape[0]
    head = k_ref.shape[0] % tk
    has_head = 1 if head else 0
    n = k_ref.shape[0] // tk + has_head

    def start_of(c):
        return head + (c - has_head) * tk

    def size_of(c):
        return head if (has_head and c == 0) else tk

    def qk(slot, start, size):
        k = jnp.concatenate([k_ref[pl.ds(start, size), :], krope_ref[pl.ds(start, size), :]], axis=1)
        s_buf[slot][:, 0:size] = _dot_nt(q_ref[...], k)

    def sm(slot, size):
        for r in range(0, tq, rg):
            s = s_buf[slot][r:r + rg, 0:size]
            m_old = m_ref[r:r + rg, :]
            m_new = jnp.maximum(m_old, jnp.max(s, axis=-1, keepdims=True))
            p_buf[slot][r:r + rg, 0:size] = jnp.exp2(s - pltpu.repeat(m_new, size // LANES, 1)).astype(BF16)
            a_buf[slot][r:r + rg, :] = jnp.exp2(m_old - m_new)
            m_ref[r:r + rg, :] = m_new

    def pv(slot, start, size):
        alpha = pltpu.repeat(a_buf[slot][...], acc_ref.shape[1] // LANES, 1)
        v = jnp.concatenate([v_ref[pl.ds(start, size), :], jnp.ones((size, LANES), BF16)], axis=1)
        acc_ref[...] = alpha * acc_ref[...] + _dot(p_buf[slot][:, 0:size], v)

    def static_start(c):
        return 0 if (has_head and c == 0) else start_of(c)

    def static_tick(t):
        if 0 <= t < n:
            qk(t % unroll, static_start(t), size_of(t))
        if 0 <= t - 2 < n:
            pv((t - 2) % unroll, static_start(t - 2), size_of(t - 2))
        if 0 <= t - 1 < n:
            sm((t - 1) % unroll, size_of(t - 1))

    m_ref[...] = jnp.full(m_ref.shape, NEG_INF, F32)
    acc_ref[...] = jnp.zeros(acc_ref.shape, F32)

    first_steady = 2 + has_head
    trips = max(n - first_steady, 0) // unroll
    for t in range(first_steady):
        static_tick(t)
    if trips:
        def trip(j, carry):
            for u in range(unroll):
                t = first_steady + unroll * j + u
                qk((first_steady + u) % unroll, pl.multiple_of(start_of(t), 2 * LANES), tk)
                pv((first_steady + u - 2) % unroll, pl.multiple_of(start_of(t - 2), 2 * LANES), tk)
                sm((first_steady + u - 1) % unroll, tk)
            return carry

        lax.fori_loop(0, trips, trip, 0)
    for t in range(first_steady + unroll * trips, n + 2):
        static_tick(t)
    acc = acc_ref[...]
    o_ref[...] = (acc[:, :MLA_V] / acc[:, MLA_V:MLA_V + 1]).astype(BF16)


def _gqa_attn_kernel(q_ref, kc_ref, vc_ref, sink_ref, *rest, seq, n_sub, latent):
    if latent:
        kl_ref, vl_ref = rest[0], rest[1]
    o_ref, s0, s1, p0, p1, e0, e1 = rest[-7:]
    s_buf, p_buf, e_buf = (s0, s1), (p0, p1), (e0, e1)
    span = 3 * WINDOW
    n_c = kc_ref.shape[0]
    rep = GQA_HEADS // GQA_KV_HEADS
    lane = lax.broadcasted_iota(jnp.int32, (1, LANES), 1)
    low_half = lane < GQA_HEAD_DIM
    even_quarter = (lane // ROPE_HALF) % 2 == 0
    half_masks = (jnp.where(even_quarter, 1.0, 0.0).astype(BF16), jnp.where(even_quarter, 0.0, 1.0).astype(BF16))
    sinks = sink_ref[...] * LOG2_E
    sinks = [sinks[:, h:h + 1] for h in range(GQA_HEADS)]

    def row0(r):
        return r * WINDOW if isinstance(r, int) else pl.multiple_of(r * WINDOW, WINDOW)

    def win_start(r):
        if isinstance(r, int):
            return min(max(r * WINDOW - WINDOW, 0), seq - span)
        return pl.multiple_of(jnp.clip(r * WINDOW - WINDOW, 0, seq - span), WINDOW)

    def qk(slot, r):
        rows = pl.ds(row0(r), WINDOW)
        for h in range(GQA_HEADS):
            g = h // rep
            q = q_ref[rows, (h // 2) * LANES:(h // 2 + 1) * LANES] * half_masks[h % 2]
            s_buf[slot][h, :, 0:n_c] = _dot_nt(q, kc_ref[:, g * LANES:(g + 1) * LANES])
            if latent:
                s_buf[slot][h, :, n_c:n_c + span] = _dot_nt(
                    q, kl_ref[pl.ds(win_start(r), span), g * LANES:(g + 1) * LANES])

    def sm(slot, r):
        if latent:
            qpos = r * WINDOW + lax.broadcasted_iota(jnp.int32, (WINDOW, span), 0)
            kpos = win_start(r) + lax.broadcasted_iota(jnp.int32, (WINDOW, span), 1)
            bias = jnp.where(jnp.abs(qpos - kpos) <= WINDOW, 0.0, NEG_INF)
        for h in range(GQA_HEADS):
            sc = s_buf[slot][h, :, 0:n_c]
            m = jnp.maximum(jnp.max(sc, axis=-1, keepdims=True), sinks[h])
            if latent:
                sw = s_buf[slot][h, :, n_c:n_c + span] + bias
                m = jnp.maximum(m, jnp.max(sw, axis=-1, keepdims=True))
                p_buf[slot][h, :, n_c:n_c + span] = jnp.exp2(sw - m).astype(BF16)
            p_buf[slot][h, :, 0:n_c] = jnp.exp2(sc - m).astype(BF16)
            e_buf[slot][h] = jnp.broadcast_to(jnp.exp2(sinks[h] - m), (WINDOW, LANES))

    def pv(slot, r):
        outs = []
        for h in range(GQA_HEADS):
            g = h // rep
            vc = jnp.concatenate([vc_ref[:, g * LANES:(g + 1) * LANES], jnp.ones((n_c, LANES), BF16)], axis=1)
            o = _dot(p_buf[slot][h, :, 0:n_c], vc)
            if latent:
                vw = jnp.concatenate([vl_ref[pl.ds(win_start(r), span), g * LANES:(g + 1) * LANES],
                                      jnp.ones((span, LANES), BF16)], axis=1)
                o = o + _dot(p_buf[slot][h, :, n_c:n_c + span], vw)
            outs.append(o[:, :LANES] / (o[:, LANES:] + e_buf[slot][h]))
        rows = pl.ds(row0(r), WINDOW)
        for t in range(GQA_HEADS // 2):
            o_ref[rows, t * LANES:(t + 1) * LANES] = jnp.where(low_half, outs[2 * t], outs[2 * t + 1]).astype(BF16)

    def tick(t, par):
        qk(par, t)
        pv(par, t - 2)
        sm(1 - par, t - 1)

    def static_tick(t):
        if 0 <= t < n_sub:
            qk(t % 2, t)
        if 0 <= t - 2 < n_sub:
            pv(t % 2, t - 2)
        if 0 <= t - 1 < n_sub:
            sm((t - 1) % 2, t - 1)

    trips = max(n_sub - 2, 0) // 2
    for t in range(2):
        static_tick(t)
    if trips:
        def trip(j, carry):
            for u in range(2):
                tick(2 + 2 * j + u, u)
            return carry

        lax.fori_loop(0, trips, trip, 0)
    for t in range(2 + 2 * trips, n_sub + 2):
        static_tick(t)


def _conv_kernel(a_ref, b_ref, w_ref, cb_ref, lg_ref, lb_ref, o_ref, u_ref, *, seq, rc):
    halo = 16
    u_ref[0:halo, :] = jnp.zeros((halo, CONV_CH), F32)
    u_ref[halo + seq:2 * halo + seq, :] = jnp.zeros((halo, CONV_CH), F32)

    def fill(r0):
        a = a_ref[pl.ds(r0, 256), :].astype(F32)
        b = b_ref[pl.ds(r0, 256), :].astype(F32)
        u_ref[pl.ds(halo + r0, 256), :] = a * _sigmoid(b)

    _for_chunks(seq, 256, fill)

    cb = cb_ref[...]
    lg = lg_ref[...]
    lb = lb_ref[...]

    half = CONV_CH // 2
    first_tap = halo - CONV_WIDTH // 2

    def conv(r0):
        parts = []
        for c0 in (0, half):
            acc = jnp.zeros((rc, half), F32) + cb[:, c0:c0 + half]
            for s in range(8):
                part = None
                for a in range(4):
                    k = 8 * a + s - first_tap
                    if 0 <= k < CONV_WIDTH:
                        rows = u_ref[pl.ds(pl.multiple_of(r0 + 8 * a, 8), rc + 8), c0:c0 + half]
                        term = w_ref[k:k + 1, c0:c0 + half] * rows
                        part = term if part is None else part + term
                acc = acc + part[s:s + rc]
            parts.append(acc)
        acc = jnp.concatenate(parts, axis=-1)
        mu = jnp.mean(acc, axis=-1, keepdims=True)
        xc = acc - mu
        var = jnp.mean(xc * xc, axis=-1, keepdims=True)
        y = xc * lax.rsqrt(var + EPS) * lg + lb
        o_ref[pl.ds(r0, rc), :] = (y * _sigmoid(y)).astype(BF16)

    _for_chunks(seq, rc, conv)


def _final_norm_kernel(x_ref, g_ref, o_ref, *, rc):
    g = g_ref[...]

    def chunk(r0):
        x = x_ref[pl.ds(r0, rc), :]
        ms = jnp.mean(x * x, axis=-1, keepdims=True)
        o_ref[pl.ds(r0, rc), :] = x * lax.rsqrt(ms + EPS) * g

    _for_chunks(x_ref.shape[0], rc, chunk)


def _rope_tables(length, extra):
    rows = length // GRID_W
    row = jnp.repeat(jnp.arange(rows), GRID_W).astype(F32)
    col = jnp.tile(jnp.arange(GRID_W), rows).astype(F32)
    n_freq = ROPE_DIM // 4
    inv_freq = ROPE_BASE ** (-jnp.arange(n_freq, dtype=F32) / n_freq)
    a_row = row[:, None] * inv_freq[None, :]
    a_col = col[:, None] * inv_freq[None, :]
    ang = jnp.concatenate([a_row, a_col], axis=-1)
    cos = jnp.concatenate([jnp.cos(ang), jnp.ones((extra, ROPE_HALF), F32)], axis=0)
    sin = jnp.concatenate([jnp.sin(ang), jnp.zeros((extra, ROPE_HALF), F32)], axis=0)
    cos = jnp.tile(cos, (1, LANES // ROPE_HALF))
    sin = jnp.concatenate([-sin, -sin, sin, sin], axis=-1)
    return cos, sin


def _pick_tile(*sizes):
    for t in (1024, 512, 256):
        if all(s % t == 0 for s in sizes):
            return t
    raise ValueError(f"unsupported row counts {sizes}")


def kernel(x, c, ctx, c_ctx, w_ada, b_ada, g_mix, w_in, conv_w, conv_b, conv_ln_g, conv_ln_b, w_conv_out, g_q_a, w_q_up, g_kv_a, w_kv_up, w_mla_out, gqa_sink, w_gqa_out, w_out, g_ffn, w_ffn_in, w_ffn_out, g_final):
    bsz, seq, d = x.shape
    n_ctx = ctx.shape[1]
    depth = w_ada.shape[0]
    d_ff = w_ffn_out.shape[1]
    m_lat = bsz * seq
    m_ctx = bsz * n_ctx
    m_all = m_lat + m_ctx
    assert seq % GRID_W == 0 and n_ctx == 2 * WINDOW and seq % (4 * WINDOW) == 0
    ada_rows = -(-(bsz + 1) // 8) * 8
    assert w_in.shape[2] == COL_GATE + N_BRANCH * d

    tm = _pick_tile(seq, m_ctx)
    ts = min(tm, 512)
    tn = 512
    rc = 256
    n_lat_tiles = m_lat // tm
    tiles_per_batch = seq // tm
    p_base = N_BRANCH * d
    p_cols = p_base + P_SMALL
    tn_in = 1536
    assert p_base % tn_in == 0 and p_cols % tn_in == 0 and p_cols - tn_in <= p_base + P_CKV
    ctx_blk0 = m_lat // n_ctx

    def mod_row(i, tile):
        return jnp.minimum(i // (seq // tile), bsz)

    def cols(a, n):
        return w_in[:, :, a:a + n].astype(BF16)

    def zeros(n):
        return jnp.zeros((depth, d, n), BF16)

    def first_half(w):
        return jnp.concatenate([w[..., 0:ROPE_QUARTER], w[..., 2 * ROPE_QUARTER:3 * ROPE_QUARTER]], axis=-1)

    def second_half(w):
        return jnp.concatenate([w[..., ROPE_QUARTER:2 * ROPE_QUARTER], w[..., 3 * ROPE_QUARTER:]], axis=-1)

    def rope_single(w):
        z = jnp.zeros(w.shape[:-1] + (ROPE_HALF,), w.dtype)
        return jnp.concatenate([first_half(w), z, second_half(w), z], axis=-1)

    def rope_pairs(w):
        tiles = []
        for t in range(w.shape[-1] // LANES):
            h0 = w[..., t * LANES:t * LANES + ROPE_DIM]
            h1 = w[..., t * LANES + ROPE_DIM:(t + 1) * LANES]
            tiles += [first_half(h0), first_half(h1), second_half(h0), second_half(h1)]
        return jnp.concatenate(tiles, axis=-1)

    w_in_r = jnp.concatenate([
        cols(COL_GATE, N_BRANCH * d),
        cols(COL_MLA_Q, MLA_Q_RANK), rope_pairs(cols(COL_GQA_Q, GQA_HEADS * GQA_HEAD_DIM)),
        cols(COL_CONV, CONV_CH), cols(COL_CONV + CONV_CH, CONV_CH),
        cols(COL_MLA_KV, MLA_KV_RANK), rope_single(cols(COL_MLA_KR, MLA_ROPE)),
        rope_pairs(cols(COL_GQA_K, GQA_KV_HEADS * GQA_HEAD_DIM)), cols(COL_GQA_V, GQA_KV_HEADS * GQA_HEAD_DIM),
        zeros(P_SMALL - P_GQA_V - GQA_KV_HEADS * GQA_HEAD_DIM)], axis=-1)
    wq = w_q_up.reshape(depth, MLA_Q_RANK, MLA_HEADS, MLA_NOPE + MLA_ROPE).astype(BF16)
    wq = jnp.concatenate([wq[..., :MLA_NOPE], rope_single(wq[..., MLA_NOPE:])], axis=-1)
    wq = wq.reshape(depth, MLA_Q_RANK, MLA_HEADS * MLA_PAD)
    wkv = w_kv_up.reshape(depth, MLA_KV_RANK, MLA_HEADS, MLA_NOPE + MLA_V).astype(BF16)
    wk = wkv[..., :MLA_NOPE].reshape(depth, MLA_KV_RANK, MLA_HEADS * MLA_NOPE)
    wv = wkv[..., MLA_NOPE:].reshape(depth, MLA_KV_RANK, MLA_HEADS * MLA_V)
    w_conv_out_b = w_conv_out.astype(BF16)
    w_mla_out_b = w_mla_out.astype(BF16)
    w_gqa_out_b = w_gqa_out.astype(BF16)
    w_out_b = w_out.astype(BF16)
    w_ffn_in_b = w_ffn_in.astype(BF16)
    w_ffn_out_b = w_ffn_out.astype(BF16)
    cos, sin = _rope_tables(seq, ts)

    cc = jnp.zeros((ada_rows, d), F32).at[:bsz].set(c).at[bsz].set(c_ctx)
    mod = _ada(cc, w_ada, b_ada).reshape(depth, ada_rows, 1, 6 * d)

    x_lat = x.reshape(m_lat, d)
    x_ctx = ctx.reshape(m_ctx, d)
    xa = None

    def vec(a):
        return a.reshape(depth, 1, a.shape[-1])

    g_mix3, g_ffn3, g_q3, g_kv3 = vec(g_mix), vec(g_ffn), vec(g_q_a), vec(g_kv_a)
    conv_b3, ln_g3, ln_b3, sink3 = vec(conv_b), vec(conv_ln_g), vec(conv_ln_b), vec(gqa_sink)

    def mod_spec(l, part, tile, width, with_j, tile0=0):
        nb = d // width
        if with_j:
            return pl.BlockSpec((None, None, 1, width),
                                lambda i, j: (l, mod_row(tile0 + i, tile), 0, part * nb + j))
        return pl.BlockSpec((None, None, 1, width), lambda i, j: (l, mod_row(tile0 + i, tile), 0, part))

    def table_specs(tile):
        per_batch = seq // tile
        n_lat = m_lat // tile
        return [pl.BlockSpec((tile, LANES), lambda i: (jnp.where(i < n_lat, i % per_batch, per_batch), 0))] * 2

    for l in range(depth):
        last = l == depth - 1
        m_rows = m_lat if last else m_all

        lat_src, lat_tile0 = (x_lat, 0) if l == 0 else (xa, 0)
        ctx_src, ctx_tile0 = (x_ctx, 0) if l == 0 else (xa, n_lat_tiles)

        def in_proj_call(src, src_tile0, row_tile0, n_row_tiles, col_tile0, n_col_tiles, prev):
            in_specs = [
                pl.BlockSpec((tm, d), lambda i, j: (src_tile0 + i, 0)),
                pl.BlockSpec((None, 1, d), lambda i, j: (l, 0, 0)),
                mod_spec(l, 0, tm, d, False, row_tile0),
                mod_spec(l, 1, tm, d, False, row_tile0),
                pl.BlockSpec((None, d, tn_in), lambda i, j: (l, 0, col_tile0 + j)),
            ]
            args = [src, g_mix3, mod, mod, w_in_r]
            aliases = {}
            if prev is not None:
                in_specs.append(pl.BlockSpec(memory_space=pl.ANY))
                args.append(prev)
                aliases = {5: 0}
            return pl.pallas_call(
                functools.partial(_in_proj_kernel, n_gate=p_base // tn_in, col0=col_tile0, rc=rc),
                grid=(n_row_tiles, n_col_tiles),
                in_specs=in_specs,
                out_specs=pl.BlockSpec((tm, tn_in), lambda i, j: (row_tile0 + i, col_tile0 + j)),
                out_shape=jax.ShapeDtypeStruct((m_all, p_cols), BF16),
                scratch_shapes=[pltpu.VMEM((tm, d), BF16)],
                input_output_aliases=aliases,
                compiler_params=_cparams("parallel", "arbitrary"),
                name="in_proj" if prev is None else "in_proj_ctx",
            )(*args)

        n_col_tiles = p_cols // tn_in
        p = in_proj_call(lat_src, lat_tile0, 0, n_lat_tiles, 0, n_col_tiles, None)
        if last:
            p = in_proj_call(ctx_src, ctx_tile0, n_lat_tiles, m_ctx // tm, n_col_tiles - 1, 1, p)
        else:
            p = in_proj_call(ctx_src, ctx_tile0, n_lat_tiles, m_ctx // tm, 0, n_col_tiles, p)

        q_mla, q_gqa = pl.pallas_call(
            functools.partial(_qup_kernel, rc=rc, s_mla=float(LOG2_E * (MLA_NOPE + MLA_ROPE) ** -0.5),
                              s_gqa=float(LOG2_E * GQA_HEAD_DIM ** -0.5)),
            grid=(m_rows // ts,),
            in_specs=[
                pl.BlockSpec((ts, P_CONV_A - P_MLA_Q), lambda i: (i, (p_base + P_MLA_Q) // (P_CONV_A - P_MLA_Q))),
                pl.BlockSpec((None, 1, MLA_Q_RANK), lambda i: (l, 0, 0)),
                pl.BlockSpec((None, MLA_Q_RANK, MLA_HEADS * MLA_PAD), lambda i: (l, 0, 0)),
            ] + table_specs(ts),
            out_specs=[pl.BlockSpec((ts, MLA_HEADS * MLA_PAD), lambda i: (i, 0)),
                       pl.BlockSpec((ts, GQA_HEADS * GQA_HEAD_DIM), lambda i: (i, 0))],
            out_shape=[jax.ShapeDtypeStruct((m_all, MLA_HEADS * MLA_PAD), BF16),
                       jax.ShapeDtypeStruct((m_all, GQA_HEADS * GQA_HEAD_DIM), BF16)],
            compiler_params=_cparams("parallel"),
            name="q_up",
        )(p, g_q3, wq, cos, sin)

        tkv = n_ctx
        kv_per_batch = seq // tkv
        kv_lat_tiles = m_lat // tkv

        def kv_block(i):
            lat = (i // kv_per_batch) * (kv_per_batch + 1) + 1 + i % kv_per_batch
            return jnp.where(i < kv_lat_tiles, lat, (i - kv_lat_tiles) * (kv_per_batch + 1))

        k_mla, k_rope, v_mla, k_gqa, v_gqa = pl.pallas_call(
            functools.partial(_kvup_kernel, rc=tkv),
            grid=(m_all // tkv,),
            in_specs=[
                pl.BlockSpec((tkv, MLA_KV_RANK), lambda i: (i, (p_base + P_CKV) // MLA_KV_RANK)),
                pl.BlockSpec((tkv, 3 * LANES), lambda i: (i, (p_base + P_KR) // (3 * LANES))),
                pl.BlockSpec((None, 1, MLA_KV_RANK), lambda i: (l, 0, 0)),
                pl.BlockSpec((None, MLA_KV_RANK, MLA_HEADS * MLA_NOPE), lambda i: (l, 0, 0)),
                pl.BlockSpec((None, MLA_KV_RANK, MLA_HEADS * MLA_V), lambda i: (l, 0, 0)),
            ] + table_specs(tkv),
            out_specs=[pl.BlockSpec((tkv, MLA_HEADS * MLA_NOPE), lambda i: (kv_block(i), 0)),
                       pl.BlockSpec((tkv, LANES), lambda i: (kv_block(i), 0)),
                       pl.BlockSpec((tkv, MLA_HEADS * MLA_V), lambda i: (kv_block(i), 0)),
                       pl.BlockSpec((tkv, 2 * LANES), lambda i: (i, 0)),
                       pl.BlockSpec((tkv, 2 * LANES), lambda i: (i, 0))],
            out_shape=[jax.ShapeDtypeStruct((m_all, MLA_HEADS * MLA_NOPE), BF16),
                       jax.ShapeDtypeStruct((m_all, LANES), BF16),
                       jax.ShapeDtypeStruct((m_all, MLA_HEADS * MLA_V), BF16),
                       jax.ShapeDtypeStruct((m_all, 2 * LANES), BF16),
                       jax.ShapeDtypeStruct((m_all, 2 * LANES), BF16)],
            compiler_params=_cparams("parallel"),
            name="kv_up",
        )(p, p, g_kv3, wk, wv, cos, sin)

        tq = min(MLA_TQ, seq)
        nq = seq // tq
        def mla_scratch(rows, tk):
            return ([pltpu.VMEM((rows, tk), F32)] * MLA_UNROLL + [pltpu.VMEM((rows, tk), BF16)] * MLA_UNROLL
                    + [pltpu.VMEM((rows, LANES), F32)] * (MLA_UNROLL + 1) + [pltpu.VMEM((rows, MLA_PAD), F32)])

        o_mla = pl.pallas_call(
            functools.partial(_mla_attn_kernel, tk=MLA_TK, rg=MLA_RG, unroll=MLA_UNROLL),
            grid=(bsz, MLA_HEADS, nq),
            scratch_shapes=mla_scratch(tq, MLA_TK),
            in_specs=[
                pl.BlockSpec((tq, MLA_PAD), lambda b, h, i: (b * nq + i, h)),
                pl.BlockSpec((seq + n_ctx, MLA_NOPE), lambda b, h, i: (b, h)),
                pl.BlockSpec((seq + n_ctx, LANES), lambda b, h, i: (b, 0)),
                pl.BlockSpec((seq + n_ctx, MLA_V), lambda b, h, i: (b, h)),
            ],
            out_specs=pl.BlockSpec((tq, MLA_V), lambda b, h, i: (b * nq + i, h)),
            out_shape=jax.ShapeDtypeStruct((m_all, MLA_HEADS * MLA_V), BF16),
            compiler_params=_cparams("parallel", "parallel", "arbitrary"),
            name="mla_attn",
        )(q_mla, k_mla, k_rope, v_mla)
        o_mla = o_mla if last else pl.pallas_call(
            functools.partial(_mla_attn_kernel, tk=n_ctx, rg=MLA_RG, unroll=MLA_UNROLL),
            grid=(bsz, MLA_HEADS),
            scratch_shapes=mla_scratch(n_ctx, n_ctx),
            in_specs=[
                pl.BlockSpec((n_ctx, MLA_PAD), lambda b, h: (ctx_blk0 + b, h)),
                pl.BlockSpec((n_ctx, MLA_NOPE), lambda b, h: (b * (kv_per_batch + 1), h)),
                pl.BlockSpec((n_ctx, LANES), lambda b, h: (b * (kv_per_batch + 1), 0)),
                pl.BlockSpec((n_ctx, MLA_V), lambda b, h: (b * (kv_per_batch + 1), h)),
                pl.BlockSpec(memory_space=pl.ANY),
            ],
            out_specs=pl.BlockSpec((n_ctx, MLA_V), lambda b, h: (ctx_blk0 + b, h)),
            out_shape=jax.ShapeDtypeStruct((m_all, MLA_HEADS * MLA_V), BF16),
            input_output_aliases={4: 0},
            compiler_params=_cparams("parallel", "parallel"),
            name="mla_attn_ctx",
        )(q_mla, k_mla, k_rope, v_mla, o_mla)

        def gqa_scratch(cols):
            return ([pltpu.VMEM((GQA_HEADS, WINDOW, cols), F32)] * 2
                    + [pltpu.VMEM((GQA_HEADS, WINDOW, cols), BF16)] * 2
                    + [pltpu.VMEM((GQA_HEADS, WINDOW, LANES), F32)] * 2)

        o_gqa = pl.pallas_call(
            functools.partial(_gqa_attn_kernel, seq=seq, n_sub=seq // WINDOW, latent=True),
            grid=(bsz,),
            in_specs=[
                pl.BlockSpec((seq, GQA_HEADS * GQA_HEAD_DIM), lambda b: (b, 0)),
                pl.BlockSpec((n_ctx, 2 * LANES), lambda b: (ctx_blk0 + b, 0)),
                pl.BlockSpec((n_ctx, 2 * LANES), lambda b: (ctx_blk0 + b, 0)),
                pl.BlockSpec((None, 1, GQA_HEADS), lambda b: (l, 0, 0)),
                pl.BlockSpec((seq, 2 * LANES), lambda b: (b, 0)),
                pl.BlockSpec((seq, 2 * LANES), lambda b: (b, 0)),
            ],
            out_specs=pl.BlockSpec((seq, GQA_HEADS * GQA_HEAD_DIM), lambda b: (b, 0)),
            out_shape=jax.ShapeDtypeStruct((m_all, GQA_HEADS * GQA_HEAD_DIM), BF16),
            scratch_shapes=gqa_scratch(n_ctx + 3 * WINDOW),
            compiler_params=_cparams("parallel"),
            name="gqa_attn",
        )(q_gqa, k_gqa, v_gqa, sink3, k_gqa, v_gqa)
        o_gqa = o_gqa if last else pl.pallas_call(
            functools.partial(_gqa_attn_kernel, seq=n_ctx, n_sub=n_ctx // WINDOW, latent=False),
            grid=(bsz,),
            scratch_shapes=gqa_scratch(n_ctx),
            in_specs=[
                pl.BlockSpec((n_ctx, GQA_HEADS * GQA_HEAD_DIM), lambda b: (ctx_blk0 + b, 0)),
                pl.BlockSpec((n_ctx, 2 * LANES), lambda b: (ctx_blk0 + b, 0)),
                pl.BlockSpec((n_ctx, 2 * LANES), lambda b: (ctx_blk0 + b, 0)),
                pl.BlockSpec((None, 1, GQA_HEADS), lambda b: (l, 0, 0)),
                pl.BlockSpec(memory_space=pl.ANY),
            ],
            out_specs=pl.BlockSpec((n_ctx, GQA_HEADS * GQA_HEAD_DIM), lambda b: (ctx_blk0 + b, 0)),
            out_shape=jax.ShapeDtypeStruct((m_all, GQA_HEADS * GQA_HEAD_DIM), BF16),
            input_output_aliases={4: 0},
            compiler_params=_cparams("parallel"),
            name="gqa_attn_ctx",
        )(q_gqa, k_gqa, v_gqa, sink3, o_gqa)

        def conv_call(length, blk0, prev):
            crc = 64
            in_specs = [
                pl.BlockSpec((length, CONV_CH), lambda b: (blk0 + b, (p_base + P_CONV_A) // CONV_CH)),
                pl.BlockSpec((length, CONV_CH), lambda b: (blk0 + b, (p_base + P_CONV_B) // CONV_CH)),
                pl.BlockSpec((None, CONV_WIDTH, CONV_CH), lambda b: (l, 0, 0)),
                pl.BlockSpec((None, 1, CONV_CH), lambda b: (l, 0, 0)),
                pl.BlockSpec((None, 1, CONV_CH), lambda b: (l, 0, 0)),
                pl.BlockSpec((None, 1, CONV_CH), lambda b: (l, 0, 0)),
            ]
            args = [p, p, conv_w, conv_b3, ln_g3, ln_b3]
            aliases = {}
            if prev is not None:
                in_specs.append(pl.BlockSpec(memory_space=pl.ANY))
                args.append(prev)
                aliases = {6: 0}

            def body(a_ref, b_ref, w_ref, cb_ref, lg_ref, lb_ref, *rest):
                o_ref, u_ref = rest[-2:]
                _conv_kernel(a_ref, b_ref, w_ref, cb_ref, lg_ref, lb_ref, o_ref, u_ref, seq=length, rc=crc)

            return pl.pallas_call(
                body,
                grid=(bsz,),
                in_specs=in_specs,
                out_specs=pl.BlockSpec((length, CONV_CH), lambda b: (blk0 + b, 0)),
                out_shape=jax.ShapeDtypeStruct((m_all, CONV_CH), BF16),
                scratch_shapes=[pltpu.VMEM((length + 32, CONV_CH), F32)],
                input_output_aliases=aliases,
                compiler_params=_cparams("parallel"),
                name="conv_branch" if prev is None else "conv_branch_ctx",
            )(*args)

        o_conv = conv_call(seq, 0, None)
        if not last:
            o_conv = conv_call(n_ctx, ctx_blk0, o_conv)

        y = pl.pallas_call(
            _merge_kernel,
            grid=(m_rows // ts, 1),
            in_specs=[
                pl.BlockSpec((ts, CONV_CH), lambda i, j: (i, 0)),
                pl.BlockSpec((ts, MLA_HEADS * MLA_V), lambda i, j: (i, 0)),
                pl.BlockSpec((ts, GQA_HEADS * GQA_HEAD_DIM), lambda i, j: (i, 0)),
                pl.BlockSpec((None, CONV_CH, d), lambda i, j: (l, 0, 0)),
                pl.BlockSpec((None, MLA_HEADS * MLA_V, d), lambda i, j: (l, 0, 0)),
                pl.BlockSpec((None, GQA_HEADS * GQA_HEAD_DIM, d), lambda i, j: (l, 0, 0)),
                pl.BlockSpec((ts, d), lambda i, j: (i, 0)),
                pl.BlockSpec((ts, d), lambda i, j: (i, 1)),
                pl.BlockSpec((ts, d), lambda i, j: (i, 2)),
            ],
            out_specs=pl.BlockSpec((ts, d), lambda i, j: (i, 0)),
            out_shape=jax.ShapeDtypeStruct((m_all, d), BF16),
            compiler_params=_cparams("parallel", "parallel"),
            name="merge",
        )(o_conv, o_mla, o_gqa, w_conv_out_b, w_mla_out_b, w_gqa_out_b, p, p, p)

        def mm_res(a, w, xin, part, tile, width, name):
            kdim = a.shape[1]
            return pl.pallas_call(
                _mm_res_kernel,
                grid=(m_rows // tile, d // width),
                in_specs=[
                    pl.BlockSpec((tile, kdim), lambda i, j: (i, 0)),
                    pl.BlockSpec((None, kdim, width), lambda i, j: (l, 0, j)),
                    pl.BlockSpec((tile, width), lambda i, j: (i, j)),
                    mod_spec(l, part, tile, width, True),
                ],
                out_specs=pl.BlockSpec((tile, width), lambda i, j: (i, j)),
                out_shape=jax.ShapeDtypeStruct((m_all, d), F32),
                compiler_params=_cparams("parallel", "parallel"),
                name=name,
            )(a, w, xin, mod)

        def out_proj_call(res_src, res_tile0, row_tile0, n_row_tiles, prev):
            in_specs = [
                pl.BlockSpec((ts, d), lambda i, j: (row_tile0 + i, 0)),
                pl.BlockSpec((None, d, d), lambda i, j: (l, 0, 0)),
                pl.BlockSpec((ts, d), lambda i, j: (res_tile0 + i, 0)),
                mod_spec(l, 2, ts, d, False, row_tile0),
                pl.BlockSpec((None, 1, d), lambda i, j: (l, 0, 0)),
                mod_spec(l, 3, ts, d, False, row_tile0),
                mod_spec(l, 4, ts, d, False, row_tile0),
            ]
            args = [y, w_out_b, res_src, mod, g_ffn3, mod, mod]
            aliases = {}
            if prev is not None:
                in_specs += [pl.BlockSpec(memory_space=pl.ANY)] * 2
                args += list(prev)
                aliases = {7: 0, 8: 1}
            return pl.pallas_call(
                _out_proj_kernel,
                grid=(n_row_tiles, 1),
                in_specs=in_specs,
                out_specs=[pl.BlockSpec((ts, d), lambda i, j: (row_tile0 + i, 0))] * 2,
                out_shape=[jax.ShapeDtypeStruct((m_all, d), F32), jax.ShapeDtypeStruct((m_all, d), BF16)],
                input_output_aliases=aliases,
                compiler_params=_cparams("parallel", "parallel"),
                name="out_proj" if prev is None else "out_proj_ctx",
            )(*args)

        ts_per_tm = tm // ts
        xa, h_ffn = out_proj_call(lat_src, lat_tile0 * ts_per_tm, 0, m_lat // ts, None)
        if not last:
            xa, h_ffn = out_proj_call(ctx_src, ctx_tile0 * ts_per_tm, m_lat // ts, m_ctx // ts, (xa, h_ffn))

        nf = d_ff // tn
        act = pl.pallas_call(
            _ffn_in_kernel,
            grid=(m_rows // tm, nf),
            in_specs=[
                pl.BlockSpec((tm, d), lambda i, j: (i, 0)),
                pl.BlockSpec((None, d, tn), lambda i, j: (l, 0, j)),
                pl.BlockSpec((None, d, tn), lambda i, j: (l, 0, nf + j)),
            ],
            out_specs=pl.BlockSpec((tm, tn), lambda i, j: (i, j)),
            out_shape=jax.ShapeDtypeStruct((m_all, d_ff), BF16),
            compiler_params=_cparams("parallel", "parallel"),
            name="ffn_in",
        )(h_ffn, w_ffn_in_b, w_ffn_in_b)
        xa = mm_res(act, w_ffn_out_b, xa, 5, tm, tn, "ffn_out")

    out = pl.pallas_call(
        functools.partial(_final_norm_kernel, rc=rc),
        grid=(m_lat // tm,),
        in_specs=[pl.BlockSpec((tm, d), lambda i: (i, 0)),
                  pl.BlockSpec((1, d), lambda i: (0, 0))],
        out_specs=pl.BlockSpec((tm, d), lambda i: (i, 0)),
        out_shape=jax.ShapeDtypeStruct((m_lat, d), F32),
        compiler_params=_cparams("parallel"),
        name="final_norm",
    )(xa, g_final.reshape(1, d))
    return out.reshape(bsz, seq, d)
```

```python
import functools

import jax
import jax.numpy as jnp
from jax import lax
from jax.experimental import pallas as pl
from jax.experimental.pallas import tpu as pltpu

F32 = jnp.float32
BF16 = jnp.bfloat16

GRID_W = 64
CONV_CH = 512
CONV_WIDTH = 31
MLA_HEADS = 8
MLA_Q_RANK = 512
MLA_KV_RANK = 256
MLA_NOPE = 128
MLA_ROPE = 64
MLA_V = 128
GQA_HEADS = 8
GQA_KV_HEADS = 2
GQA_HEAD_DIM = 64
WINDOW = 128
N_BRANCH = 3
ROPE_DIM = 64
ROPE_BASE = 10000.0
EPS = 1e-6
NEG_INF = -1e30

COL_MLA_KV = 0
COL_MLA_KR = COL_MLA_KV + MLA_KV_RANK
COL_GQA_K = COL_MLA_KR + MLA_ROPE
COL_GQA_V = COL_GQA_K + GQA_KV_HEADS * GQA_HEAD_DIM
KV_COLS = COL_GQA_V + GQA_KV_HEADS * GQA_HEAD_DIM
COL_MLA_Q = KV_COLS
COL_GQA_Q = COL_MLA_Q + MLA_Q_RANK
COL_CONV = COL_GQA_Q + GQA_HEADS * GQA_HEAD_DIM
COL_GATE = COL_CONV + 2 * CONV_CH

LANES = 128
MLA_PAD = 256
VMEM_LIMIT = 56 * 1024 * 1024
MLA_TQ = 1024
MLA_RG = 64
MLA_TK = 2048
MLA_UNROLL = 2
LOG2_E = 1.4426950408889634

P_MLA_Q = 0
P_GQA_Q = 512
P_CONV_A = 1024
P_CONV_B = 1536
P_CKV = 2048
P_KR = 2304
P_GQA_K = 2432
P_GQA_V = 2560
P_SMALL = 3072


def _cparams(*sem):
    return pltpu.CompilerParams(dimension_semantics=sem, vmem_limit_bytes=VMEM_LIMIT)


def _for_chunks(n_rows, rc, fn):
    n = n_rows // rc
    if n == 1:
        fn(0)
        return

    def body(i, carry):
        fn(pl.multiple_of(i * rc, rc))
        return carry

    lax.fori_loop(0, n, body, 0)


def _sigmoid(x):
    return 0.5 * jnp.tanh(0.5 * x) + 0.5


def _rope(x, cos, sin_up, sin_dn):
    return x * cos + pltpu.roll(x, LANES - 16, 1) * sin_up + pltpu.roll(x, 16, 1) * sin_dn


def _dot(a, b):
    return jnp.dot(a, b, preferred_element_type=F32)


def _dot_nt(a, b):
    return lax.dot_general(a, b, (((1,), (1,)), ((), ())), preferred_element_type=F32)


def _ada_kernel(c_ref, w_ref, b_ref, o_ref):
    c = c_ref[...]
    s = c * _sigmoid(c)
    o_ref[...] = jnp.dot(s, w_ref[...], preferred_element_type=F32,
                         precision=lax.Precision.HIGHEST) + b_ref[...]


def _ada(cc, w_ada, b_ada):
    depth, d, n = w_ada.shape
    rows = cc.shape[0]
    tn = 1024
    return pl.pallas_call(
        _ada_kernel,
        grid=(depth, n // tn),
        in_specs=[
            pl.BlockSpec((rows, d), lambda l, j: (0, 0)),
            pl.BlockSpec((None, d, tn), lambda l, j: (l, 0, j)),
            pl.BlockSpec((None, 1, tn), lambda l, j: (l, 0, j)),
        ],
        out_specs=pl.BlockSpec((None, rows, tn), lambda l, j: (l, 0, j)),
        out_shape=jax.ShapeDtypeStruct((depth, rows, n), F32),
        compiler_params=_cparams("parallel", "parallel"),
        name="ada_mod",
    )(cc, w_ada, b_ada.reshape(depth, 1, n))


def _norm_mod_store(x_ref, g_ref, sh_ref, sc_ref, h_ref, rc):
    gs = g_ref[...] * (1.0 + sc_ref[...])
    sh = sh_ref[...]

    def chunk(r0):
        x = x_ref[pl.ds(r0, rc), :]
        ms = jnp.mean(x * x, axis=-1, keepdims=True)
        h_ref[pl.ds(r0, rc), :] = (x * lax.rsqrt(ms + EPS) * gs + sh).astype(BF16)

    _for_chunks(x_ref.shape[0], rc, chunk)


def _in_proj_kernel(x_ref, g_ref, sh_ref, sc_ref, w_ref, *rest, n_gate, col0, rc):
    o_ref, h_ref = rest[-2:]
    j = pl.program_id(1)

    @pl.when(j == 0)
    def _():
        _norm_mod_store(x_ref, g_ref, sh_ref, sc_ref, h_ref, rc)

    acc = _dot(h_ref[...], w_ref[...])
    o_ref[...] = jnp.where(col0 + j < n_gate, _sigmoid(acc), acc).astype(BF16)


def _ffn_in_kernel(h_ref, w1_ref, w2_ref, o_ref):
    h = h_ref[...]
    u1 = _dot(h, w1_ref[...])
    u2 = _dot(h, w2_ref[...])
    o_ref[...] = (u1 * _sigmoid(u1) * u2).astype(BF16)


def _mm_res_kernel(a_ref, w_ref, x_ref, gt_ref, o_ref):
    o_ref[...] = x_ref[...] + gt_ref[...] * _dot(a_ref[...], w_ref[...])


def _ffn_out_final_kernel(a_ref, w_ref, x_ref, gt_ref, gf_ref, o_ref, acc_ref, *, rc):
    k = pl.program_id(1)

    @pl.when(k == 0)
    def _():
        acc_ref[...] = jnp.zeros(acc_ref.shape, F32)

    acc_ref[...] += _dot(a_ref[...], w_ref[...])

    @pl.when(k == pl.num_programs(1) - 1)
    def _():
        gt = gt_ref[...]
        gf = gf_ref[...]

        def chunk(r0):
            x = x_ref[pl.ds(r0, rc), :] + gt * acc_ref[pl.ds(r0, rc), :]
            ms = jnp.mean(x * x, axis=-1, keepdims=True)
            o_ref[pl.ds(r0, rc), :] = x * lax.rsqrt(ms + EPS) * gf

        _for_chunks(x_ref.shape[0], rc, chunk)


def _out_proj_kernel(a_ref, w_ref, x_ref, gt_ref, g_ref, sh_ref, sc_ref, *rest):
    o_ref, h_ref = rest[-2:]
    gs = g_ref[...] * (1.0 + sc_ref[...])
    rows = a_ref.shape[0] // 2
    for r in (0, rows):
        x_new = x_ref[r:r + rows, :] + gt_ref[...] * _dot(a_ref[r:r + rows, :], w_ref[...])
        o_ref[r:r + rows, :] = x_new
        ms = jnp.mean(x_new * x_new, axis=-1, keepdims=True)
        h_ref[r:r + rows, :] = (x_new * lax.rsqrt(ms + EPS) * gs + sh_ref[...]).astype(BF16)


def _merge_kernel(oc_ref, om_ref, og_ref, wc_ref, wm_ref, wg_ref, g0_ref, g1_ref, g2_ref, y_ref):
    y = g0_ref[...].astype(F32) * _dot(oc_ref[...], wc_ref[...])
    y += g1_ref[...].astype(F32) * _dot(om_ref[...], wm_ref[...])
    y += g2_ref[...].astype(F32) * _dot(og_ref[...], wg_ref[...])
    y_ref[...] = y.astype(BF16)


def _rms_bf16(a, g):
    ms = jnp.mean(a * a, axis=-1, keepdims=True)
    return (a * lax.rsqrt(ms + EPS) * g).astype(BF16)


def _qup_kernel(pq_ref, g_ref, wq_ref, cos_ref, su_ref, sd_ref, q_ref, qg_ref, *, rc, s_mla, s_gqa):
    g = g_ref[...]

    def chunk(r0):
        rows = pl.ds(r0, rc)
        n = _rms_bf16(pq_ref[rows, 0:MLA_Q_RANK].astype(F32), g)
        q = _dot(n, wq_ref[...])
        cos = cos_ref[rows, :]
        su = su_ref[rows, :]
        sd = sd_ref[rows, :]
        for h in range(MLA_HEADS):
            c0 = h * MLA_PAD
            q_ref[rows, c0:c0 + LANES] = (q[:, c0:c0 + LANES] * s_mla).astype(BF16)
            r = _rope(q[:, c0 + LANES:c0 + 2 * LANES], cos, su, sd)
            q_ref[rows, c0 + LANES:c0 + 2 * LANES] = (r * s_mla).astype(BF16)
        for t in range(GQA_HEADS * GQA_HEAD_DIM // LANES):
            gq = pq_ref[rows, MLA_Q_RANK + t * LANES:MLA_Q_RANK + (t + 1) * LANES].astype(F32)
            qg_ref[rows, t * LANES:(t + 1) * LANES] = (_rope(gq, cos, su, sd) * s_gqa).astype(BF16)

    _for_chunks(pq_ref.shape[0], rc, chunk)


def _kvup_kernel(ckv_ref, kr_ref, g_ref, wk_ref, wv_ref, cos_ref, su_ref, sd_ref,
                 k_ref, krope_ref, v_ref, kg_ref, vg_ref, *, rc):
    g = g_ref[...]

    def chunk(r0):
        rows = pl.ds(r0, rc)
        n = _rms_bf16(ckv_ref[rows, :].astype(F32), g)
        k_ref[rows, :] = _dot(n, wk_ref[...]).astype(BF16)
        v_ref[rows, :] = _dot(n, wv_ref[...]).astype(BF16)
        cos = cos_ref[rows, :]
        su = su_ref[rows, :]
        sd = sd_ref[rows, :]
        krope_ref[rows, :] = _rope(kr_ref[rows, 0:LANES].astype(F32), cos, su, sd).astype(BF16)
        low_half = lax.broadcasted_iota(jnp.int32, (rc, LANES), 1) < GQA_HEAD_DIM
        gk = _rope(kr_ref[rows, LANES:2 * LANES].astype(F32), cos, su, sd)
        gk_sw = pltpu.roll(gk, GQA_HEAD_DIM, 1)
        kg_ref[rows, 0:LANES] = jnp.where(low_half, gk, gk_sw).astype(BF16)
        kg_ref[rows, LANES:2 * LANES] = jnp.where(low_half, gk_sw, gk).astype(BF16)
        gv = kr_ref[rows, 2 * LANES:3 * LANES].astype(F32)
        gv_sw = pltpu.roll(gv, GQA_HEAD_DIM, 1)
        vg_ref[rows, 0:LANES] = jnp.where(low_half, gv, gv_sw).astype(BF16)
        vg_ref[rows, LANES:2 * LANES] = jnp.where(low_half, gv_sw, gv).astype(BF16)

    _for_chunks(ckv_ref.shape[0], rc, chunk)


def _mla_attn_kernel(q_ref, k_ref, krope_ref, v_ref, *rest, tk, rg, unroll):
    n_buf = 3 * unroll
    o_ref = rest[-n_buf - 3]
    bufs = rest[-n_buf - 2:-2]
    s_buf, p_buf, a_buf = bufs[:unroll], bufs[unroll:2 * unroll], bufs[2 * unroll:]
    m_ref, acc_ref = rest[-2:]
    tq = q_ref.shape[0]
    head = k_ref.shape[0] % tk
    has_head = 1 if head else 0
    n = k_ref.shape[0] // tk + has_head

    def start_of(c):
        return head + (c - has_head) * tk

    def size_of(c):
        return head if (has_head and c == 0) else tk

    def qk(slot, start, size):
        k = jnp.concatenate([k_ref[pl.ds(start, size), :], krope_ref[pl.ds(start, size), :]], axis=1)
        s_buf[slot][:, 0:size] = _dot_nt(q_ref[...], k)

    def sm(slot, size):
        for r in range(0, tq, rg):
            s = s_buf[slot][r:r + rg, 0:size]
            m_old = m_ref[r:r + rg, :]
            m_new = jnp.maximum(m_old, jnp.max(s, axis=-1, keepdims=True))
            p_buf[slot][r:r + rg, 0:size] = jnp.exp2(s - pltpu.repeat(m_new, size // LANES, 1)).astype(BF16)
            a_buf[slot][r:r + rg, :] = jnp.exp2(m_old - m_new)
            m_ref[r:r + rg, :] = m_new

    def pv(slot, start, size):
        alpha = pltpu.repeat(a_buf[slot][...], acc_ref.shape[1] // LANES, 1)
        v = jnp.concatenate([v_ref[pl.ds(start, size), :], jnp.ones((size, LANES), BF16)], axis=1)
        acc_ref[...] = alpha * acc_ref[...] + _dot(p_buf[slot][:, 0:size], v)

    def static_start(c):
        return 0 if (has_head and c == 0) else start_of(c)

    def static_tick(t):
        if 0 <= t < n:
            qk(t % unroll, static_start(t), size_of(t))
        if 0 <= t - 2 < n:
            pv((t - 2) % unroll, static_start(t - 2), size_of(t - 2))
        if 0 <= t - 1 < n:
            sm((t - 1) % unroll, size_of(t - 1))

    m_ref[...] = jnp.full(m_ref.shape, NEG_INF, F32)
    acc_ref[...] = jnp.zeros(acc_ref.shape, F32)

    first_steady = 2 + has_head
    trips = max(n - first_steady, 0) // unroll
    for t in range(first_steady):
        static_tick(t)
    if trips:
        def trip(j, carry):
            for u in range(unroll):
                t = first_steady + unroll * j + u
                qk((first_steady + u) % unroll, pl.multiple_of(start_of(t), 2 * LANES), tk)
                pv((first_steady + u - 2) % unroll, pl.multiple_of(start_of(t - 2), 2 * LANES), tk)
                sm((first_steady + u - 1) % unroll, tk)
            return carry

        lax.fori_loop(0, trips, trip, 0)
    for t in range(first_steady + unroll * trips, n + 2):
        static_tick(t)
    acc = acc_ref[...]
    o_ref[...] = (acc[:, :MLA_V] / acc[:, MLA_V:MLA_V + 1]).astype(BF16)


def _gqa_attn_kernel(q_ref, kc_ref, vc_ref, sink_ref, *rest, seq, n_sub, latent):
    if latent:
        kl_ref, vl_ref = rest[0], rest[1]
    o_ref, s0, s1, p0, p1, e0, e1 = rest[-7:]
    s_buf, p_buf, e_buf = (s0, s1), (p0, p1), (e0, e1)
    span = 3 * WINDOW
    n_c = kc_ref.shape[0]
    rep = GQA_HEADS // GQA_KV_HEADS
    lane = lax.broadcasted_iota(jnp.int32, (1, LANES), 1)
    low_half = lane < GQA_HEAD_DIM
    half_masks = (jnp.where(low_half, 1.0, 0.0).astype(BF16), jnp.where(low_half, 0.0, 1.0).astype(BF16))
    sinks = sink_ref[...] * LOG2_E
    sinks = [sinks[:, h:h + 1] for h in range(GQA_HEADS)]

    def row0(r):
        return r * WINDOW if isinstance(r, int) else pl.multiple_of(r * WINDOW, WINDOW)

    def win_start(r):
        if isinstance(r, int):
            return min(max(r * WINDOW - WINDOW, 0), seq - span)
        return pl.multiple_of(jnp.clip(r * WINDOW - WINDOW, 0, seq - span), WINDOW)

    def qk(slot, r):
        rows = pl.ds(row0(r), WINDOW)
        for h in range(GQA_HEADS):
            g = h // rep
            q = q_ref[rows, (h // 2) * LANES:(h // 2 + 1) * LANES] * half_masks[h % 2]
            s_buf[slot][h, :, 0:n_c] = _dot_nt(q, kc_ref[:, g * LANES:(g + 1) * LANES])
            if latent:
                s_buf[slot][h, :, n_c:n_c + span] = _dot_nt(
                    q, kl_ref[pl.ds(win_start(r), span), g * LANES:(g + 1) * LANES])

    def sm(slot, r):
        if latent:
            qpos = r * WINDOW + lax.broadcasted_iota(jnp.int32, (WINDOW, span), 0)
            kpos = win_start(r) + lax.broadcasted_iota(jnp.int32, (WINDOW, span), 1)
            bias = jnp.where(jnp.abs(qpos - kpos) <= WINDOW, 0.0, NEG_INF)
        for h in range(GQA_HEADS):
            sc = s_buf[slot][h, :, 0:n_c]
            m = jnp.maximum(jnp.max(sc, axis=-1, keepdims=True), sinks[h])
            if latent:
                sw = s_buf[slot][h, :, n_c:n_c + span] + bias
                m = jnp.maximum(m, jnp.max(sw, axis=-1, keepdims=True))
                p_buf[slot][h, :, n_c:n_c + span] = jnp.exp2(sw - m).astype(BF16)
            p_buf[slot][h, :, 0:n_c] = jnp.exp2(sc - m).astype(BF16)
            e_buf[slot][h] = jnp.broadcast_to(jnp.exp2(sinks[h] - m), (WINDOW, LANES))

    def pv(slot, r):
        outs = []
        for h in range(GQA_HEADS):
            g = h // rep
            vc = jnp.concatenate([vc_ref[:, g * LANES:(g + 1) * LANES], jnp.ones((n_c, LANES), BF16)], axis=1)
            o = _dot(p_buf[slot][h, :, 0:n_c], vc)
            if latent:
                vw = jnp.concatenate([vl_ref[pl.ds(win_start(r), span), g * LANES:(g + 1) * LANES],
                                      jnp.ones((span, LANES), BF16)], axis=1)
                o = o + _dot(p_buf[slot][h, :, n_c:n_c + span], vw)
            outs.append(o[:, :LANES] / (o[:, LANES:] + e_buf[slot][h]))
        rows = pl.ds(row0(r), WINDOW)
        for t in range(GQA_HEADS // 2):
            o_ref[rows, t * LANES:(t + 1) * LANES] = jnp.where(low_half, outs[2 * t], outs[2 * t + 1]).astype(BF16)

    def tick(t, par):
        qk(par, t)
        pv(par, t - 2)
        sm(1 - par, t - 1)

    def static_tick(t):
        if 0 <= t < n_sub:
            qk(t % 2, t)
        if 0 <= t - 2 < n_sub:
            pv(t % 2, t - 2)
        if 0 <= t - 1 < n_sub:
            sm((t - 1) % 2, t - 1)

    trips = max(n_sub - 2, 0) // 2
    for t in range(2):
        static_tick(t)
    if trips:
        def trip(j, carry):
            for u in range(2):
                tick(2 + 2 * j + u, u)
            return carry

        lax.fori_loop(0, trips, trip, 0)
    for t in range(2 + 2 * trips, n_sub + 2):
        static_tick(t)


def _conv_kernel(a_ref, b_ref, w_ref, cb_ref, lg_ref, lb_ref, o_ref, u_ref, *, seq, rc):
    halo = 16
    u_ref[0:halo, :] = jnp.zeros((halo, CONV_CH), F32)
    u_ref[halo + seq:2 * halo + seq, :] = jnp.zeros((halo, CONV_CH), F32)

    def fill(r0):
        a = a_ref[pl.ds(r0, 256), :].astype(F32)
        b = b_ref[pl.ds(r0, 256), :].astype(F32)
        u_ref[pl.ds(halo + r0, 256), :] = a * _sigmoid(b)

    _for_chunks(seq, 256, fill)

    cb = cb_ref[...]
    lg = lg_ref[...]
    lb = lb_ref[...]

    half = CONV_CH // 2
    first_tap = halo - CONV_WIDTH // 2

    def conv(r0):
        parts = []
        for c0 in (0, half):
            acc = jnp.zeros((rc, half), F32) + cb[:, c0:c0 + half]
            for s in range(8):
                part = None
                for a in range(4):
                    k = 8 * a + s - first_tap
                    if 0 <= k < CONV_WIDTH:
                        rows = u_ref[pl.ds(pl.multiple_of(r0 + 8 * a, 8), rc + 8), c0:c0 + half]
                        term = w_ref[k:k + 1, c0:c0 + half] * rows
                        part = term if part is None else part + term
                acc = acc + part[s:s + rc]
            parts.append(acc)
        acc = jnp.concatenate(parts, axis=-1)
        mu = jnp.mean(acc, axis=-1, keepdims=True)
        xc = acc - mu
        var = jnp.mean(xc * xc, axis=-1, keepdims=True)
        y = xc * lax.rsqrt(var + EPS) * lg + lb
        o_ref[pl.ds(r0, rc), :] = (y * _sigmoid(y)).astype(BF16)

    _for_chunks(seq, rc, conv)


def _rope_tables(length, extra):
    rows = length // GRID_W
    row = jnp.repeat(jnp.arange(rows), GRID_W).astype(F32)
    col = jnp.tile(jnp.arange(GRID_W), rows).astype(F32)
    n_freq = ROPE_DIM // 4
    inv_freq = ROPE_BASE ** (-jnp.arange(n_freq, dtype=F32) / n_freq)
    a_row = row[:, None] * inv_freq[None, :]
    a_col = col[:, None] * inv_freq[None, :]
    ang = jnp.concatenate([a_row, a_row, a_col, a_col], axis=-1)
    cos = jnp.concatenate([jnp.cos(ang), jnp.ones((extra, ROPE_DIM), F32)], axis=0)
    sin = jnp.concatenate([jnp.sin(ang), jnp.zeros((extra, ROPE_DIM), F32)], axis=0)
    cos = jnp.tile(cos, (1, LANES // ROPE_DIM))
    sin = jnp.tile(sin, (1, LANES // ROPE_DIM))
    first = (jnp.arange(LANES) // 16) % 2 == 0
    sin_up = jnp.where(first[None, :], -sin, 0.0)
    sin_dn = jnp.where(first[None, :], 0.0, sin)
    return cos, sin_up, sin_dn


def _pick_tile(*sizes):
    for t in (1024, 512, 256):
        if all(s % t == 0 for s in sizes):
            return t
    raise ValueError(f"unsupported row counts {sizes}")


def kernel(x, c, ctx, c_ctx, w_ada, b_ada, g_mix, w_in, conv_w, conv_b, conv_ln_g, conv_ln_b, w_conv_out, g_q_a, w_q_up, g_kv_a, w_kv_up, w_mla_out, gqa_sink, w_gqa_out, w_out, g_ffn, w_ffn_in, w_ffn_out, g_final):
    bsz, seq, d = x.shape
    n_ctx = ctx.shape[1]
    depth = w_ada.shape[0]
    d_ff = w_ffn_out.shape[1]
    m_lat = bsz * seq
    m_ctx = bsz * n_ctx
    m_all = m_lat + m_ctx
    assert seq % GRID_W == 0 and n_ctx == 2 * WINDOW and seq % (4 * WINDOW) == 0
    assert bsz + 1 <= 16 and w_in.shape[2] == COL_GATE + N_BRANCH * d

    tm = _pick_tile(seq, m_ctx)
    ts = min(tm, 512)
    tn = 512
    rc = 256
    n_lat_tiles = m_lat // tm
    tiles_per_batch = seq // tm
    p_base = N_BRANCH * d
    p_cols = p_base + P_SMALL
    tn_in = 1536
    assert p_base % tn_in == 0 and p_cols % tn_in == 0 and p_cols - tn_in <= p_base + P_CKV
    ctx_blk0 = m_lat // n_ctx

    def mod_row(i, tile):
        return jnp.minimum(i // (seq // tile), bsz)

    def cols(a, n):
        return w_in[:, :, a:a + n].astype(BF16)

    def zeros(n):
        return jnp.zeros((depth, d, n), BF16)

    w_in_r = jnp.concatenate([
        cols(COL_GATE, N_BRANCH * d),
        cols(COL_MLA_Q, MLA_Q_RANK), cols(COL_GQA_Q, GQA_HEADS * GQA_HEAD_DIM),
        cols(COL_CONV, CONV_CH), cols(COL_CONV + CONV_CH, CONV_CH),
        cols(COL_MLA_KV, MLA_KV_RANK), cols(COL_MLA_KR, MLA_ROPE), zeros(LANES - MLA_ROPE),
        cols(COL_GQA_K, GQA_KV_HEADS * GQA_HEAD_DIM), cols(COL_GQA_V, GQA_KV_HEADS * GQA_HEAD_DIM),
        zeros(P_SMALL - P_GQA_V - GQA_KV_HEADS * GQA_HEAD_DIM)], axis=-1)
    wq = w_q_up.reshape(depth, MLA_Q_RANK, MLA_HEADS, MLA_NOPE + MLA_ROPE).astype(BF16)
    wq = jnp.pad(wq, ((0, 0), (0, 0), (0, 0), (0, MLA_PAD - MLA_NOPE - MLA_ROPE)))
    wq = wq.reshape(depth, MLA_Q_RANK, MLA_HEADS * MLA_PAD)
    wkv = w_kv_up.reshape(depth, MLA_KV_RANK, MLA_HEADS, MLA_NOPE + MLA_V).astype(BF16)
    wk = wkv[..., :MLA_NOPE].reshape(depth, MLA_KV_RANK, MLA_HEADS * MLA_NOPE)
    wv = wkv[..., MLA_NOPE:].reshape(depth, MLA_KV_RANK, MLA_HEADS * MLA_V)
    w_conv_out_b = w_conv_out.astype(BF16)
    w_mla_out_b = w_mla_out.astype(BF16)
    w_gqa_out_b = w_gqa_out.astype(BF16)
    w_out_b = w_out.astype(BF16)
    w_ffn_in_b = w_ffn_in.astype(BF16)
    w_ffn_out_b = w_ffn_out.astype(BF16)
    cos, sin_up, sin_dn = _rope_tables(seq, ts)

    cc = jnp.zeros((16, d), F32).at[:bsz].set(c).at[bsz].set(c_ctx)
    mod = _ada(cc, w_ada, b_ada).reshape(depth, 16, 1, 6 * d)

    x_lat = x.reshape(m_lat, d)
    x_ctx = ctx.reshape(m_ctx, d)
    xa = None

    def vec(a):
        return a.reshape(depth, 1, a.shape[-1])

    g_mix3, g_ffn3, g_q3, g_kv3 = vec(g_mix), vec(g_ffn), vec(g_q_a), vec(g_kv_a)
    conv_b3, ln_g3, ln_b3, sink3 = vec(conv_b), vec(conv_ln_g), vec(conv_ln_b), vec(gqa_sink)

    def mod_spec(l, part, tile, width, with_j, tile0=0):
        nb = d // width
        if with_j:
            return pl.BlockSpec((None, None, 1, width),
                                lambda i, j: (l, mod_row(tile0 + i, tile), 0, part * nb + j))
        return pl.BlockSpec((None, None, 1, width), lambda i, j: (l, mod_row(tile0 + i, tile), 0, part))

    def table_specs(tile):
        per_batch = seq // tile
        n_lat = m_lat // tile
        return [pl.BlockSpec((tile, LANES), lambda i: (jnp.where(i < n_lat, i % per_batch, per_batch), 0))] * 3

    for l in range(depth):
        last = l == depth - 1
        m_rows = m_lat if last else m_all

        lat_src, lat_tile0 = (x_lat, 0) if l == 0 else (xa, 0)
        ctx_src, ctx_tile0 = (x_ctx, 0) if l == 0 else (xa, n_lat_tiles)

        def in_proj_call(src, src_tile0, row_tile0, n_row_tiles, col_tile0, n_col_tiles, prev):
            in_specs = [
                pl.BlockSpec((tm, d), lambda i, j: (src_tile0 + i, 0)),
                pl.BlockSpec((None, 1, d), lambda i, j: (l, 0, 0)),
                mod_spec(l, 0, tm, d, False, row_tile0),
                mod_spec(l, 1, tm, d, False, row_tile0),
                pl.BlockSpec((None, d, tn_in), lambda i, j: (l, 0, col_tile0 + j)),
            ]
            args = [src, g_mix3, mod, mod, w_in_r]
            aliases = {}
            if prev is not None:
                in_specs.append(pl.BlockSpec(memory_space=pl.ANY))
                args.append(prev)
                aliases = {5: 0}
            return pl.pallas_call(
                functools.partial(_in_proj_kernel, n_gate=p_base // tn_in, col0=col_tile0, rc=rc),
                grid=(n_row_tiles, n_col_tiles),
                in_specs=in_specs,
                out_specs=pl.BlockSpec((tm, tn_in), lambda i, j: (row_tile0 + i, col_tile0 + j)),
                out_shape=jax.ShapeDtypeStruct((m_all, p_cols), BF16),
                scratch_shapes=[pltpu.VMEM((tm, d), BF16)],
                input_output_aliases=aliases,
                compiler_params=_cparams("parallel", "arbitrary"),
                name="in_proj" if prev is None else "in_proj_ctx",
            )(*args)

        n_col_tiles = p_cols // tn_in
        p = in_proj_call(lat_src, lat_tile0, 0, n_lat_tiles, 0, n_col_tiles, None)
        if last:
            p = in_proj_call(ctx_src, ctx_tile0, n_lat_tiles, m_ctx // tm, n_col_tiles - 1, 1, p)
        else:
            p = in_proj_call(ctx_src, ctx_tile0, n_lat_tiles, m_ctx // tm, 0, n_col_tiles, p)

        q_mla, q_gqa = pl.pallas_call(
            functools.partial(_qup_kernel, rc=rc, s_mla=float(LOG2_E * (MLA_NOPE + MLA_ROPE) ** -0.5),
                              s_gqa=float(LOG2_E * GQA_HEAD_DIM ** -0.5)),
            grid=(m_rows // ts,),
            in_specs=[
                pl.BlockSpec((ts, 1024), lambda i: (i, (p_base + P_MLA_Q) // 1024)),
                pl.BlockSpec((None, 1, MLA_Q_RANK), lambda i: (l, 0, 0)),
                pl.BlockSpec((None, MLA_Q_RANK, MLA_HEADS * MLA_PAD), lambda i: (l, 0, 0)),
            ] + table_specs(ts),
            out_specs=[pl.BlockSpec((ts, MLA_HEADS * MLA_PAD), lambda i: (i, 0)),
                       pl.BlockSpec((ts, GQA_HEADS * GQA_HEAD_DIM), lambda i: (i, 0))],
            out_shape=[jax.ShapeDtypeStruct((m_all, MLA_HEADS * MLA_PAD), BF16),
                       jax.ShapeDtypeStruct((m_all, GQA_HEADS * GQA_HEAD_DIM), BF16)],
            compiler_params=_cparams("parallel"),
            name="q_up",
        )(p, g_q3, wq, cos, sin_up, sin_dn)

        tkv = n_ctx
        kv_per_batch = seq // tkv
        kv_lat_tiles = m_lat // tkv

        def kv_block(i):
            lat = (i // kv_per_batch) * (kv_per_batch + 1) + 1 + i % kv_per_batch
            return jnp.where(i < kv_lat_tiles, lat, (i - kv_lat_tiles) * (kv_per_batch + 1))

        k_mla, k_rope, v_mla, k_gqa, v_gqa = pl.pallas_call(
            functools.partial(_kvup_kernel, rc=tkv),
            grid=(m_all // tkv,),
            in_specs=[
                pl.BlockSpec((tkv, MLA_KV_RANK), lambda i: (i, (p_base + P_CKV) // MLA_KV_RANK)),
                pl.BlockSpec((tkv, 3 * LANES), lambda i: (i, (p_base + P_KR) // (3 * LANES))),
                pl.BlockSpec((None, 1, MLA_KV_RANK), lambda i: (l, 0, 0)),
                pl.BlockSpec((None, MLA_KV_RANK, MLA_HEADS * MLA_NOPE), lambda i: (l, 0, 0)),
                pl.BlockSpec((None, MLA_KV_RANK, MLA_HEADS * MLA_V), lambda i: (l, 0, 0)),
            ] + table_specs(tkv),
            out_specs=[pl.BlockSpec((tkv, MLA_HEADS * MLA_NOPE), lambda i: (kv_block(i), 0)),
                       pl.BlockSpec((tkv, LANES), lambda i: (kv_block(i), 0)),
                       pl.BlockSpec((tkv, MLA_HEADS * MLA_V), lambda i: (kv_block(i), 0)),
                       pl.BlockSpec((tkv, 2 * LANES), lambda i: (i, 0)),
                       pl.BlockSpec((tkv, 2 * LANES), lambda i: (i, 0))],
            out_shape=[jax.ShapeDtypeStruct((m_all, MLA_HEADS * MLA_NOPE), BF16),
                       jax.ShapeDtypeStruct((m_all, LANES), BF16),
                       jax.ShapeDtypeStruct((m_all, MLA_HEADS * MLA_V), BF16),
                       jax.ShapeDtypeStruct((m_all, 2 * LANES), BF16),
                       jax.ShapeDtypeStruct((m_all, 2 * LANES), BF16)],
            compiler_params=_cparams("parallel"),
            name="kv_up",
        )(p, p, g_kv3, wk, wv, cos, sin_up, sin_dn)

        tq = min(MLA_TQ, seq)
        nq = seq // tq
        def mla_scratch(rows, tk):
            return ([pltpu.VMEM((rows, tk), F32)] * MLA_UNROLL + [pltpu.VMEM((rows, tk), BF16)] * MLA_UNROLL
                    + [pltpu.VMEM((rows, LANES), F32)] * (MLA_UNROLL + 1) + [pltpu.VMEM((rows, MLA_PAD), F32)])

        o_mla = pl.pallas_call(
            functools.partial(_mla_attn_kernel, tk=MLA_TK, rg=MLA_RG, unroll=MLA_UNROLL),
            grid=(bsz, MLA_HEADS, nq),
            scratch_shapes=mla_scratch(tq, MLA_TK),
            in_specs=[
                pl.BlockSpec((tq, MLA_PAD), lambda b, h, i: (b * nq + i, h)),
                pl.BlockSpec((seq + n_ctx, MLA_NOPE), lambda b, h, i: (b, h)),
                pl.BlockSpec((seq + n_ctx, LANES), lambda b, h, i: (b, 0)),
                pl.BlockSpec((seq + n_ctx, MLA_V), lambda b, h, i: (b, h)),
            ],
            out_specs=pl.BlockSpec((tq, MLA_V), lambda b, h, i: (b * nq + i, h)),
            out_shape=jax.ShapeDtypeStruct((m_all, MLA_HEADS * MLA_V), BF16),
            compiler_params=_cparams("parallel", "parallel", "arbitrary"),
            name="mla_attn",
        )(q_mla, k_mla, k_rope, v_mla)
        o_mla = o_mla if last else pl.pallas_call(
            functools.partial(_mla_attn_kernel, tk=n_ctx, rg=MLA_RG, unroll=MLA_UNROLL),
            grid=(bsz, MLA_HEADS),
            scratch_shapes=mla_scratch(n_ctx, n_ctx),
            in_specs=[
                pl.BlockSpec((n_ctx, MLA_PAD), lambda b, h: (ctx_blk0 + b, h)),
                pl.BlockSpec((n_ctx, MLA_NOPE), lambda b, h: (b * (kv_per_batch + 1), h)),
                pl.BlockSpec((n_ctx, LANES), lambda b, h: (b * (kv_per_batch + 1), 0)),
                pl.BlockSpec((n_ctx, MLA_V), lambda b, h: (b * (kv_per_batch + 1), h)),
                pl.BlockSpec(memory_space=pl.ANY),
            ],
            out_specs=pl.BlockSpec((n_ctx, MLA_V), lambda b, h: (ctx_blk0 + b, h)),
            out_shape=jax.ShapeDtypeStruct((m_all, MLA_HEADS * MLA_V), BF16),
            input_output_aliases={4: 0},
            compiler_params=_cparams("parallel", "parallel"),
            name="mla_attn_ctx",
        )(q_mla, k_mla, k_rope, v_mla, o_mla)

        def gqa_scratch(cols):
            return ([pltpu.VMEM((GQA_HEADS, WINDOW, cols), F32)] * 2
                    + [pltpu.VMEM((GQA_HEADS, WINDOW, cols), BF16)] * 2
                    + [pltpu.VMEM((GQA_HEADS, WINDOW, LANES), F32)] * 2)

        o_gqa = pl.pallas_call(
            functools.partial(_gqa_attn_kernel, seq=seq, n_sub=seq // WINDOW, latent=True),
            grid=(bsz,),
            in_specs=[
                pl.BlockSpec((seq, GQA_HEADS * GQA_HEAD_DIM), lambda b: (b, 0)),
                pl.BlockSpec((n_ctx, 2 * LANES), lambda b: (ctx_blk0 + b, 0)),
                pl.BlockSpec((n_ctx, 2 * LANES), lambda b: (ctx_blk0 + b, 0)),
                pl.BlockSpec((None, 1, GQA_HEADS), lambda b: (l, 0, 0)),
                pl.BlockSpec((seq, 2 * LANES), lambda b: (b, 0)),
                pl.BlockSpec((seq, 2 * LANES), lambda b: (b, 0)),
            ],
            out_specs=pl.BlockSpec((seq, GQA_HEADS * GQA_HEAD_DIM), lambda b: (b, 0)),
            out_shape=jax.ShapeDtypeStruct((m_all, GQA_HEADS * GQA_HEAD_DIM), BF16),
            scratch_shapes=gqa_scratch(n_ctx + 3 * WINDOW),
            compiler_params=_cparams("parallel"),
            name="gqa_attn",
        )(q_gqa, k_gqa, v_gqa, sink3, k_gqa, v_gqa)
        o_gqa = o_gqa if last else pl.pallas_call(
            functools.partial(_gqa_attn_kernel, seq=n_ctx, n_sub=n_ctx // WINDOW, latent=False),
            grid=(bsz,),
            scratch_shapes=gqa_scratch(n_ctx),
            in_specs=[
                pl.BlockSpec((n_ctx, GQA_HEADS * GQA_HEAD_DIM), lambda b: (ctx_blk0 + b, 0)),
                pl.BlockSpec((n_ctx, 2 * LANES), lambda b: (ctx_blk0 + b, 0)),
                pl.BlockSpec((n_ctx, 2 * LANES), lambda b: (ctx_blk0 + b, 0)),
                pl.BlockSpec((None, 1, GQA_HEADS), lambda b: (l, 0, 0)),
                pl.BlockSpec(memory_space=pl.ANY),
            ],
            out_specs=pl.BlockSpec((n_ctx, GQA_HEADS * GQA_HEAD_DIM), lambda b: (ctx_blk0 + b, 0)),
            out_shape=jax.ShapeDtypeStruct((m_all, GQA_HEADS * GQA_HEAD_DIM), BF16),
            input_output_aliases={4: 0},
            compiler_params=_cparams("parallel"),
            name="gqa_attn_ctx",
        )(q_gqa, k_gqa, v_gqa, sink3, o_gqa)

        def conv_call(length, blk0, prev):
            crc = 64
            in_specs = [
                pl.BlockSpec((length, CONV_CH), lambda b: (blk0 + b, (p_base + P_CONV_A) // CONV_CH)),
                pl.BlockSpec((length, CONV_CH), lambda b: (blk0 + b, (p_base + P_CONV_B) // CONV_CH)),
                pl.BlockSpec((None, CONV_WIDTH, CONV_CH), lambda b: (l, 0, 0)),
                pl.BlockSpec((None, 1, CONV_CH), lambda b: (l, 0, 0)),
                pl.BlockSpec((None, 1, CONV_CH), lambda b: (l, 0, 0)),
                pl.BlockSpec((None, 1, CONV_CH), lambda b: (l, 0, 0)),
            ]
            args = [p, p, conv_w, conv_b3, ln_g3, ln_b3]
            aliases = {}
            if prev is not None:
                in_specs.append(pl.BlockSpec(memory_space=pl.ANY))
                args.append(prev)
                aliases = {6: 0}

            def body(a_ref, b_ref, w_ref, cb_ref, lg_ref, lb_ref, *rest):
                o_ref, u_ref = rest[-2:]
                _conv_kernel(a_ref, b_ref, w_ref, cb_ref, lg_ref, lb_ref, o_ref, u_ref, seq=length, rc=crc)

            return pl.pallas_call(
                body,
                grid=(bsz,),
                in_specs=in_specs,
                out_specs=pl.BlockSpec((length, CONV_CH), lambda b: (blk0 + b, 0)),
                out_shape=jax.ShapeDtypeStruct((m_all, CONV_CH), BF16),
                scratch_shapes=[pltpu.VMEM((length + 32, CONV_CH), F32)],
                input_output_aliases=aliases,
                compiler_params=_cparams("parallel"),
                name="conv_branch" if prev is None else "conv_branch_ctx",
            )(*args)

        o_conv = conv_call(seq, 0, None)
        if not last:
            o_conv = conv_call(n_ctx, ctx_blk0, o_conv)

        y = pl.pallas_call(
            _merge_kernel,
            grid=(m_rows // ts, 1),
            in_specs=[
                pl.BlockSpec((ts, CONV_CH), lambda i, j: (i, 0)),
                pl.BlockSpec((ts, MLA_HEADS * MLA_V), lambda i, j: (i, 0)),
                pl.BlockSpec((ts, GQA_HEADS * GQA_HEAD_DIM), lambda i, j: (i, 0)),
                pl.BlockSpec((None, CONV_CH, d), lambda i, j: (l, 0, 0)),
                pl.BlockSpec((None, MLA_HEADS * MLA_V, d), lambda i, j: (l, 0, 0)),
                pl.BlockSpec((None, GQA_HEADS * GQA_HEAD_DIM, d), lambda i, j: (l, 0, 0)),
                pl.BlockSpec((ts, d), lambda i, j: (i, 0)),
                pl.BlockSpec((ts, d), lambda i, j: (i, 1)),
                pl.BlockSpec((ts, d), lambda i, j: (i, 2)),
            ],
            out_specs=pl.BlockSpec((ts, d), lambda i, j: (i, 0)),
            out_shape=jax.ShapeDtypeStruct((m_all, d), BF16),
            compiler_params=_cparams("parallel", "parallel"),
            name="merge",
        )(o_conv, o_mla, o_gqa, w_conv_out_b, w_mla_out_b, w_gqa_out_b, p, p, p)

        def mm_res(a, w, xin, part, tile, width, name):
            kdim = a.shape[1]
            return pl.pallas_call(
                _mm_res_kernel,
                grid=(m_rows // tile, d // width),
                in_specs=[
                    pl.BlockSpec((tile, kdim), lambda i, j: (i, 0)),
                    pl.BlockSpec((None, kdim, width), lambda i, j: (l, 0, j)),
                    pl.BlockSpec((tile, width), lambda i, j: (i, j)),
                    mod_spec(l, part, tile, width, True),
                ],
                out_specs=pl.BlockSpec((tile, width), lambda i, j: (i, j)),
                out_shape=jax.ShapeDtypeStruct((m_all, d), F32),
                compiler_params=_cparams("parallel", "parallel"),
                name=name,
            )(a, w, xin, mod)

        def out_proj_call(res_src, res_tile0, row_tile0, n_row_tiles, prev):
            in_specs = [
                pl.BlockSpec((ts, d), lambda i, j: (row_tile0 + i, 0)),
                pl.BlockSpec((None, d, d), lambda i, j: (l, 0, 0)),
                pl.BlockSpec((ts, d), lambda i, j: (res_tile0 + i, 0)),
                mod_spec(l, 2, ts, d, False, row_tile0),
                pl.BlockSpec((None, 1, d), lambda i, j: (l, 0, 0)),
                mod_spec(l, 3, ts, d, False, row_tile0),
                mod_spec(l, 4, ts, d, False, row_tile0),
            ]
            args = [y, w_out_b, res_src, mod, g_ffn3, mod, mod]
            aliases = {}
            if prev is not None:
                in_specs += [pl.BlockSpec(memory_space=pl.ANY)] * 2
                args += list(prev)
                aliases = {7: 0, 8: 1}
            return pl.pallas_call(
                _out_proj_kernel,
                grid=(n_row_tiles, 1),
                in_specs=in_specs,
                out_specs=[pl.BlockSpec((ts, d), lambda i, j: (row_tile0 + i, 0))] * 2,
                out_shape=[jax.ShapeDtypeStruct((m_all, d), F32), jax.ShapeDtypeStruct((m_all, d), BF16)],
                input_output_aliases=aliases,
                compiler_params=_cparams("parallel", "parallel"),
                name="out_proj" if prev is None else "out_proj_ctx",
            )(*args)

        ts_per_tm = tm // ts
        xa, h_ffn = out_proj_call(lat_src, lat_tile0 * ts_per_tm, 0, m_lat // ts, None)
        if not last:
            xa, h_ffn = out_proj_call(ctx_src, ctx_tile0 * ts_per_tm, m_lat // ts, m_ctx // ts, (xa, h_ffn))

        nf = d_ff // tn
        act = pl.pallas_call(
            _ffn_in_kernel,
            grid=(m_rows // tm, nf),
            in_specs=[
                pl.BlockSpec((tm, d), lambda i, j: (i, 0)),
                pl.BlockSpec((None, d, tn), lambda i, j: (l, 0, j)),
                pl.BlockSpec((None, d, tn), lambda i, j: (l, 0, nf + j)),
            ],
            out_specs=pl.BlockSpec((tm, tn), lambda i, j: (i, j)),
            out_shape=jax.ShapeDtypeStruct((m_all, d_ff), BF16),
            compiler_params=_cparams("parallel", "parallel"),
            name="ffn_in",
        )(h_ffn, w_ffn_in_b, w_ffn_in_b)
        if not last:
            xa = mm_res(act, w_ffn_out_b, xa, 5, tm, tn, "ffn_out")

    out = pl.pallas_call(
        functools.partial(_ffn_out_final_kernel, rc=rc),
        grid=(m_lat // ts, 2),
        in_specs=[
            pl.BlockSpec((ts, d_ff // 2), lambda i, k: (i, k)),
            pl.BlockSpec((None, d_ff // 2, d), lambda i, k: (depth - 1, k, 0)),
            pl.BlockSpec((ts, d), lambda i, k: (i, 0)),
            mod_spec(depth - 1, 5, ts, d, False),
            pl.BlockSpec((1, d), lambda i, k: (0, 0)),
        ],
        out_specs=pl.BlockSpec((ts, d), lambda i, k: (i, 0)),
        out_shape=jax.ShapeDtypeStruct((m_lat, d), F32),
        scratch_shapes=[pltpu.VMEM((ts, d), F32)],
        compiler_params=_cparams("parallel", "arbitrary"),
        name="ffn_out_final",
    )(act, w_ffn_out_b, xa, mod, g_final.reshape(1, d))
    return out.reshape(bsz, seq, d)
```
